```python
import math
import jax
import jax.numpy as jnp
from jax import lax
import numpy as np

D_MODEL = 1024
BATCH = 8
SEQ = 8192
DEPTH = 4

GRID_W = 64
CTX_LEN = 256
N_MOD = 9
D_FF = 2816
FFN_RES = 0.5
EPS = 1e-6
ROPE_THETA = 10000.0
Q_BLOCK = 128

CONV_WIDTH = 3
D_CONV = 512
D_FOURIER = 512
FOURIER_GROUPS = 4
D_FG = D_FOURIER // FOURIER_GROUPS
D_IN_EVEN = 3 * D_CONV + D_FOURIER
D_OUT_EVEN = D_CONV + D_FOURIER

DA_HEADS = 8
DA_DK = 64
DA_DV = 2 * DA_DK
DA_SCALE = DA_DK ** -0.5
MLA_HEADS = 8
MLA_NOPE = 64
MLA_ROPE = 32
MLA_DQK = MLA_NOPE + MLA_ROPE
MLA_DV = 64
MLA_Q_RANK = 384
MLA_KV_RANK = 256
MLA_SCALE = MLA_DQK ** -0.5
DA_QCOLS = DA_HEADS * 2 * DA_DK
DA_VCOLS = DA_HEADS * DA_DV
Q_COLS = DA_QCOLS + MLA_Q_RANK
KV_COLS = DA_QCOLS + DA_VCOLS + MLA_KV_RANK + MLA_ROPE
D_IN_ODD = Q_COLS + KV_COLS
D_OUT_ODD = DA_HEADS * DA_DV + MLA_HEADS * MLA_DV

kernel_name = 'hybrid_conv_fourier_diffattn_mla_dit'


def rmsnorm(x, g):
    x32 = x.astype(jnp.float32)
    y = x32 * lax.rsqrt(jnp.mean(x32 * x32, axis=-1, keepdims=True) + EPS)
    return (y * g.astype(jnp.float32)).astype(x.dtype)


def modulate(h, g, shift, scale):
    return rmsnorm(h, g) * (1 + scale) + shift


def adaln_params(cond, w, b):
    m = jax.nn.silu(cond) @ w + b
    return m.reshape(cond.shape[0], 1, N_MOD, cond.shape[-1])


def swiglu(x, w_in, w_out):
    gate, up = jnp.split(x @ w_in, 2, axis=-1)
    return (jax.nn.silu(gate) * up) @ w_out


def macaron_ffn(h, m, g, w_in, w_out, slot):
    xm = modulate(h, g[2 * slot], m[:, :, 3 * slot], m[:, :, 3 * slot + 1])
    y = swiglu(xm, w_in, w_out)
    return h + FFN_RES * m[:, :, 3 * slot + 2] * rmsnorm(y, g[2 * slot + 1])


def rope_tables(n_tokens, rot_dim):
    rows = n_tokens // GRID_W
    row = jnp.broadcast_to(jnp.arange(rows)[:, None], (rows, GRID_W)).reshape(-1)
    col = jnp.broadcast_to(jnp.arange(GRID_W)[None, :], (rows, GRID_W)).reshape(-1)
    n_freq = rot_dim // 4
    freqs = ROPE_THETA ** (-jnp.arange(n_freq, dtype=jnp.float32) / n_freq)
    ang = jnp.stack([row.astype(jnp.float32)[:, None] * freqs,
                     col.astype(jnp.float32)[:, None] * freqs], axis=1)
    ang = ang[None, :, None, :, None, :]
    return jnp.cos(ang), jnp.sin(ang)


def apply_rope(x, cos, sin):
    b_, n, h, r = x.shape
    xr = x.astype(jnp.float32).reshape(b_, n, h, 2, 2, r // 4)
    x1, x2 = xr[..., 0:1, :], xr[..., 1:2, :]
    out = jnp.concatenate([x1 * cos - x2 * sin, x2 * cos + x1 * sin], axis=-2)
    return out.reshape(b_, n, h, r).astype(x.dtype)


def short_conv_fourier_mixer(xm, w_in, conv_w, w_out):
    b_, t = xm.shape[:2]
    u = xm @ w_in
    gate_b, gate_c, xv, xf = jnp.split(u, [D_CONV, 2 * D_CONV, 3 * D_CONV], axis=-1)
    pad = CONV_WIDTH // 2
    z = jnp.pad(gate_c * xv, ((0, 0), (pad, pad), (0, 0)))
    conv = z[:, 0:t] * conv_w[0]
    for k in range(1, CONV_WIDTH):
        conv = conv + z[:, k:k + t] * conv_w[k]
    y_conv = gate_b * conv
    xf = xf.astype(jnp.float32).reshape(b_, t, FOURIER_GROUPS, D_FG)
    y_four = jnp.real(jnp.fft.fft2(xf, axes=(1, 3), norm='ortho'))
    y_four = y_four.reshape(b_, t, D_FOURIER).astype(xm.dtype)
    return jnp.concatenate([y_conv, y_four], axis=-1) @ w_out


def attn_queries(u_q, g_q, w_uq, rope_da, rope_mla):
    b_, t = u_q.shape[:2]
    q_da, c_q = jnp.split(u_q, [DA_QCOLS], axis=-1)
    q_da = q_da.reshape(b_, t, DA_HEADS * 2, DA_DK)
    q_m = (rmsnorm(c_q, g_q) @ w_uq).reshape(b_, t, MLA_HEADS, MLA_DQK)
    q_nope, q_rope = q_m[..., :MLA_NOPE], q_m[..., MLA_NOPE:]
    if rope_da is not None:
        q_da = apply_rope(q_da, *rope_da)
        q_rope = apply_rope(q_rope, *rope_mla)
    q_da = q_da.reshape(b_, t, DA_HEADS, 2, DA_DK)
    return q_da, jnp.concatenate([q_nope, q_rope], axis=-1)


def attn_keys_values(u_kv, g_kv, w_uk, w_uv, rope_da, rope_mla):
    b_, t = u_kv.shape[:2]
    k_da, v_da, c_kv, k_r = jnp.split(
        u_kv, [DA_QCOLS, DA_QCOLS + DA_VCOLS, DA_QCOLS + DA_VCOLS + MLA_KV_RANK], axis=-1)
    k_da = k_da.reshape(b_, t, DA_HEADS * 2, DA_DK)
    v_da = v_da.reshape(b_, t, DA_HEADS, DA_DV)
    c_kv = rmsnorm(c_kv, g_kv)
    k_nope = (c_kv @ w_uk).reshape(b_, t, MLA_HEADS, MLA_NOPE)
    v_m = (c_kv @ w_uv).reshape(b_, t, MLA_HEADS, MLA_DV)
    k_r = k_r.reshape(b_, t, 1, MLA_ROPE)
    if rope_da is not None:
        k_da = apply_rope(k_da, *rope_da)
        k_r = apply_rope(k_r, *rope_mla)
    k_da = k_da.reshape(b_, t, DA_HEADS, 2, DA_DK)
    k_m = jnp.concatenate([k_nope, jnp.broadcast_to(k_r, (b_, t, MLA_HEADS, MLA_ROPE))], axis=-1)
    return k_da, v_da, k_m, v_m


def attend(q_da, q_m, k_da, v_da, k_m, v_m, lam):
    f32 = jnp.float32
    s = jnp.einsum('bqhmd,bkhmd->bhmqk', q_da, k_da).astype(f32) * DA_SCALE
    p = jax.nn.softmax(s, axis=-1)
    w_diff = p[:, :, 0] - lam * p[:, :, 1]
    o_da = jnp.einsum('bhqk,bkhd->bqhd', w_diff, v_da.astype(f32))
    sm = jnp.einsum('bqhd,bkhd->bhqk', q_m, k_m).astype(f32) * MLA_SCALE
    pm = jax.nn.softmax(sm, axis=-1)
    o_m = jnp.einsum('bhqk,bkhd->bqhd', pm, v_m.astype(f32))
    return o_da.astype(q_da.dtype), o_m.astype(q_m.dtype)


def attend_latent_blocks(qs, kvs, lam):
    b_, t = qs[0].shape[:2]
    nb = t // Q_BLOCK
    blocks = tuple(jnp.moveaxis(q.reshape(b_, nb, Q_BLOCK, *q.shape[2:]), 1, 0) for q in qs)
    o_da, o_m = lax.map(lambda qb: attend(qb[0], qb[1], *kvs, lam), blocks)
    unblock = lambda o: jnp.moveaxis(o, 0, 1).reshape(b_, t, *o.shape[3:])
    return unblock(o_da), unblock(o_m)


def attn_output(o_da, o_m, g_sub, lam_init, w_out):
    b_, t = o_da.shape[:2]
    o_da = (rmsnorm(o_da, g_sub) * (1 - lam_init)).reshape(b_, t, DA_HEADS * DA_DV)
    return jnp.concatenate([o_da, o_m.reshape(b_, t, MLA_HEADS * MLA_DV)], axis=-1) @ w_out


def setup_inputs(seed: int = 0) -> dict:
    key = jax.random.key(seed)
    ks = iter(jax.random.split(key, 32))
    f32 = jnp.float32
    n_even = (DEPTH + 1) // 2
    n_odd = DEPTH // 2

    def nrm(shape, scale):
        return jax.random.normal(next(ks), shape, f32) * scale

    def gain(shape):
        return 1.0 + nrm(shape, 0.05)

    return {
        'x': nrm((BATCH, SEQ, D_MODEL), 1.0),
        'c': nrm((BATCH, D_MODEL), 1.0),
        'ctx': nrm((BATCH, CTX_LEN, D_MODEL), 1.0),
        'c_ctx': nrm((D_MODEL,), 1.0),
        'w_mod': nrm((DEPTH, D_MODEL, N_MOD * D_MODEL), 0.5 * D_MODEL ** -0.5),
        'b_mod': nrm((DEPTH, N_MOD * D_MODEL), 0.01),
        'norm_g': gain((DEPTH, 6, D_MODEL)),
        'w_ffn_in': nrm((DEPTH, 2, D_MODEL, 2 * D_FF), D_MODEL ** -0.5),
        'w_ffn_out': nrm((DEPTH, 2, D_FF, D_MODEL), D_FF ** -0.5),
        'w_in_even': nrm((n_even, D_MODEL, D_IN_EVEN), D_MODEL ** -0.5),
        'conv_w': nrm((n_even, CONV_WIDTH, D_CONV), CONV_WIDTH ** -0.5),
        'w_out_even': nrm((n_even, D_OUT_EVEN, D_MODEL), D_OUT_EVEN ** -0.5),
        'w_in_odd': nrm((n_odd, D_MODEL, D_IN_ODD), D_MODEL ** -0.5),
        'g_q_mla': gain((n_odd, MLA_Q_RANK)),
        'w_uq': nrm((n_odd, MLA_Q_RANK, MLA_HEADS * MLA_DQK), MLA_Q_RANK ** -0.5),
        'g_kv_mla': gain((n_odd, MLA_KV_RANK)),
        'w_uk': nrm((n_odd, MLA_KV_RANK, MLA_HEADS * MLA_NOPE), MLA_KV_RANK ** -0.5),
        'w_uv': nrm((n_odd, MLA_KV_RANK, MLA_HEADS * MLA_DV), MLA_KV_RANK ** -0.5),
        'lam_q1': nrm((n_odd, DA_DK), 0.1),
        'lam_k1': nrm((n_odd, DA_DK), 0.1),
        'lam_q2': nrm((n_odd, DA_DK), 0.1),
        'lam_k2': nrm((n_odd, DA_DK), 0.1),
        'g_subln': gain((n_odd, DA_DV)),
        'w_out_odd': nrm((n_odd, D_OUT_ODD, D_MODEL), D_OUT_ODD ** -0.5),
    }


def reference(x, c, ctx, c_ctx, w_mod, b_mod, norm_g, w_ffn_in, w_ffn_out,
              w_in_even, conv_w, w_out_even,
              w_in_odd, g_q_mla, w_uq, g_kv_mla, w_uk, w_uv,
              lam_q1, lam_k1, lam_q2, lam_k2, g_subln, w_out_odd):
    f32 = jnp.float32
    n_lat = x.shape[1]
    rope_da = rope_tables(n_lat, DA_DK)
    rope_mla = rope_tables(n_lat, MLA_ROPE)
    h, hc = x, ctx
    for l in range(DEPTH):
        last = l == DEPTH - 1
        odd = l % 2 == 1
        ctx_live = (not last) or odd
        g = norm_g[l]
        m_x = adaln_params(c, w_mod[l], b_mod[l])
        h = macaron_ffn(h, m_x, g, w_ffn_in[l, 0], w_ffn_out[l, 0], 0)
        if ctx_live:
            m_c = adaln_params(c_ctx[None, :], w_mod[l], b_mod[l])
            hc = macaron_ffn(hc, m_c, g, w_ffn_in[l, 0], w_ffn_out[l, 0], 0)

        if not odd:
            e = l // 2
            xm = modulate(h, g[2], m_x[:, :, 3], m_x[:, :, 4])
            y = short_conv_fourier_mixer(xm, w_in_even[e], conv_w[e], w_out_even[e])
            h = h + m_x[:, :, 5] * rmsnorm(y, g[3])
            if ctx_live:
                xc = modulate(hc, g[2], m_c[:, :, 3], m_c[:, :, 4])
                yc = short_conv_fourier_mixer(xc, w_in_even[e], conv_w[e], w_out_even[e])
                hc = hc + m_c[:, :, 5] * rmsnorm(yc, g[3])
        else:
            o = l // 2
            lam_init = 0.8 - 0.6 * math.exp(-0.3 * l)
            lam = (jnp.exp(jnp.sum(lam_q1[o].astype(f32) * lam_k1[o].astype(f32)))
                   - jnp.exp(jnp.sum(lam_q2[o].astype(f32) * lam_k2[o].astype(f32))) + lam_init)
            w_in = w_in_odd[o]
            xm = modulate(h, g[2], m_x[:, :, 3], m_x[:, :, 4])
            xc = modulate(hc, g[2], m_c[:, :, 3], m_c[:, :, 4])
            u = xm @ w_in
            q_x = attn_queries(u[..., :Q_COLS], g_q_mla[o], w_uq[o], rope_da, rope_mla)
            kv_x = attn_keys_values(u[..., Q_COLS:], g_kv_mla[o], w_uk[o], w_uv[o], rope_da, rope_mla)
            kv_c = attn_keys_values(xc @ w_in[:, Q_COLS:], g_kv_mla[o], w_uk[o], w_uv[o], None, None)
            kv_all = tuple(jnp.concatenate([kc, kx], axis=1) for kc, kx in zip(kv_c, kv_x))
            o_da, o_m = attend_latent_blocks(q_x, kv_all, lam)
            y = attn_output(o_da, o_m, g_subln[o], lam_init, w_out_odd[o])
            if not last:
                q_c = attn_queries(xc @ w_in[:, :Q_COLS], g_q_mla[o], w_uq[o], None, None)
                oc_da, oc_m = attend(q_c[0], q_c[1], *kv_c, lam)
                yc = attn_output(oc_da, oc_m, g_subln[o], lam_init, w_out_odd[o])
                hc = hc + m_c[:, :, 5] * rmsnorm(yc, g[3])
            h = h + m_x[:, :, 5] * rmsnorm(y, g[3])

        h = macaron_ffn(h, m_x, g, w_ffn_in[l, 1], w_ffn_out[l, 1], 2)
        if not last:
            hc = macaron_ffn(hc, m_c, g, w_ffn_in[l, 1], w_ffn_out[l, 1], 2)
    return h
```

```python
import functools
import math

import jax
import jax.numpy as jnp
from jax import lax
from jax.experimental import pallas as pl
from jax.experimental.pallas import tpu as pltpu

F32 = jnp.float32
BF16 = jnp.bfloat16

D_MODEL = 1024
GRID_W = 64
N_MOD = 9
FFN_RES = 0.5
EPS = 1e-6
ROPE_THETA = 10000.0

D_CONV = 512
D_FOURIER = 512
FOURIER_GROUPS = 4
D_FG = D_FOURIER // FOURIER_GROUPS

DA_HEADS = 8
DA_DK = 64
DA_DV = 128
DA_SCALE = DA_DK ** -0.5
MLA_HEADS = 8
MLA_NOPE = 64
MLA_ROPE = 32
MLA_DQK = MLA_NOPE + MLA_ROPE
MLA_DV = 64
MLA_Q_RANK = 384
MLA_KV_RANK = 256
MLA_SCALE = MLA_DQK ** -0.5
DA_QCOLS = DA_HEADS * 2 * DA_DK
DA_VCOLS = DA_HEADS * DA_DV
Q_COLS = DA_QCOLS + MLA_Q_RANK
HEAD_PAD = 128
N_MAPS = 2 * DA_HEADS + MLA_HEADS
LOG2E = 1.4426950408889634
NEG_BIG = -1e30

VMEM_LIMIT_V7X = 56 * 1024 * 1024


def _cparams(sem):
    return pltpu.CompilerParams(dimension_semantics=sem, vmem_limit_bytes=VMEM_LIMIT_V7X)


def _tile(n, pref):
    if n <= pref:
        return n
    t = pref - pref % 128
    while t >= 128:
        if n % t == 0:
            return t
        t -= 128
    return n


def _const_spec(shape):
    nd = len(shape)
    return pl.BlockSpec(shape, lambda *_: (0,) * nd, pipeline_mode=pl.Buffered(1))


def _mod_spec(mod):
    if mod.shape[0] == 1:
        return pl.BlockSpec((1, N_MOD, D_MODEL), lambda b, *_: (0, 0, 0))
    return pl.BlockSpec((1, N_MOD, D_MODEL), lambda b, *_: (b, 0, 0))


def _rms_rows(x, g):
    ms = jnp.mean(x * x, axis=-1, keepdims=True)
    return x * lax.rsqrt(ms + EPS) * g


def _modulated(x, mod, g, slot):
    shift = mod[3 * slot:3 * slot + 1]
    scale = mod[3 * slot + 1:3 * slot + 2]
    return _rms_rows(x, g[2 * slot:2 * slot + 1]) * (1.0 + scale) + shift


def _dot(a, b):
    return jnp.dot(a, b, preferred_element_type=F32)


def _dot_nt(a, b):
    return lax.dot_general(a, b, (((1,), (1,)), ((), ())), preferred_element_type=F32)


def _adaln_kernel(c_ref, w_ref, b_ref, o_ref):
    c = c_ref[...]
    a = c * jax.nn.sigmoid(c)
    o_ref[0] = jnp.dot(a, w_ref[0], preferred_element_type=F32,
                       precision=lax.Precision.HIGHEST) + b_ref[0]


def _adaln(cond, w_mod, b_mod):
    depth, d, n = w_mod.shape
    rows = cond.shape[0]
    tn = _tile(n, 1152)
    return pl.pallas_call(
        _adaln_kernel,
        grid=(depth, n // tn),
        in_specs=[pl.BlockSpec((rows, d), lambda l, j: (0, 0)),
                  pl.BlockSpec((1, d, tn), lambda l, j: (l, 0, j)),
                  pl.BlockSpec((1, 1, tn), lambda l, j: (l, 0, j))],
        out_specs=pl.BlockSpec((1, rows, tn), lambda l, j: (l, 0, j)),
        out_shape=jax.ShapeDtypeStruct((depth, rows, n), F32),
        compiler_params=_cparams(("parallel", "parallel")),
        name="adaln",
    )(cond, w_mod, b_mod.reshape(depth, 1, n))


def _ffn_kernel(h_ref, mod_ref, g_ref, win_ref, wout_ref, o_ref, *, slot, d_ff):
    x = h_ref[0]
    mod = mod_ref[0]
    g = g_ref[...]
    xm = _modulated(x, mod, g, slot).astype(BF16)
    gate = _dot(xm, win_ref[:, :d_ff])
    up = _dot(xm, win_ref[:, d_ff:])
    act = (gate * jax.nn.sigmoid(gate) * up).astype(BF16)
    y = _dot(act, wout_ref[...])
    res_gate = mod[3 * slot + 2:3 * slot + 3]
    o_ref[0] = x + FFN_RES * res_gate * _rms_rows(y, g[2 * slot + 1:2 * slot + 2])


def _ffn(h, mod, g, w_in, w_out, slot):
    b, t, d = h.shape
    d_ff = w_out.shape[0]
    tm = _tile(t, 512)
    return pl.pallas_call(
        functools.partial(_ffn_kernel, slot=slot, d_ff=d_ff),
        grid=(b, t // tm),
        in_specs=[pl.BlockSpec((1, tm, d), lambda bb, i: (bb, i, 0)),
                  _mod_spec(mod),
                  _const_spec(g.shape),
                  _const_spec(w_in.shape),
                  _const_spec(w_out.shape)],
        out_specs=pl.BlockSpec((1, tm, d), lambda bb, i: (bb, i, 0)),
        out_shape=jax.ShapeDtypeStruct(h.shape, F32),
        compiler_params=_cparams(("parallel", "parallel")),
        name="ffn",
    )(h, mod, g, w_in, w_out)


def _even_in_kernel(h_ref, mod_ref, g_ref, w_ref, dft_ref, gb_ref, z_ref, a_ref):
    xm = _modulated(h_ref[0], mod_ref[0], g_ref[...], 1).astype(BF16)
    u = _dot(xm, w_ref[...])
    gb_ref[0] = u[:, :D_CONV].astype(BF16)
    z_ref[0] = (u[:, D_CONV:2 * D_CONV] * u[:, 2 * D_CONV:3 * D_CONV]).astype(BF16)
    xf = u[:, 3 * D_CONV:].astype(BF16)
    for gi in range(FOURIER_GROUPS):
        pq = _dot(xf[:, gi * D_FG:(gi + 1) * D_FG], dft_ref[...])
        a_ref[0, :, gi * D_FG:(gi + 1) * D_FG] = pq[:, :D_FG].astype(BF16)
        a_ref[1, :, gi * D_FG:(gi + 1) * D_FG] = pq[:, D_FG:].astype(BF16)


def _even_in(h, mod, g, w_in, dft_c):
    b, t, d = h.shape
    tm = _tile(t, 512)
    return pl.pallas_call(
        _even_in_kernel,
        grid=(b, t // tm),
        in_specs=[pl.BlockSpec((1, tm, d), lambda bb, i: (bb, i, 0)),
                  _mod_spec(mod),
                  _const_spec(g.shape),
                  _const_spec(w_in.shape),
                  _const_spec(dft_c.shape)],
        out_specs=[pl.BlockSpec((1, tm, D_CONV), lambda bb, i: (bb, i, 0)),
                   pl.BlockSpec((1, tm, D_CONV), lambda bb, i: (bb, i, 0)),
                   pl.BlockSpec((2, tm, D_FOURIER), lambda bb, i: (0, i, bb))],
        out_shape=[jax.ShapeDtypeStruct((b, t, D_CONV), BF16),
                   jax.ShapeDtypeStruct((b, t, D_CONV), BF16),
                   jax.ShapeDtypeStruct((2, t, b * D_FOURIER), BF16)],
        compiler_params=_cparams(("parallel", "parallel")),
        name="even_in",
    )(h, mod, g, w_in, dft_c)


def _matmul_kernel(a_ref, b_ref, o_ref, acc_ref, *, scale):
    k = pl.program_id(2)

    @pl.when(k == 0)
    def _():
        acc_ref[...] = jnp.zeros_like(acc_ref)

    acc_ref[...] += _dot(a_ref[...], b_ref[...])

    @pl.when(k == pl.num_programs(2) - 1)
    def _():
        o_ref[...] = (acc_ref[...] * scale).astype(o_ref.dtype)


def _matmul(a, b, scale, out_dtype):
    m, kk = a.shape
    n = b.shape[1]
    bm, bn, bk = _tile(m, 1024), _tile(n, 1024), _tile(kk, 2048)
    return pl.pallas_call(
        functools.partial(_matmul_kernel, scale=scale),
        grid=(m // bm, n // bn, kk // bk),
        in_specs=[pl.BlockSpec((bm, bk), lambda i, j, k: (i, k)),
                  pl.BlockSpec((bk, bn), lambda i, j, k: (k, j))],
        out_specs=pl.BlockSpec((bm, bn), lambda i, j, k: (i, j)),
        out_shape=jax.ShapeDtypeStruct((m, n), out_dtype),
        scratch_shapes=[pltpu.VMEM((bm, bn), F32)],
        compiler_params=_cparams(("parallel", "parallel", "arbitrary")),
        name="dft_matmul",
    )(a, b)


def _even_out_kernel(h_ref, mod_ref, g_ref, gb_ref, z_ref, zp_ref, zn_ref, yf_ref, cw_ref, w_ref, o_ref):
    i = pl.program_id(1)
    x = h_ref[0]
    mod = mod_ref[0]
    g = g_ref[...]
    z = z_ref[0].astype(F32)
    tm = z.shape[0]
    halo = zp_ref.shape[1]
    prev_row = jnp.where(i > 0, zp_ref[0, halo - 1:halo, :].astype(F32), 0.0)
    next_row = jnp.where(i < pl.num_programs(1) - 1, zn_ref[0, 0:1, :].astype(F32), 0.0)
    row = lax.broadcasted_iota(jnp.int32, z.shape, 0)
    z_before = jnp.where(row == 0, prev_row, pltpu.roll(z, 1, 0))
    z_after = jnp.where(row == tm - 1, next_row, pltpu.roll(z, tm - 1, 0))
    cw = cw_ref[...]
    conv = z_before * cw[0:1] + z * cw[1:2] + z_after * cw[2:3]
    y_conv = (gb_ref[0].astype(F32) * conv).astype(BF16)
    y = _dot(y_conv, w_ref[:D_CONV, :]) + _dot(yf_ref[...], w_ref[D_CONV:, :])
    o_ref[0] = x + mod[5:6] * _rms_rows(y, g[3:4])


def _even_out(h, mod, g, gb, z, yf, conv_w, w_out):
    b, t, d = h.shape
    tm = _tile(t, 512)
    halo = 16
    nh = tm // halo
    last_halo = t // halo - 1
    return pl.pallas_call(
        _even_out_kernel,
        grid=(b, t // tm),
        in_specs=[pl.BlockSpec((1, tm, d), lambda bb, i: (bb, i, 0)),
                  _mod_spec(mod),
                  _const_spec(g.shape),
                  pl.BlockSpec((1, tm, D_CONV), lambda bb, i: (bb, i, 0)),
                  pl.BlockSpec((1, tm, D_CONV), lambda bb, i: (bb, i, 0)),
                  pl.BlockSpec((1, halo, D_CONV), lambda bb, i: (bb, jnp.maximum(i * nh - 1, 0), 0)),
                  pl.BlockSpec((1, halo, D_CONV), lambda bb, i: (bb, jnp.minimum((i + 1) * nh, last_halo), 0)),
                  pl.BlockSpec((tm, D_FOURIER), lambda bb, i: (i, bb)),
                  _const_spec(conv_w.shape),
                  _const_spec(w_out.shape)],
        out_specs=pl.BlockSpec((1, tm, d), lambda bb, i: (bb, i, 0)),
        out_shape=jax.ShapeDtypeStruct(h.shape, F32),
        compiler_params=_cparams(("parallel", "parallel")),
        name="even_out",
    )(h, mod, g, gb, z, z, z, yf, conv_w, w_out)


def _dft_tables(t):
    n = jnp.arange(t, dtype=jnp.int32)
    ang = ((n[:, None] * n[None, :]) % t).astype(F32) * (2.0 * math.pi / t)
    w_pos = jnp.concatenate([jnp.cos(ang), -jnp.sin(ang)], axis=1).astype(BF16)
    return w_pos


def _channel_dft_table():
    n = jnp.arange(D_FG, dtype=jnp.int32)
    ang = ((n[:, None] * n[None, :]) % D_FG).astype(F32) * (2.0 * math.pi / D_FG)
    return jnp.concatenate([jnp.cos(ang), jnp.sin(ang)], axis=1).astype(BF16)


def _even_mixer(h, mod, g, w_in, conv_w, w_out, dft_c, w_pos):
    b, t, _ = h.shape
    gb, z, a = _even_in(h, mod, g, w_in, dft_c)
    yf = _matmul(w_pos, a.reshape(2 * t, b * D_FOURIER), 1.0 / math.sqrt(t * D_FG), BF16)
    return _even_out(h, mod, g, gb, z, yf, conv_w, w_out)


def _odd_in_kernel(h_ref, mod_ref, g_ref, wtok_ref, wt_ref, wukp_ref, wuvt_ref, wuqt_ref, place_ref,
                   gkv_row_ref, gkv_col_ref, gq_col_ref,
                   cosk_ref, sink_ref, tkr_ref, cosq_ref, sinq_ref, cosm_ref, sinm_ref,
                   kh_ref, vtd_ref, vtm_ref, *q_refs, with_q):
    xm = _modulated(h_ref[0], mod_ref[0], g_ref[...], 1).astype(BF16)
    tm = xm.shape[0]

    ut = _dot(xm, wtok_ref[...])
    k = ut[:, :DA_QCOLS]
    ckv = ut[:, DA_QCOLS:DA_QCOLS + MLA_KV_RANK]
    kr2 = ut[:, DA_QCOLS + MLA_KV_RANK:]
    lane = lax.broadcasted_iota(jnp.int32, k.shape, 1)
    first_half = (lane % (DA_DK // 2)) < (DA_DK // 4)
    k_sw = jnp.where(first_half, pltpu.roll(k, DA_QCOLS - DA_DK // 4, 1), pltpu.roll(k, DA_DK // 4, 1))
    reps = DA_QCOLS // HEAD_PAD
    cosk = jnp.concatenate([cosk_ref[...]] * reps, axis=1)
    sink = jnp.concatenate([sink_ref[...]] * reps, axis=1)
    k_rot = k * cosk + k_sw * sink
    ckvn = _rms_rows(ckv, gkv_row_ref[...]).astype(BF16)
    k_nope = _dot(ckvn, wukp_ref[...])
    pr = kr2 * tkr_ref[...]
    pr_hi = pr.astype(BF16)
    pr_lo = (pr - pr_hi.astype(F32)).astype(BF16)
    k_mla = k_nope + _dot(pr_hi, place_ref[...]) + _dot(pr_lo, place_ref[...])
    for hh in range(DA_HEADS):
        kh_ref[0, hh] = k_rot[:, hh * HEAD_PAD:(hh + 1) * HEAD_PAD].astype(BF16)
    for hh in range(MLA_HEADS):
        kh_ref[0, DA_HEADS + hh] = k_mla[:, hh * HEAD_PAD:(hh + 1) * HEAD_PAD].astype(BF16)

    r0 = DA_VCOLS
    vckv = _dot_nt(wt_ref[:r0 + MLA_KV_RANK, :], xm)
    for hh in range(DA_HEADS):
        vtd_ref[0, hh] = vckv[hh * DA_DV:(hh + 1) * DA_DV].astype(BF16)
    ckv_t = vckv[r0:]
    ms = jnp.mean(ckv_t * ckv_t, axis=0, keepdims=True)
    ckvn_t = (ckv_t * lax.rsqrt(ms + EPS) * gkv_col_ref[...]).astype(BF16)
    vm_t = _dot(wuvt_ref[...], ckvn_t)
    for hh in range(MLA_HEADS):
        vtm_ref[0, hh] = vm_t[hh * MLA_DV:(hh + 1) * MLA_DV].astype(BF16)

    if with_q:
        qtd_ref, qtm_ref = q_refs
        r1 = r0 + MLA_KV_RANK
        q_t = _dot_nt(wt_ref[r1:, :], xm)
        cosq, sinq = cosq_ref[...], sinq_ref[...]
        qd = DA_DK // 4
        for mp in range(2 * DA_HEADS):
            q = q_t[mp * DA_DK:(mp + 1) * DA_DK]
            q_sw = jnp.concatenate([q[qd:2 * qd], q[:qd], q[3 * qd:], q[2 * qd:3 * qd]], axis=0)
            q_rot = q * cosq + q_sw * sinq
            qtd_ref[0, mp // 2, (mp % 2) * DA_DK:(mp % 2 + 1) * DA_DK, :] = q_rot.astype(BF16)
        cq_t = q_t[DA_QCOLS:]
        ms = jnp.mean(cq_t * cq_t, axis=0, keepdims=True)
        cqn_t = (cq_t * lax.rsqrt(ms + EPS) * gq_col_ref[...]).astype(BF16)
        qm_t = _dot(wuqt_ref[...], cqn_t)
        cosm, sinm = cosm_ref[...], sinm_ref[...]
        rd = MLA_ROPE // 4
        for hh in range(MLA_HEADS):
            base = hh * HEAD_PAD
            qtm_ref[0, hh, :MLA_NOPE, :] = qm_t[base:base + MLA_NOPE].astype(BF16)
            r = qm_t[base + MLA_NOPE:base + MLA_DQK]
            r_sw = jnp.concatenate([r[rd:2 * rd], r[:rd], r[3 * rd:], r[2 * rd:3 * rd]], axis=0)
            qtm_ref[0, hh, MLA_NOPE:MLA_DQK, :] = (r * cosm + r_sw * sinm).astype(BF16)
            qtm_ref[0, hh, MLA_DQK:, :] = jnp.zeros((HEAD_PAD - MLA_DQK, tm), BF16)


def _odd_in(h, mod, g, wts, tabs, kv_prev, tok_off, with_q):
    b, t, d = h.shape
    tk_total = tabs["tk_total"]
    tm = _tile(t, 256)
    assert tok_off % tm == 0
    off = tok_off // tm
    row_tab = lambda w: pl.BlockSpec((tm, w), lambda bb, i: (i, 0))
    col_tab = lambda r: pl.BlockSpec((r, tm), lambda bb, i: (0, i))
    consts = [wts["wtok"], wts["wt"], wts["wukp"], wts["wuvt"], wts["wuqt"], wts["place"],
              wts["gkv_row"], wts["gkv_col"], wts["gq_col"]]
    in_specs = ([pl.BlockSpec((1, tm, d), lambda bb, i: (bb, i, 0)), _mod_spec(mod), _const_spec(g.shape)]
                + [_const_spec(c.shape) for c in consts]
                + [row_tab(HEAD_PAD), row_tab(HEAD_PAD), row_tab(2 * MLA_ROPE),
                   col_tab(DA_DK), col_tab(DA_DK), col_tab(MLA_ROPE), col_tab(MLA_ROPE)])
    args = [h, mod, g] + consts + [tabs["cosk"], tabs["sink"], tabs["tkr"],
                                   tabs["cosq"], tabs["sinq"], tabs["cosm"], tabs["sinm"]]
    out_specs = [pl.BlockSpec((1, 2 * DA_HEADS, tm, HEAD_PAD), lambda bb, i: (bb, 0, i + off, 0)),
                 pl.BlockSpec((1, DA_HEADS, DA_DV, tm), lambda bb, i: (bb, 0, 0, i + off)),
                 pl.BlockSpec((1, MLA_HEADS, MLA_DV, tm), lambda bb, i: (bb, 0, 0, i + off))]
    out_shape = [jax.ShapeDtypeStruct((b, 2 * DA_HEADS, tk_total, HEAD_PAD), BF16),
                 jax.ShapeDtypeStruct((b, DA_HEADS, DA_DV, tk_total), BF16),
                 jax.ShapeDtypeStruct((b, MLA_HEADS, MLA_DV, tk_total), BF16)]
    if with_q:
        out_specs += [pl.BlockSpec((1, DA_HEADS, HEAD_PAD, tm), lambda bb, i: (bb, 0, 0, i)),
                      pl.BlockSpec((1, MLA_HEADS, HEAD_PAD, tm), lambda bb, i: (bb, 0, 0, i))]
        out_shape += [jax.ShapeDtypeStruct((b, DA_HEADS, HEAD_PAD, t), BF16),
                      jax.ShapeDtypeStruct((b, MLA_HEADS, HEAD_PAD, t), BF16)]
    aliases = {}
    kernel_fn = functools.partial(_odd_in_kernel, with_q=with_q)
    if kv_prev is not None:
        n_in = len(args)
        in_specs += [pl.BlockSpec(memory_space=pl.ANY)] * 3
        args += list(kv_prev)
        aliases = {n_in: 0, n_in + 1: 1, n_in + 2: 2}
        kernel_fn = functools.partial(_odd_in_alias_kernel, n_in=n_in, with_q=with_q)
    return pl.pallas_call(
        kernel_fn,
        grid=(b, t // tm),
        in_specs=in_specs,
        out_specs=out_specs,
        out_shape=out_shape,
        input_output_aliases=aliases,
        compiler_params=_cparams(("parallel", "parallel")),
        name="odd_in",
    )(*args)


def _odd_in_alias_kernel(*refs, n_in, with_q):
    _odd_in_kernel(*refs[:n_in], *refs[n_in + 3:], with_q=with_q)


def _attn_kernel(lam_ref, gsub_ref, qtd_ref, qtm_ref, kh_ref, vtd_ref, vtm_ref, otd_ref, otm_ref,
                 m_ref, l_ref, accd_ref, accm_ref, *, lam_init):
    j = pl.program_id(2)
    tq = qtd_ref.shape[-1]

    @pl.when(j == 0)
    def _():
        m_ref[...] = jnp.full_like(m_ref, NEG_BIG)
        l_ref[...] = jnp.zeros_like(l_ref)
        accd_ref[...] = jnp.zeros_like(accd_ref)
        accm_ref[...] = jnp.zeros_like(accm_ref)

    def one_map(idx, keys, q_t, v_t, acc_ref, acc_idx, c):
        s = _dot(keys, q_t)
        m_old = m_ref[idx]
        m_new = jnp.maximum(m_old, jnp.max(s, axis=0, keepdims=True))
        p = jnp.exp2((s - m_new) * c)
        alpha = jnp.exp2((m_old - m_new) * c)
        m_ref[idx] = m_new
        l_ref[idx] = alpha * l_ref[idx] + jnp.sum(p, axis=0, keepdims=True)
        acc_ref[acc_idx] = acc_ref[acc_idx] * alpha + _dot(v_t, p.astype(BF16))

    def da_head(hh, carry):
        keys = kh_ref[0, hh]
        qp = qtd_ref[0, hh]
        v_t = vtd_ref[0, hh]
        zero = jnp.zeros((DA_DK, tq), qp.dtype)
        q_first = jnp.concatenate([qp[:DA_DK], zero], axis=0)
        q_second = jnp.concatenate([zero, qp[DA_DK:]], axis=0)
        one_map(2 * hh, keys, q_first, v_t, accd_ref, 2 * hh, DA_SCALE * LOG2E)
        one_map(2 * hh + 1, keys, q_second, v_t, accd_ref, 2 * hh + 1, DA_SCALE * LOG2E)
        return carry

    def mla_head(hh, carry):
        one_map(2 * DA_HEADS + hh, kh_ref[0, DA_HEADS + hh], qtm_ref[0, hh], vtm_ref[0, hh], accm_ref, hh,
                MLA_SCALE * LOG2E)
        return carry

    lax.fori_loop(0, DA_HEADS, da_head, 0)
    lax.fori_loop(0, MLA_HEADS, mla_head, 0)

    @pl.when(j == pl.num_programs(2) - 1)
    def _():
        lp = lam_ref[...]
        lam = (jnp.exp(jnp.sum(lp[0:1] * lp[1:2], axis=1, keepdims=True))
               - jnp.exp(jnp.sum(lp[2:3] * lp[3:4], axis=1, keepdims=True)) + lam_init)
        gsub = gsub_ref[...]
        for hh in range(DA_HEADS):
            o = accd_ref[2 * hh] / l_ref[2 * hh] - lam * (accd_ref[2 * hh + 1] / l_ref[2 * hh + 1])
            ms = jnp.mean(o * o, axis=0, keepdims=True)
            otd_ref[0, hh] = (o * lax.rsqrt(ms + EPS) * gsub * (1.0 - lam_init)).astype(otd_ref.dtype)
        for hh in range(MLA_HEADS):
            otm_ref[0, hh] = (accm_ref[hh] / l_ref[2 * DA_HEADS + hh]).astype(otm_ref.dtype)


def _attention(lam_p, gsub_col, qtd, qtm, kh, vtd, vtm, lam_init, tq, tk, kv_off, n_kv):
    b, _, _, t = qtd.shape
    return pl.pallas_call(
        functools.partial(_attn_kernel, lam_init=lam_init),
        grid=(b, t // tq, n_kv),
        in_specs=[_const_spec(lam_p.shape),
                  _const_spec(gsub_col.shape),
                  pl.BlockSpec((1, DA_HEADS, HEAD_PAD, tq), lambda bb, i, j: (bb, 0, 0, i)),
                  pl.BlockSpec((1, MLA_HEADS, HEAD_PAD, tq), lambda bb, i, j: (bb, 0, 0, i)),
                  pl.BlockSpec((1, 2 * DA_HEADS, tk, HEAD_PAD), lambda bb, i, j: (bb, 0, kv_off + j, 0)),
                  pl.BlockSpec((1, DA_HEADS, DA_DV, tk), lambda bb, i, j: (bb, 0, 0, kv_off + j)),
                  pl.BlockSpec((1, MLA_HEADS, MLA_DV, tk), lambda bb, i, j: (bb, 0, 0, kv_off + j))],
        out_specs=[pl.BlockSpec((1, DA_HEADS, DA_DV, tq), lambda bb, i, j: (bb, 0, 0, i)),
                   pl.BlockSpec((1, MLA_HEADS, MLA_DV, tq), lambda bb, i, j: (bb, 0, 0, i))],
        out_shape=[jax.ShapeDtypeStruct((b, DA_HEADS, DA_DV, t), BF16),
                   jax.ShapeDtypeStruct((b, MLA_HEADS, MLA_DV, t), BF16)],
        scratch_shapes=[pltpu.VMEM((N_MAPS, 1, tq), F32),
                        pltpu.VMEM((N_MAPS, 1, tq), F32),
                        pltpu.VMEM((2 * DA_HEADS, DA_DV, tq), F32),
                        pltpu.VMEM((MLA_HEADS, MLA_DV, tq), F32)],
        compiler_params=_cparams(("parallel", "parallel", "arbitrary")),
        name="attention",
    )(lam_p, gsub_col, qtd, qtm, kh, vtd, vtm)


def _odd_out_kernel(h_ref, mod_ref, g_ref, otd_ref, otm_ref, wt_ref, o_ref):
    y_t = _dot(wt_ref[:, :DA_VCOLS], otd_ref[0]) + _dot(wt_ref[:, DA_VCOLS:], otm_ref[0])
    y = y_t.T
    o_ref[0] = h_ref[0] + mod_ref[0][5:6] * _rms_rows(y, g_ref[3:4])


def _odd_out(h, mod, g, otd, otm, w_out_t):
    b, t, d = h.shape
    tm = _tile(t, 512)
    otd = otd.reshape(b, DA_VCOLS, t)
    otm = otm.reshape(b, MLA_HEADS * MLA_DV, t)
    return pl.pallas_call(
        _odd_out_kernel,
        grid=(b, t // tm),
        in_specs=[pl.BlockSpec((1, tm, d), lambda bb, i: (bb, i, 0)),
                  _mod_spec(mod),
                  _const_spec(g.shape),
                  pl.BlockSpec((1, DA_VCOLS, tm), lambda bb, i: (bb, 0, i)),
                  pl.BlockSpec((1, MLA_HEADS * MLA_DV, tm), lambda bb, i: (bb, 0, i)),
                  _const_spec(w_out_t.shape)],
        out_specs=pl.BlockSpec((1, tm, d), lambda bb, i: (bb, i, 0)),
        out_shape=jax.ShapeDtypeStruct(h.shape, F32),
        compiler_params=_cparams(("parallel", "parallel")),
        name="odd_out",
    )(h, mod, g, otd, otm, w_out_t)


def _rope_angles(n_tokens, rot_dim):
    rows = n_tokens // GRID_W
    row = jnp.broadcast_to(jnp.arange(rows)[:, None], (rows, GRID_W)).reshape(-1).astype(F32)
    col = jnp.broadcast_to(jnp.arange(GRID_W)[None, :], (rows, GRID_W)).reshape(-1).astype(F32)
    n_freq = rot_dim // 4
    freqs = ROPE_THETA ** (-jnp.arange(n_freq, dtype=F32) / n_freq)
    return row[:, None] * freqs, col[:, None] * freqs


def _rope_cos_sin(n_tokens, rot_dim, identity):
    if identity:
        return jnp.ones((n_tokens, rot_dim), F32), jnp.zeros((n_tokens, rot_dim), F32)
    ar, ac = _rope_angles(n_tokens, rot_dim)
    cos = jnp.concatenate([jnp.cos(ar), jnp.cos(ar), jnp.cos(ac), jnp.cos(ac)], axis=1)
    sin = jnp.concatenate([-jnp.sin(ar), jnp.sin(ar), -jnp.sin(ac), jnp.sin(ac)], axis=1)
    return cos, sin


def _rope_tables(n_tokens, identity, tk_total):
    cd, sd = _rope_cos_sin(n_tokens, DA_DK, identity)
    cm, sm = _rope_cos_sin(n_tokens, MLA_ROPE, identity)
    return {
        "cosk": jnp.concatenate([cd, cd], axis=1), "sink": jnp.concatenate([sd, sd], axis=1),
        "tkr": jnp.concatenate([cm, sm], axis=1),
        "cosq": cd.T, "sinq": sd.T, "cosm": cm.T, "sinm": sm.T,
        "tk_total": tk_total,
    }


def _swap_perm(rot_dim):
    q = rot_dim // 4
    return jnp.concatenate([jnp.arange(q, 2 * q), jnp.arange(0, q), jnp.arange(3 * q, 4 * q), jnp.arange(2 * q, 3 * q)])


def _odd_weights(w_in, g_q, w_uq, g_kv, w_uk, w_uv):
    w_q = w_in[:, :DA_QCOLS]
    w_cq = w_in[:, DA_QCOLS:Q_COLS]
    w_k = w_in[:, Q_COLS:Q_COLS + DA_QCOLS]
    w_v = w_in[:, Q_COLS + DA_QCOLS:Q_COLS + DA_QCOLS + DA_VCOLS]
    w_ckv = w_in[:, Q_COLS + DA_QCOLS + DA_VCOLS:Q_COLS + DA_QCOLS + DA_VCOLS + MLA_KV_RANK]
    w_kr = w_in[:, Q_COLS + DA_QCOLS + DA_VCOLS + MLA_KV_RANK:]
    wtok = jnp.concatenate([w_k, w_ckv, w_kr, w_kr[:, _swap_perm(MLA_ROPE)]], axis=1).astype(BF16)
    wt = jnp.concatenate([w_v, w_ckv, w_q, w_cq], axis=1).T.astype(BF16)
    pad_k = jnp.zeros((MLA_KV_RANK, MLA_HEADS, HEAD_PAD), F32)
    wukp = pad_k.at[:, :, :MLA_NOPE].set(w_uk.reshape(MLA_KV_RANK, MLA_HEADS, MLA_NOPE))
    wukp = wukp.reshape(MLA_KV_RANK, MLA_HEADS * HEAD_PAD).astype(BF16)
    pad_q = jnp.zeros((MLA_Q_RANK, MLA_HEADS, HEAD_PAD), F32)
    wuqp = pad_q.at[:, :, :MLA_DQK].set(w_uq.reshape(MLA_Q_RANK, MLA_HEADS, MLA_DQK))
    wuqt = wuqp.reshape(MLA_Q_RANK, MLA_HEADS * HEAD_PAD).T.astype(BF16)
    eye = jnp.eye(MLA_ROPE, dtype=F32)
    place = jnp.zeros((2, MLA_ROPE, MLA_HEADS, HEAD_PAD), F32)
    place = place.at[:, :, :, MLA_NOPE:MLA_DQK].set(jnp.broadcast_to(eye[None, :, None, :], (2, MLA_ROPE, MLA_HEADS, MLA_ROPE)))
    place = place.reshape(2 * MLA_ROPE, MLA_HEADS * HEAD_PAD).astype(BF16)
    return {
        "wtok": wtok, "wt": wt, "wukp": wukp, "wuvt": w_uv.T.astype(BF16), "wuqt": wuqt, "place": place,
        "gkv_row": g_kv.reshape(1, -1).astype(F32), "gkv_col": g_kv.reshape(-1, 1).astype(F32),
        "gq_col": g_q.reshape(-1, 1).astype(F32),
    }


def kernel(x, c, ctx, c_ctx, w_mod, b_mod, norm_g, w_ffn_in, w_ffn_out, w_in_even, conv_w, w_out_even,
           w_in_odd, g_q_mla, w_uq, g_kv_mla, w_uk, w_uv, lam_q1, lam_k1, lam_q2, lam_k2, g_subln, w_out_odd):
    b, t, d = x.shape
    tc = ctx.shape[1]
    depth = w_mod.shape[0]
    tk_total = t + tc

    rows = -(-(b + 1) // 8) * 8
    cond = jnp.zeros((rows, d), F32).at[:b].set(c).at[b].set(c_ctx)
    mod = _adaln(cond, w_mod, b_mod).reshape(depth, rows, N_MOD, d)

    dft_c = _channel_dft_table()
    w_pos_x, w_pos_c = _dft_tables(t), _dft_tables(tc)
    tabs_x = _rope_tables(t, False, tk_total)
    tabs_c = _rope_tables(tc, True, tk_total)
    tq = _tile(t, 512)
    tk = 768 if tk_total % 768 == 0 else _tile(tk_total, 512)
    assert t % tc == 0 and tk_total % tk == 0

    h, hc = x, ctx
    for l in range(depth):
        last = l == depth - 1
        odd = l % 2 == 1
        ctx_live = (not last) or odd
        g = norm_g[l]
        m_x, m_c = mod[l, :b], mod[l, b:b + 1]
        wi0, wo0 = w_ffn_in[l, 0].astype(BF16), w_ffn_out[l, 0].astype(BF16)
        h = _ffn(h, m_x, g, wi0, wo0, 0)
        if ctx_live:
            hc = _ffn(hc, m_c, g, wi0, wo0, 0)

        if not odd:
            e = l // 2
            w_in, w_out = w_in_even[e].astype(BF16), w_out_even[e].astype(BF16)
            h = _even_mixer(h, m_x, g, w_in, conv_w[e], w_out, dft_c, w_pos_x)
            if ctx_live:
                hc = _even_mixer(hc, m_c, g, w_in, conv_w[e], w_out, dft_c, w_pos_c)
        else:
            o = l // 2
            lam_init = 0.8 - 0.6 * math.exp(-0.3 * l)
            wts = _odd_weights(w_in_odd[o], g_q_mla[o], w_uq[o], g_kv_mla[o], w_uk[o], w_uv[o])
            lam_p = jnp.stack([lam_q1[o], lam_k1[o], lam_q2[o], lam_k2[o]]).astype(F32)
            gsub_col = g_subln[o].reshape(-1, 1).astype(F32)
            w_out_t = w_out_odd[o].T.astype(BF16)
            kh, vtd, vtm, qtd, qtm = _odd_in(h, m_x, g, wts, tabs_x, None, 0, True)
            outs_c = _odd_in(hc, m_c, g, wts, tabs_c, (kh, vtd, vtm), t, not last)
            kh, vtd, vtm = outs_c[:3]
            otd, otm = _attention(lam_p, gsub_col, qtd, qtm, kh, vtd, vtm, lam_init, tq, tk, 0, tk_total // tk)
            if not last:
                ocd, ocm = _attention(lam_p, gsub_col, outs_c[3], outs_c[4], kh, vtd, vtm, lam_init,
                                      tc, tc, t // tc, 1)
                hc = _odd_out(hc, m_c, g, ocd, ocm, w_out_t)
            h = _odd_out(h, m_x, g, otd, otm, w_out_t)

        wi1, wo1 = w_ffn_in[l, 1].astype(BF16), w_ffn_out[l, 1].astype(BF16)
        h = _ffn(h, m_x, g, wi1, wo1, 2)
        if not last:
            hc = _ffn(hc, m_c, g, wi1, wo1, 2)
    return h
```

```python
import functools
import math

import jax
import jax.numpy as jnp
from jax import lax
from jax.experimental import pallas as pl
from jax.experimental.pallas import tpu as pltpu

F32 = jnp.float32
BF16 = jnp.bfloat16

D_MODEL = 1024
GRID_W = 64
N_MOD = 9
FFN_RES = 0.5
EPS = 1e-6
ROPE_THETA = 10000.0

D_CONV = 512
D_FOURIER = 512
FOURIER_GROUPS = 4
D_FG = D_FOURIER // FOURIER_GROUPS

DA_HEADS = 8
DA_DK = 64
DA_DV = 128
DA_SCALE = DA_DK ** -0.5
MLA_HEADS = 8
MLA_NOPE = 64
MLA_ROPE = 32
MLA_DQK = MLA_NOPE + MLA_ROPE
MLA_DV = 64
MLA_Q_RANK = 384
MLA_KV_RANK = 256
MLA_SCALE = MLA_DQK ** -0.5
DA_QCOLS = DA_HEADS * 2 * DA_DK
DA_VCOLS = DA_HEADS * DA_DV
Q_COLS = DA_QCOLS + MLA_Q_RANK
HEAD_PAD = 128
N_MAPS = 2 * DA_HEADS + MLA_HEADS
LOG2E = 1.4426950408889634
NEG_BIG = -1e30
N_KV_ARRAYS = 4
BOUND_SLACK = 1.02
MIN_DENOMINATOR = 2.0 ** -40

VMEM_LIMIT_V7X = 56 * 1024 * 1024


def _cparams(sem):
    return pltpu.CompilerParams(dimension_semantics=sem, vmem_limit_bytes=VMEM_LIMIT_V7X)


def _tile(n, pref):
    if n <= pref:
        return n
    t = pref - pref % 128
    while t >= 128:
        if n % t == 0:
            return t
        t -= 128
    return n


def _const_spec(shape):
    nd = len(shape)
    return pl.BlockSpec(shape, lambda *_: (0,) * nd, pipeline_mode=pl.Buffered(1))


def _mod_spec(mod):
    if mod.shape[0] == 1:
        return pl.BlockSpec((1, N_MOD, D_MODEL), lambda b, *_: (0, 0, 0))
    return pl.BlockSpec((1, N_MOD, D_MODEL), lambda b, *_: (b, 0, 0))


def _rms_rows(x, g):
    ms = jnp.mean(x * x, axis=-1, keepdims=True)
    return x * lax.rsqrt(ms + EPS) * g


def _modulated(x, mod, g, slot):
    shift = mod[3 * slot:3 * slot + 1]
    scale = mod[3 * slot + 1:3 * slot + 2]
    return _rms_rows(x, g[2 * slot:2 * slot + 1]) * (1.0 + scale) + shift


def _dot(a, b):
    return jnp.dot(a, b, preferred_element_type=F32)


def _dot_nt(a, b):
    return lax.dot_general(a, b, (((1,), (1,)), ((), ())), preferred_element_type=F32)


def _adaln_kernel(c_ref, w_ref, b_ref, o_ref):
    c = c_ref[...]
    a = c * jax.nn.sigmoid(c)
    o_ref[0] = jnp.dot(a, w_ref[0], preferred_element_type=F32,
                       precision=lax.Precision.HIGHEST) + b_ref[0]


def _adaln(cond, w_mod, b_mod):
    depth, d, n = w_mod.shape
    rows = cond.shape[0]
    tn = _tile(n, 1152)
    return pl.pallas_call(
        _adaln_kernel,
        grid=(depth, n // tn),
        in_specs=[pl.BlockSpec((rows, d), lambda l, j: (0, 0)),
                  pl.BlockSpec((1, d, tn), lambda l, j: (l, 0, j)),
                  pl.BlockSpec((1, 1, tn), lambda l, j: (l, 0, j))],
        out_specs=pl.BlockSpec((1, rows, tn), lambda l, j: (l, 0, j)),
        out_shape=jax.ShapeDtypeStruct((depth, rows, n), F32),
        compiler_params=_cparams(("parallel", "parallel")),
        name="adaln",
    )(cond, w_mod, b_mod.reshape(depth, 1, n))


def _ffn_kernel(h_ref, mod_ref, g_ref, win_ref, wout_ref, o_ref, *, slot, d_ff):
    x = h_ref[0]
    mod = mod_ref[0]
    g = g_ref[...]
    xm = _modulated(x, mod, g, slot).astype(BF16)
    gate = _dot(xm, win_ref[:, :d_ff])
    up = _dot(xm, win_ref[:, d_ff:])
    act = (gate * jax.nn.sigmoid(gate) * up).astype(BF16)
    y = _dot(act, wout_ref[...])
    res_gate = mod[3 * slot + 2:3 * slot + 3]
    o_ref[0] = x + FFN_RES * res_gate * _rms_rows(y, g[2 * slot + 1:2 * slot + 2])


def _ffn(h, mod, g, w_in, w_out, slot):
    b, t, d = h.shape
    d_ff = w_out.shape[0]
    tm = _tile(t, 512)
    return pl.pallas_call(
        functools.partial(_ffn_kernel, slot=slot, d_ff=d_ff),
        grid=(b, t // tm),
        in_specs=[pl.BlockSpec((1, tm, d), lambda bb, i: (bb, i, 0)),
                  _mod_spec(mod),
                  _const_spec(g.shape),
                  _const_spec(w_in.shape),
                  _const_spec(w_out.shape)],
        out_specs=pl.BlockSpec((1, tm, d), lambda bb, i: (bb, i, 0)),
        out_shape=jax.ShapeDtypeStruct(h.shape, F32),
        compiler_params=_cparams(("parallel", "parallel")),
        name="ffn",
    )(h, mod, g, w_in, w_out)


def _even_in_kernel(h_ref, mod_ref, g_ref, w_ref, dft_ref, gb_ref, z_ref, a_ref):
    xm = _modulated(h_ref[0], mod_ref[0], g_ref[...], 1).astype(BF16)
    u = _dot(xm, w_ref[...])
    gb_ref[0] = u[:, :D_CONV].astype(BF16)
    z_ref[0] = (u[:, D_CONV:2 * D_CONV] * u[:, 2 * D_CONV:3 * D_CONV]).astype(BF16)
    xf = u[:, 3 * D_CONV:].astype(BF16)
    for gi in range(FOURIER_GROUPS):
        pq = _dot(xf[:, gi * D_FG:(gi + 1) * D_FG], dft_ref[...])
        a_ref[0, :, gi * D_FG:(gi + 1) * D_FG] = pq[:, :D_FG].astype(BF16)
        a_ref[1, :, gi * D_FG:(gi + 1) * D_FG] = pq[:, D_FG:].astype(BF16)


def _even_in(h, mod, g, w_in, dft_c):
    b, t, d = h.shape
    tm = _tile(t, 512)
    return pl.pallas_call(
        _even_in_kernel,
        grid=(b, t // tm),
        in_specs=[pl.BlockSpec((1, tm, d), lambda bb, i: (bb, i, 0)),
                  _mod_spec(mod),
                  _const_spec(g.shape),
                  _const_spec(w_in.shape),
                  _const_spec(dft_c.shape)],
        out_specs=[pl.BlockSpec((1, tm, D_CONV), lambda bb, i: (bb, i, 0)),
                   pl.BlockSpec((1, tm, D_CONV), lambda bb, i: (bb, i, 0)),
                   pl.BlockSpec((2, tm, D_FOURIER), lambda bb, i: (0, i, bb))],
        out_shape=[jax.ShapeDtypeStruct((b, t, D_CONV), BF16),
                   jax.ShapeDtypeStruct((b, t, D_CONV), BF16),
                   jax.ShapeDtypeStruct((2, t, b * D_FOURIER), BF16)],
        compiler_params=_cparams(("parallel", "parallel")),
        name="even_in",
    )(h, mod, g, w_in, dft_c)


def _matmul_kernel(a_ref, b_ref, o_ref, acc_ref, *, scale):
    k = pl.program_id(2)

    @pl.when(k == 0)
    def _():
        acc_ref[...] = jnp.zeros_like(acc_ref)

    acc_ref[...] += _dot(a_ref[...], b_ref[...])

    @pl.when(k == pl.num_programs(2) - 1)
    def _():
        o_ref[...] = (acc_ref[...] * scale).astype(o_ref.dtype)


def _matmul(a, b, scale, out_dtype):
    m, kk = a.shape
    n = b.shape[1]
    bm, bn, bk = _tile(m, 1024), _tile(n, 1024), _tile(kk, 2048)
    return pl.pallas_call(
        functools.partial(_matmul_kernel, scale=scale),
        grid=(m // bm, n // bn, kk // bk),
        in_specs=[pl.BlockSpec((bm, bk), lambda i, j, k: (i, k)),
                  pl.BlockSpec((bk, bn), lambda i, j, k: (k, j))],
        out_specs=pl.BlockSpec((bm, bn), lambda i, j, k: (i, j)),
        out_shape=jax.ShapeDtypeStruct((m, n), out_dtype),
        scratch_shapes=[pltpu.VMEM((bm, bn), F32)],
        compiler_params=_cparams(("parallel", "parallel", "arbitrary")),
        name="dft_matmul",
    )(a, b)


def _even_out_kernel(h_ref, mod_ref, g_ref, gb_ref, z_ref, zp_ref, zn_ref, yf_ref, cw_ref, w_ref, o_ref):
    i = pl.program_id(1)
    x = h_ref[0]
    mod = mod_ref[0]
    g = g_ref[...]
    z = z_ref[0].astype(F32)
    tm = z.shape[0]
    halo = zp_ref.shape[1]
    prev_row = jnp.where(i > 0, zp_ref[0, halo - 1:halo, :].astype(F32), 0.0)
    next_row = jnp.where(i < pl.num_programs(1) - 1, zn_ref[0, 0:1, :].astype(F32), 0.0)
    row = lax.broadcasted_iota(jnp.int32, z.shape, 0)
    z_before = jnp.where(row == 0, prev_row, pltpu.roll(z, 1, 0))
    z_after = jnp.where(row == tm - 1, next_row, pltpu.roll(z, tm - 1, 0))
    cw = cw_ref[...]
    conv = z_before * cw[0:1] + z * cw[1:2] + z_after * cw[2:3]
    y_conv = (gb_ref[0].astype(F32) * conv).astype(BF16)
    y = _dot(y_conv, w_ref[:D_CONV, :]) + _dot(yf_ref[...], w_ref[D_CONV:, :])
    o_ref[0] = x + mod[5:6] * _rms_rows(y, g[3:4])


def _even_out(h, mod, g, gb, z, yf, conv_w, w_out):
    b, t, d = h.shape
    tm = _tile(t, 512)
    halo = 16
    nh = tm // halo
    last_halo = t // halo - 1
    return pl.pallas_call(
        _even_out_kernel,
        grid=(b, t // tm),
        in_specs=[pl.BlockSpec((1, tm, d), lambda bb, i: (bb, i, 0)),
                  _mod_spec(mod),
                  _const_spec(g.shape),
                  pl.BlockSpec((1, tm, D_CONV), lambda bb, i: (bb, i, 0)),
                  pl.BlockSpec((1, tm, D_CONV), lambda bb, i: (bb, i, 0)),
                  pl.BlockSpec((1, halo, D_CONV), lambda bb, i: (bb, jnp.maximum(i * nh - 1, 0), 0)),
                  pl.BlockSpec((1, halo, D_CONV), lambda bb, i: (bb, jnp.minimum((i + 1) * nh, last_halo), 0)),
                  pl.BlockSpec((tm, D_FOURIER), lambda bb, i: (i, bb)),
                  _const_spec(conv_w.shape),
                  _const_spec(w_out.shape)],
        out_specs=pl.BlockSpec((1, tm, d), lambda bb, i: (bb, i, 0)),
        out_shape=jax.ShapeDtypeStruct(h.shape, F32),
        compiler_params=_cparams(("parallel", "parallel")),
        name="even_out",
    )(h, mod, g, gb, z, z, z, yf, conv_w, w_out)


def _dft_tables(t):
    n = jnp.arange(t, dtype=jnp.int32)
    ang = ((n[:, None] * n[None, :]) % t).astype(F32) * (2.0 * math.pi / t)
    w_pos = jnp.concatenate([jnp.cos(ang), -jnp.sin(ang)], axis=1).astype(BF16)
    return w_pos


def _channel_dft_table():
    n = jnp.arange(D_FG, dtype=jnp.int32)
    ang = ((n[:, None] * n[None, :]) % D_FG).astype(F32) * (2.0 * math.pi / D_FG)
    return jnp.concatenate([jnp.cos(ang), jnp.sin(ang)], axis=1).astype(BF16)


def _even_mixer(h, mod, g, w_in, conv_w, w_out, dft_c, w_pos):
    b, t, _ = h.shape
    gb, z, a = _even_in(h, mod, g, w_in, dft_c)
    yf = _matmul(w_pos, a.reshape(2 * t, b * D_FOURIER), 1.0 / math.sqrt(t * D_FG), BF16)
    return _even_out(h, mod, g, gb, z, yf, conv_w, w_out)


def _odd_in_kernel(h_ref, mod_ref, g_ref, wtok_ref, wt_ref, wukp_ref, wuvt_ref, wuqt_ref, place_ref, gsel_ref,
                   gkv_row_ref, gkv_col_ref, gq_col_ref,
                   cosk_ref, sink_ref, tkr_ref, cosq_ref, sinq_ref, cosm_ref, sinm_ref,
                   kh_ref, vtd_ref, vtm_ref, kn_ref, *q_refs, with_q):
    xm = _modulated(h_ref[0], mod_ref[0], g_ref[...], 1).astype(BF16)
    tm = xm.shape[0]

    ut = _dot(xm, wtok_ref[...])
    k = ut[:, :DA_QCOLS]
    ckv = ut[:, DA_QCOLS:DA_QCOLS + MLA_KV_RANK]
    kr2 = ut[:, DA_QCOLS + MLA_KV_RANK:]
    lane = lax.broadcasted_iota(jnp.int32, k.shape, 1)
    first_half = (lane % (DA_DK // 2)) < (DA_DK // 4)
    k_sw = jnp.where(first_half, pltpu.roll(k, DA_QCOLS - DA_DK // 4, 1), pltpu.roll(k, DA_DK // 4, 1))
    reps = DA_QCOLS // HEAD_PAD
    cosk = jnp.concatenate([cosk_ref[...]] * reps, axis=1)
    sink = jnp.concatenate([sink_ref[...]] * reps, axis=1)
    k_rot = k * cosk + k_sw * sink
    ckvn = _rms_rows(ckv, gkv_row_ref[...]).astype(BF16)
    k_nope = _dot(ckvn, wukp_ref[...])
    pr = kr2 * tkr_ref[...]
    pr_hi = pr.astype(BF16)
    pr_lo = (pr - pr_hi.astype(F32)).astype(BF16)
    k_mla = k_nope + _dot(pr_hi, place_ref[...]) + _dot(pr_lo, place_ref[...])
    k_odd = pltpu.roll(k_rot, DA_QCOLS - DA_DK, 1)
    lane_h = lax.broadcasted_iota(jnp.int32, (tm, HEAD_PAD), 1)
    sq = []
    for mp in range(N_MAPS):
        if mp < 2 * DA_HEADS:
            src = (k_rot if mp % 2 == 0 else k_odd)[:, (mp // 2) * HEAD_PAD:(mp // 2 + 1) * HEAD_PAD]
            kd = jnp.where(lane_h < DA_DK, src, 0.0)
            one_lane = DA_DK
        else:
            hh = mp - 2 * DA_HEADS
            kd = k_mla[:, hh * HEAD_PAD:(hh + 1) * HEAD_PAD]
            one_lane = MLA_DQK
        kh_ref[0, mp] = jnp.where(lane_h == one_lane, 1.0, kd).astype(BF16)
        kf = kd.astype(BF16).astype(F32)
        sq.append((kf * kf).astype(BF16))
    kn_ref[0] = _dot_nt(gsel_ref[...], jnp.concatenate(sq, axis=1))

    r0 = DA_VCOLS
    vckv = _dot_nt(wt_ref[:r0 + MLA_KV_RANK, :], xm)
    for hh in range(DA_HEADS):
        vtd_ref[0, hh] = vckv[hh * DA_DV:(hh + 1) * DA_DV].astype(BF16)
    ckv_t = vckv[r0:]
    ms = jnp.mean(ckv_t * ckv_t, axis=0, keepdims=True)
    ckvn_t = (ckv_t * lax.rsqrt(ms + EPS) * gkv_col_ref[...]).astype(BF16)
    vm_t = _dot(wuvt_ref[...], ckvn_t)
    for hh in range(MLA_HEADS):
        vtm_ref[0, hh] = vm_t[hh * MLA_DV:(hh + 1) * MLA_DV].astype(BF16)

    if with_q:
        qtd_ref, qtm_ref = q_refs
        r1 = r0 + MLA_KV_RANK
        q_t = _dot_nt(wt_ref[r1:, :], xm)
        cosq, sinq = cosq_ref[...], sinq_ref[...]
        qd = DA_DK // 4
        for mp in range(2 * DA_HEADS):
            q = q_t[mp * DA_DK:(mp + 1) * DA_DK]
            q_sw = jnp.concatenate([q[qd:2 * qd], q[:qd], q[3 * qd:], q[2 * qd:3 * qd]], axis=0)
            q_rot = (q * cosq + q_sw * sinq) * (DA_SCALE * LOG2E)
            qtd_ref[0, mp // 2, (mp % 2) * DA_DK:(mp % 2 + 1) * DA_DK, :] = q_rot.astype(BF16)
        cq_t = q_t[DA_QCOLS:]
        ms = jnp.mean(cq_t * cq_t, axis=0, keepdims=True)
        cqn_t = (cq_t * lax.rsqrt(ms + EPS) * gq_col_ref[...]).astype(BF16)
        qm_t = _dot(wuqt_ref[...], cqn_t) * (MLA_SCALE * LOG2E)
        cosm, sinm = cosm_ref[...], sinm_ref[...]
        rd = MLA_ROPE // 4
        for hh in range(MLA_HEADS):
            base = hh * HEAD_PAD
            qtm_ref[0, hh, :MLA_NOPE, :] = qm_t[base:base + MLA_NOPE].astype(BF16)
            r = qm_t[base + MLA_NOPE:base + MLA_DQK]
            r_sw = jnp.concatenate([r[rd:2 * rd], r[:rd], r[3 * rd:], r[2 * rd:3 * rd]], axis=0)
            qtm_ref[0, hh, MLA_NOPE:MLA_DQK, :] = (r * cosm + r_sw * sinm).astype(BF16)
            qtm_ref[0, hh, MLA_DQK:, :] = jnp.zeros((HEAD_PAD - MLA_DQK, tm), BF16)


def _odd_in(h, mod, g, wts, tabs, kv_prev, tok_off, with_q):
    b, t, d = h.shape
    tk_total = tabs["tk_total"]
    tm = _tile(t, 256)
    assert tok_off % tm == 0
    off = tok_off // tm
    row_tab = lambda w: pl.BlockSpec((tm, w), lambda bb, i: (i, 0))
    col_tab = lambda r: pl.BlockSpec((r, tm), lambda bb, i: (0, i))
    consts = [wts["wtok"], wts["wt"], wts["wukp"], wts["wuvt"], wts["wuqt"], wts["place"], wts["gsel"],
              wts["gkv_row"], wts["gkv_col"], wts["gq_col"]]
    in_specs = ([pl.BlockSpec((1, tm, d), lambda bb, i: (bb, i, 0)), _mod_spec(mod), _const_spec(g.shape)]
                + [_const_spec(c.shape) for c in consts]
                + [row_tab(HEAD_PAD), row_tab(HEAD_PAD), row_tab(2 * MLA_ROPE),
                   col_tab(DA_DK), col_tab(DA_DK), col_tab(MLA_ROPE), col_tab(MLA_ROPE)])
    args = [h, mod, g] + consts + [tabs["cosk"], tabs["sink"], tabs["tkr"],
                                   tabs["cosq"], tabs["sinq"], tabs["cosm"], tabs["sinm"]]
    out_specs = [pl.BlockSpec((1, N_MAPS, tm, HEAD_PAD), lambda bb, i: (bb, 0, i + off, 0)),
                 pl.BlockSpec((1, DA_HEADS, DA_DV, tm), lambda bb, i: (bb, 0, 0, i + off)),
                 pl.BlockSpec((1, MLA_HEADS, MLA_DV, tm), lambda bb, i: (bb, 0, 0, i + off)),
                 pl.BlockSpec((1, N_MAPS, tm), lambda bb, i: (bb, 0, i + off))]
    out_shape = [jax.ShapeDtypeStruct((b, N_MAPS, tk_total, HEAD_PAD), BF16),
                 jax.ShapeDtypeStruct((b, DA_HEADS, DA_DV, tk_total), BF16),
                 jax.ShapeDtypeStruct((b, MLA_HEADS, MLA_DV, tk_total), BF16),
                 jax.ShapeDtypeStruct((b, N_MAPS, tk_total), F32)]
    if with_q:
        out_specs += [pl.BlockSpec((1, DA_HEADS, HEAD_PAD, tm), lambda bb, i: (bb, 0, 0, i)),
                      pl.BlockSpec((1, MLA_HEADS, HEAD_PAD, tm), lambda bb, i: (bb, 0, 0, i))]
        out_shape += [jax.ShapeDtypeStruct((b, DA_HEADS, HEAD_PAD, t), BF16),
                      jax.ShapeDtypeStruct((b, MLA_HEADS, HEAD_PAD, t), BF16)]
    aliases = {}
    kernel_fn = functools.partial(_odd_in_kernel, with_q=with_q)
    if kv_prev is not None:
        n_in = len(args)
        in_specs += [pl.BlockSpec(memory_space=pl.ANY)] * N_KV_ARRAYS
        args += list(kv_prev)
        aliases = {n_in + a: a for a in range(N_KV_ARRAYS)}
        kernel_fn = functools.partial(_odd_in_alias_kernel, n_in=n_in, with_q=with_q)
    return pl.pallas_call(
        kernel_fn,
        grid=(b, t // tm),
        in_specs=in_specs,
        out_specs=out_specs,
        out_shape=out_shape,
        input_output_aliases=aliases,
        compiler_params=_cparams(("parallel", "parallel")),
        name="odd_in",
    )(*args)


def _odd_in_alias_kernel(*refs, n_in, with_q):
    _odd_in_kernel(*refs[:n_in], *refs[n_in + N_KV_ARRAYS:], with_q=with_q)


def _attn_kernel(lam_ref, gsub_ref, kn_ref, qtd_ref, qtm_ref, kh_ref, vtd_ref, vtm_ref,
                 otd_ref, otm_ref, lmin_ref, qa_ref, m_ref, l_ref, accd_ref, accm_ref, *, lam_init, safe):
    j = pl.program_id(2)
    tq = qtd_ref.shape[-1]

    @pl.when(j == 0)
    def _():
        m_ref[...] = jnp.full_like(m_ref, NEG_BIG)
        l_ref[...] = jnp.zeros_like(l_ref)
        accd_ref[...] = jnp.zeros_like(accd_ref)
        accm_ref[...] = jnp.zeros_like(accm_ref)
        kmax = jnp.sqrt(jnp.max(kn_ref[0], axis=1, keepdims=True))
        shift_row = lax.broadcasted_iota(jnp.int32, (16, tq), 0) == 0
        for idx in range(N_MAPS):
            if idx < 2 * DA_HEADS:
                q = qtd_ref[0, idx // 2, (idx % 2) * DA_DK:(idx % 2 + 1) * DA_DK, :]
            else:
                q = qtm_ref[0, idx - 2 * DA_HEADS, :MLA_DQK, :]
            if safe:
                shift_blk = jnp.zeros((16, tq), BF16)
            else:
                qf = q.astype(F32)
                qn = jnp.sqrt(jnp.sum(qf * qf, axis=0, keepdims=True))
                shift = -(BOUND_SLACK * kmax[idx:idx + 1]) * qn
                shift_blk = jnp.where(shift_row, shift, 0.0).astype(BF16)
            pad = jnp.zeros((HEAD_PAD - q.shape[0] - 16, tq), BF16)
            qa_ref[idx] = jnp.concatenate([q, shift_blk, pad], axis=0)

    def one_map(idx, v_t, acc_ref, acc_idx):
        s = _dot(kh_ref[0, idx], qa_ref[idx])
        if safe:
            m_old = m_ref[idx]
            m_new = jnp.maximum(m_old, jnp.max(s, axis=0, keepdims=True))
            alpha = jnp.exp2(m_old - m_new)
            m_ref[idx] = m_new
            p = jnp.exp2(s - m_new)
            l_ref[idx] = alpha * l_ref[idx] + jnp.sum(p.reshape(-1, 8, tq), axis=0)
            acc_ref[acc_idx] = acc_ref[acc_idx] * alpha + _dot(v_t, p.astype(BF16))
        else:
            p = jnp.exp2(s)
            l_ref[idx] += jnp.sum(p.reshape(-1, 8, tq), axis=0)
            acc_ref[acc_idx] += _dot(v_t, p.astype(BF16))

    def da_head(hh, carry):
        v_t = vtd_ref[0, hh]
        one_map(2 * hh, v_t, accd_ref, 2 * hh)
        one_map(2 * hh + 1, v_t, accd_ref, 2 * hh + 1)
        return carry

    def mla_head(hh, carry):
        one_map(2 * DA_HEADS + hh, vtm_ref[0, hh], accm_ref, hh)
        return carry

    lax.fori_loop(0, DA_HEADS, da_head, 0)
    lax.fori_loop(0, MLA_HEADS, mla_head, 0)

    @pl.when(j == pl.num_programs(2) - 1)
    def _():
        lp = lam_ref[...]
        lam = (jnp.exp(jnp.sum(lp[0:1] * lp[1:2], axis=1, keepdims=True))
               - jnp.exp(jnp.sum(lp[2:3] * lp[3:4], axis=1, keepdims=True)) + lam_init)
        gsub = gsub_ref[...]
        lsum = [jnp.sum(l_ref[idx], axis=0, keepdims=True) for idx in range(N_MAPS)]
        for hh in range(DA_HEADS):
            o = accd_ref[2 * hh] / lsum[2 * hh] - lam * (accd_ref[2 * hh + 1] / lsum[2 * hh + 1])
            ms = jnp.mean(o * o, axis=0, keepdims=True)
            otd_ref[0, hh] = (o * lax.rsqrt(ms + EPS) * gsub * (1.0 - lam_init)).astype(otd_ref.dtype)
        for hh in range(MLA_HEADS):
            otm_ref[0, hh] = (accm_ref[hh] / lsum[2 * DA_HEADS + hh]).astype(otm_ref.dtype)
        lmin = lsum[0]
        for idx in range(1, N_MAPS):
            lmin = jnp.minimum(lmin, lsum[idx])
        lmin_ref[0, 0] = jnp.broadcast_to(jnp.min(lmin, axis=1, keepdims=True), lmin_ref.shape[2:])


def _attention_call(lam_p, gsub_col, qtd, qtm, kv, lam_init, tq, tk, kv_off, n_kv, safe):
    kh, vtd, vtm, kn = kv
    b, _, _, t = qtd.shape
    nq = t // tq
    assert kv_off % n_kv == 0
    return pl.pallas_call(
        functools.partial(_attn_kernel, lam_init=lam_init, safe=safe),
        grid=(b, nq, n_kv),
        in_specs=[_const_spec(lam_p.shape),
                  _const_spec(gsub_col.shape),
                  pl.BlockSpec((1, N_MAPS, n_kv * tk), lambda bb, i, j: (bb, 0, kv_off // n_kv)),
                  pl.BlockSpec((1, DA_HEADS, HEAD_PAD, tq), lambda bb, i, j: (bb, 0, 0, i)),
                  pl.BlockSpec((1, MLA_HEADS, HEAD_PAD, tq), lambda bb, i, j: (bb, 0, 0, i)),
                  pl.BlockSpec((1, N_MAPS, tk, HEAD_PAD), lambda bb, i, j: (bb, 0, kv_off + j, 0)),
                  pl.BlockSpec((1, DA_HEADS, DA_DV, tk), lambda bb, i, j: (bb, 0, 0, kv_off + j)),
                  pl.BlockSpec((1, MLA_HEADS, MLA_DV, tk), lambda bb, i, j: (bb, 0, 0, kv_off + j))],
        out_specs=[pl.BlockSpec((1, DA_HEADS, DA_DV, tq), lambda bb, i, j: (bb, 0, 0, i)),
                   pl.BlockSpec((1, MLA_HEADS, MLA_DV, tq), lambda bb, i, j: (bb, 0, 0, i)),
                   pl.BlockSpec((1, 1, 8, 128), lambda bb, i, j: (bb, i, 0, 0))],
        out_shape=[jax.ShapeDtypeStruct((b, DA_HEADS, DA_DV, t), BF16),
                   jax.ShapeDtypeStruct((b, MLA_HEADS, MLA_DV, t), BF16),
                   jax.ShapeDtypeStruct((b, nq, 8, 128), F32)],
        scratch_shapes=[pltpu.VMEM((N_MAPS, HEAD_PAD, tq), BF16),
                        pltpu.VMEM((N_MAPS, 1, tq), F32),
                        pltpu.VMEM((N_MAPS, 8, tq), F32),
                        pltpu.VMEM((2 * DA_HEADS, DA_DV, tq), F32),
                        pltpu.VMEM((MLA_HEADS, MLA_DV, tq), F32)],
        compiler_params=_cparams(("parallel", "parallel", "arbitrary")),
        name="attention_safe" if safe else "attention",
    )(lam_p, gsub_col, kn, qtd, qtm, kh, vtd, vtm)


def _attention(lam_p, gsub_col, qtd, qtm, kv, lam_init, tq, tk, kv_off, n_kv):
    args = (lam_p, gsub_col, qtd, qtm, kv)
    otd, otm, lmin = _attention_call(*args, lam_init, tq, tk, kv_off, n_kv, False)
    ok = jnp.min(lmin) >= MIN_DENOMINATOR
    return lax.cond(ok, lambda *_: (otd, otm),
                    lambda *a: tuple(_attention_call(*a, lam_init, tq, tk, kv_off, n_kv, True)[:2]), *args)


def _odd_out_kernel(h_ref, mod_ref, g_ref, otd_ref, otm_ref, wt_ref, o_ref):
    y_t = _dot(wt_ref[:, :DA_VCOLS], otd_ref[0]) + _dot(wt_ref[:, DA_VCOLS:], otm_ref[0])
    y = y_t.T
    o_ref[0] = h_ref[0] + mod_ref[0][5:6] * _rms_rows(y, g_ref[3:4])


def _odd_out(h, mod, g, otd, otm, w_out_t):
    b, t, d = h.shape
    tm = _tile(t, 512)
    otd = otd.reshape(b, DA_VCOLS, t)
    otm = otm.reshape(b, MLA_HEADS * MLA_DV, t)
    return pl.pallas_call(
        _odd_out_kernel,
        grid=(b, t // tm),
        in_specs=[pl.BlockSpec((1, tm, d), lambda bb, i: (bb, i, 0)),
                  _mod_spec(mod),
                  _const_spec(g.shape),
                  pl.BlockSpec((1, DA_VCOLS, tm), lambda bb, i: (bb, 0, i)),
                  pl.BlockSpec((1, MLA_HEADS * MLA_DV, tm), lambda bb, i: (bb, 0, i)),
                  _const_spec(w_out_t.shape)],
        out_specs=pl.BlockSpec((1, tm, d), lambda bb, i: (bb, i, 0)),
        out_shape=jax.ShapeDtypeStruct(h.shape, F32),
        compiler_params=_cparams(("parallel", "parallel")),
        name="odd_out",
    )(h, mod, g, otd, otm, w_out_t)


def _rope_angles(n_tokens, rot_dim):
    rows = n_tokens // GRID_W
    row = jnp.broadcast_to(jnp.arange(rows)[:, None], (rows, GRID_W)).reshape(-1).astype(F32)
    col = jnp.broadcast_to(jnp.arange(GRID_W)[None, :], (rows, GRID_W)).reshape(-1).astype(F32)
    n_freq = rot_dim // 4
    freqs = ROPE_THETA ** (-jnp.arange(n_freq, dtype=F32) / n_freq)
    return row[:, None] * freqs, col[:, None] * freqs


def _rope_cos_sin(n_tokens, rot_dim, identity):
    if identity:
        return jnp.ones((n_tokens, rot_dim), F32), jnp.zeros((n_tokens, rot_dim), F32)
    ar, ac = _rope_angles(n_tokens, rot_dim)
    cos = jnp.concatenate([jnp.cos(ar), jnp.cos(ar), jnp.cos(ac), jnp.cos(ac)], axis=1)
    sin = jnp.concatenate([-jnp.sin(ar), jnp.sin(ar), -jnp.sin(ac), jnp.sin(ac)], axis=1)
    return cos, sin


def _rope_tables(n_tokens, identity, tk_total):
    cd, sd = _rope_cos_sin(n_tokens, DA_DK, identity)
    cm, sm = _rope_cos_sin(n_tokens, MLA_ROPE, identity)
    return {
        "cosk": jnp.concatenate([cd, cd], axis=1), "sink": jnp.concatenate([sd, sd], axis=1),
        "tkr": jnp.concatenate([cm, sm], axis=1),
        "cosq": cd.T, "sinq": sd.T, "cosm": cm.T, "sinm": sm.T,
        "tk_total": tk_total,
    }


def _swap_perm(rot_dim):
    q = rot_dim // 4
    return jnp.concatenate([jnp.arange(q, 2 * q), jnp.arange(0, q), jnp.arange(3 * q, 4 * q), jnp.arange(2 * q, 3 * q)])


def _odd_weights(w_in, g_q, w_uq, g_kv, w_uk, w_uv):
    w_q = w_in[:, :DA_QCOLS]
    w_cq = w_in[:, DA_QCOLS:Q_COLS]
    w_k = w_in[:, Q_COLS:Q_COLS + DA_QCOLS]
    w_v = w_in[:, Q_COLS + DA_QCOLS:Q_COLS + DA_QCOLS + DA_VCOLS]
    w_ckv = w_in[:, Q_COLS + DA_QCOLS + DA_VCOLS:Q_COLS + DA_QCOLS + DA_VCOLS + MLA_KV_RANK]
    w_kr = w_in[:, Q_COLS + DA_QCOLS + DA_VCOLS + MLA_KV_RANK:]
    wtok = jnp.concatenate([w_k, w_ckv, w_kr, w_kr[:, _swap_perm(MLA_ROPE)]], axis=1).astype(BF16)
    wt = jnp.concatenate([w_v, w_ckv, w_q, w_cq], axis=1).T.astype(BF16)
    pad_k = jnp.zeros((MLA_KV_RANK, MLA_HEADS, HEAD_PAD), F32)
    wukp = pad_k.at[:, :, :MLA_NOPE].set(w_uk.reshape(MLA_KV_RANK, MLA_HEADS, MLA_NOPE))
    wukp = wukp.reshape(MLA_KV_RANK, MLA_HEADS * HEAD_PAD).astype(BF16)
    pad_q = jnp.zeros((MLA_Q_RANK, MLA_HEADS, HEAD_PAD), F32)
    wuqp = pad_q.at[:, :, :MLA_DQK].set(w_uq.reshape(MLA_Q_RANK, MLA_HEADS, MLA_DQK))
    wuqt = wuqp.reshape(MLA_Q_RANK, MLA_HEADS * HEAD_PAD).T.astype(BF16)
    eye = jnp.eye(MLA_ROPE, dtype=F32)
    place = jnp.zeros((2, MLA_ROPE, MLA_HEADS, HEAD_PAD), F32)
    place = place.at[:, :, :, MLA_NOPE:MLA_DQK].set(jnp.broadcast_to(eye[None, :, None, :], (2, MLA_ROPE, MLA_HEADS, MLA_ROPE)))
    place = place.reshape(2 * MLA_ROPE, MLA_HEADS * HEAD_PAD).astype(BF16)
    return {
        "wtok": wtok, "wt": wt, "wukp": wukp, "wuvt": w_uv.T.astype(BF16), "wuqt": wuqt, "place": place,
        "gsel": jnp.kron(jnp.eye(N_MAPS, dtype=F32), jnp.ones((1, HEAD_PAD), F32)).astype(BF16),
        "gkv_row": g_kv.reshape(1, -1).astype(F32), "gkv_col": g_kv.reshape(-1, 1).astype(F32),
        "gq_col": g_q.reshape(-1, 1).astype(F32),
    }


def kernel(x, c, ctx, c_ctx, w_mod, b_mod, norm_g, w_ffn_in, w_ffn_out, w_in_even, conv_w, w_out_even,
           w_in_odd, g_q_mla, w_uq, g_kv_mla, w_uk, w_uv, lam_q1, lam_k1, lam_q2, lam_k2, g_subln, w_out_odd):
    b, t, d = x.shape
    tc = ctx.shape[1]
    depth = w_mod.shape[0]
    tk_total = t + tc

    rows = -(-(b + 1) // 8) * 8
    cond = jnp.zeros((rows, d), F32).at[:b].set(c).at[b].set(c_ctx)
    mod = _adaln(cond, w_mod, b_mod).reshape(depth, rows, N_MOD, d)

    dft_c = _channel_dft_table()
    w_pos_x, w_pos_c = _dft_tables(t), _dft_tables(tc)
    tabs_x = _rope_tables(t, False, tk_total)
    tabs_c = _rope_tables(tc, True, tk_total)
    tq = _tile(t, 512)
    tk = 768 if tk_total % 768 == 0 else _tile(tk_total, 512)
    assert t % tc == 0 and tk_total % tk == 0

    h, hc = x, ctx
    for l in range(depth):
        last = l == depth - 1
        odd = l % 2 == 1
        ctx_live = (not last) or odd
        g = norm_g[l]
        m_x, m_c = mod[l, :b], mod[l, b:b + 1]
        wi0, wo0 = w_ffn_in[l, 0].astype(BF16), w_ffn_out[l, 0].astype(BF16)
        h = _ffn(h, m_x, g, wi0, wo0, 0)
        if ctx_live:
            hc = _ffn(hc, m_c, g, wi0, wo0, 0)

        if not odd:
            e = l // 2
            w_in, w_out = w_in_even[e].astype(BF16), w_out_even[e].astype(BF16)
            h = _even_mixer(h, m_x, g, w_in, conv_w[e], w_out, dft_c, w_pos_x)
            if ctx_live:
                hc = _even_mixer(hc, m_c, g, w_in, conv_w[e], w_out, dft_c, w_pos_c)
        else:
            o = l // 2
            lam_init = 0.8 - 0.6 * math.exp(-0.3 * l)
            wts = _odd_weights(w_in_odd[o], g_q_mla[o], w_uq[o], g_kv_mla[o], w_uk[o], w_uv[o])
            lam_p = jnp.stack([lam_q1[o], lam_k1[o], lam_q2[o], lam_k2[o]]).astype(F32)
            gsub_col = g_subln[o].reshape(-1, 1).astype(F32)
            w_out_t = w_out_odd[o].T.astype(BF16)
            outs_x = _odd_in(h, m_x, g, wts, tabs_x, None, 0, True)
            qtd, qtm = outs_x[N_KV_ARRAYS:]
            outs_c = _odd_in(hc, m_c, g, wts, tabs_c, outs_x[:N_KV_ARRAYS], t, not last)
            kv = outs_c[:N_KV_ARRAYS]
            otd, otm = _attention(lam_p, gsub_col, qtd, qtm, kv, lam_init, tq, tk, 0, tk_total // tk)
            if not last:
                ocd, ocm = _attention_call(lam_p, gsub_col, outs_c[N_KV_ARRAYS], outs_c[N_KV_ARRAYS + 1], kv,
                                           lam_init, tc, tc, t // tc, 1, True)[:2]
                hc = _odd_out(hc, m_c, g, ocd, ocm, w_out_t)
            h = _odd_out(h, m_x, g, otd, otm, w_out_t)

        wi1, wo1 = w_ffn_in[l, 1].astype(BF16), w_ffn_out[l, 1].astype(BF16)
        h = _ffn(h, m_x, g, wi1, wo1, 2)
        if not last:
            hc = _ffn(hc, m_c, g, wi1, wo1, 2)
    return h
```

```python
import functools
import math

import jax
import jax.numpy as jnp
from jax import lax
from jax.experimental import pallas as pl
from jax.experimental.pallas import tpu as pltpu

F32 = jnp.float32
BF16 = jnp.bfloat16

D_MODEL = 1024
GRID_W = 64
N_MOD = 9
FFN_RES = 0.5
EPS = 1e-6
ROPE_THETA = 10000.0

D_CONV = 512
D_FOURIER = 512
FOURIER_GROUPS = 4
D_FG = D_FOURIER // FOURIER_GROUPS

DA_HEADS = 8
DA_DK = 64
DA_DV = 128
DA_SCALE = DA_DK ** -0.5
MLA_HEADS = 8
MLA_NOPE = 64
MLA_ROPE = 32
MLA_DQK = MLA_NOPE + MLA_ROPE
MLA_DV = 64
MLA_Q_RANK = 384
MLA_KV_RANK = 256
MLA_SCALE = MLA_DQK ** -0.5
DA_QCOLS = DA_HEADS * 2 * DA_DK
DA_VCOLS = DA_HEADS * DA_DV
Q_COLS = DA_QCOLS + MLA_Q_RANK
HEAD_PAD = 128
N_MAPS = 2 * DA_HEADS + MLA_HEADS
LOG2E = 1.4426950408889634
NEG_BIG = -1e30
N_KV_ARRAYS = 4
DA_HEADS_PER_ITER = 2
MLA_HEADS_PER_ITER = 4
BOUND_SLACK = 1.02
MIN_DENOMINATOR = 2.0 ** -40

VMEM_LIMIT_V7X = 56 * 1024 * 1024


def _cparams(sem):
    return pltpu.CompilerParams(dimension_semantics=sem, vmem_limit_bytes=VMEM_LIMIT_V7X)


def _tile(n, pref):
    if n <= pref:
        return n
    t = pref - pref % 128
    while t >= 128:
        if n % t == 0:
            return t
        t -= 128
    return n


def _const_spec(shape):
    nd = len(shape)
    return pl.BlockSpec(shape, lambda *_: (0,) * nd, pipeline_mode=pl.Buffered(1))


def _mod_spec(mod):
    if mod.shape[0] == 1:
        return pl.BlockSpec((1, N_MOD, D_MODEL), lambda b, *_: (0, 0, 0))
    return pl.BlockSpec((1, N_MOD, D_MODEL), lambda b, *_: (b, 0, 0))


def _rms_rows(x, g):
    ms = jnp.mean(x * x, axis=-1, keepdims=True)
    return x * lax.rsqrt(ms + EPS) * g


def _modulated(x, mod, g, slot):
    shift = mod[3 * slot:3 * slot + 1]
    scale = mod[3 * slot + 1:3 * slot + 2]
    return _rms_rows(x, g[2 * slot:2 * slot + 1]) * (1.0 + scale) + shift


def _dot(a, b):
    return jnp.dot(a, b, preferred_element_type=F32)


def _dot_nt(a, b):
    return lax.dot_general(a, b, (((1,), (1,)), ((), ())), preferred_element_type=F32)


def _adaln_kernel(c_ref, w_ref, b_ref, o_ref):
    c = c_ref[...]
    a = c * jax.nn.sigmoid(c)
    o_ref[0] = jnp.dot(a, w_ref[0], preferred_element_type=F32,
                       precision=lax.Precision.HIGHEST) + b_ref[0]


def _adaln(cond, w_mod, b_mod):
    depth, d, n = w_mod.shape
    rows = cond.shape[0]
    tn = _tile(n, 1152)
    return pl.pallas_call(
        _adaln_kernel,
        grid=(depth, n // tn),
        in_specs=[pl.BlockSpec((rows, d), lambda l, j: (0, 0)),
                  pl.BlockSpec((1, d, tn), lambda l, j: (l, 0, j)),
                  pl.BlockSpec((1, 1, tn), lambda l, j: (l, 0, j))],
        out_specs=pl.BlockSpec((1, rows, tn), lambda l, j: (l, 0, j)),
        out_shape=jax.ShapeDtypeStruct((depth, rows, n), F32),
        compiler_params=_cparams(("parallel", "parallel")),
        name="adaln",
    )(cond, w_mod, b_mod.reshape(depth, 1, n))


def _ffn_kernel(h_ref, mod_ref, g_ref, win_ref, wout_ref, o_ref, *, slot, d_ff):
    x = h_ref[0]
    mod = mod_ref[0]
    g = g_ref[...]
    xm = _modulated(x, mod, g, slot).astype(BF16)
    gate = _dot(xm, win_ref[:, :d_ff])
    up = _dot(xm, win_ref[:, d_ff:])
    act = (gate * jax.nn.sigmoid(gate) * up).astype(BF16)
    y = _dot(act, wout_ref[...])
    res_gate = mod[3 * slot + 2:3 * slot + 3]
    o_ref[0] = x + FFN_RES * res_gate * _rms_rows(y, g[2 * slot + 1:2 * slot + 2])


def _ffn(h, mod, g, w_in, w_out, slot):
    b, t, d = h.shape
    d_ff = w_out.shape[0]
    tm = _tile(t, 512)
    return pl.pallas_call(
        functools.partial(_ffn_kernel, slot=slot, d_ff=d_ff),
        grid=(b, t // tm),
        in_specs=[pl.BlockSpec((1, tm, d), lambda bb, i: (bb, i, 0)),
                  _mod_spec(mod),
                  _const_spec(g.shape),
                  _const_spec(w_in.shape),
                  _const_spec(w_out.shape)],
        out_specs=pl.BlockSpec((1, tm, d), lambda bb, i: (bb, i, 0)),
        out_shape=jax.ShapeDtypeStruct(h.shape, F32),
        compiler_params=_cparams(("parallel", "parallel")),
        name="ffn",
    )(h, mod, g, w_in, w_out)


def _even_in_kernel(h_ref, mod_ref, g_ref, w_ref, dft_ref, gb_ref, z_ref, a_ref):
    xm = _modulated(h_ref[0], mod_ref[0], g_ref[...], 1).astype(BF16)
    u = _dot(xm, w_ref[...])
    gb_ref[0] = u[:, :D_CONV].astype(BF16)
    z_ref[0] = (u[:, D_CONV:2 * D_CONV] * u[:, 2 * D_CONV:3 * D_CONV]).astype(BF16)
    xf = u[:, 3 * D_CONV:].astype(BF16)
    for gi in range(FOURIER_GROUPS):
        pq = _dot(xf[:, gi * D_FG:(gi + 1) * D_FG], dft_ref[...])
        a_ref[0, :, gi * D_FG:(gi + 1) * D_FG] = pq[:, :D_FG].astype(BF16)
        a_ref[1, :, gi * D_FG:(gi + 1) * D_FG] = pq[:, D_FG:].astype(BF16)


def _even_in(h, mod, g, w_in, dft_c):
    b, t, d = h.shape
    tm = _tile(t, 512)
    return pl.pallas_call(
        _even_in_kernel,
        grid=(b, t // tm),
        in_specs=[pl.BlockSpec((1, tm, d), lambda bb, i: (bb, i, 0)),
                  _mod_spec(mod),
                  _const_spec(g.shape),
                  _const_spec(w_in.shape),
                  _const_spec(dft_c.shape)],
        out_specs=[pl.BlockSpec((1, tm, D_CONV), lambda bb, i: (bb, i, 0)),
                   pl.BlockSpec((1, tm, D_CONV), lambda bb, i: (bb, i, 0)),
                   pl.BlockSpec((2, tm, D_FOURIER), lambda bb, i: (0, i, bb))],
        out_shape=[jax.ShapeDtypeStruct((b, t, D_CONV), BF16),
                   jax.ShapeDtypeStruct((b, t, D_CONV), BF16),
                   jax.ShapeDtypeStruct((2, t, b * D_FOURIER), BF16)],
        compiler_params=_cparams(("parallel", "parallel")),
        name="even_in",
    )(h, mod, g, w_in, dft_c)


def _matmul_kernel(a_ref, b_ref, o_ref, acc_ref, *, scale):
    k = pl.program_id(2)

    @pl.when(k == 0)
    def _():
        acc_ref[...] = jnp.zeros_like(acc_ref)

    acc_ref[...] += _dot(a_ref[...], b_ref[...])

    @pl.when(k == pl.num_programs(2) - 1)
    def _():
        o_ref[...] = (acc_ref[...] * scale).astype(o_ref.dtype)


def _matmul(a, b, scale, out_dtype):
    m, kk = a.shape
    n = b.shape[1]
    bm, bn, bk = _tile(m, 1024), _tile(n, 1024), _tile(kk, 2048)
    return pl.pallas_call(
        functools.partial(_matmul_kernel, scale=scale),
        grid=(m // bm, n // bn, kk // bk),
        in_specs=[pl.BlockSpec((bm, bk), lambda i, j, k: (i, k)),
                  pl.BlockSpec((bk, bn), lambda i, j, k: (k, j))],
        out_specs=pl.BlockSpec((bm, bn), lambda i, j, k: (i, j)),
        out_shape=jax.ShapeDtypeStruct((m, n), out_dtype),
        scratch_shapes=[pltpu.VMEM((bm, bn), F32)],
        compiler_params=_cparams(("parallel", "parallel", "arbitrary")),
        name="dft_matmul",
    )(a, b)


def _even_out_kernel(h_ref, mod_ref, g_ref, gb_ref, z_ref, zp_ref, zn_ref, yf_ref, cw_ref, w_ref, o_ref):
    i = pl.program_id(1)
    x = h_ref[0]
    mod = mod_ref[0]
    g = g_ref[...]
    z = z_ref[0].astype(F32)
    tm = z.shape[0]
    halo = zp_ref.shape[1]
    prev_row = jnp.where(i > 0, zp_ref[0, halo - 1:halo, :].astype(F32), 0.0)
    next_row = jnp.where(i < pl.num_programs(1) - 1, zn_ref[0, 0:1, :].astype(F32), 0.0)
    row = lax.broadcasted_iota(jnp.int32, z.shape, 0)
    z_before = jnp.where(row == 0, prev_row, pltpu.roll(z, 1, 0))
    z_after = jnp.where(row == tm - 1, next_row, pltpu.roll(z, tm - 1, 0))
    cw = cw_ref[...]
    conv = z_before * cw[0:1] + z * cw[1:2] + z_after * cw[2:3]
    y_conv = (gb_ref[0].astype(F32) * conv).astype(BF16)
    y = _dot(y_conv, w_ref[:D_CONV, :]) + _dot(yf_ref[...], w_ref[D_CONV:, :])
    o_ref[0] = x + mod[5:6] * _rms_rows(y, g[3:4])


def _even_out(h, mod, g, gb, z, yf, conv_w, w_out):
    b, t, d = h.shape
    tm = _tile(t, 512)
    halo = 16
    nh = tm // halo
    last_halo = t // halo - 1
    return pl.pallas_call(
        _even_out_kernel,
        grid=(b, t // tm),
        in_specs=[pl.BlockSpec((1, tm, d), lambda bb, i: (bb, i, 0)),
                  _mod_spec(mod),
                  _const_spec(g.shape),
                  pl.BlockSpec((1, tm, D_CONV), lambda bb, i: (bb, i, 0)),
                  pl.BlockSpec((1, tm, D_CONV), lambda bb, i: (bb, i, 0)),
                  pl.BlockSpec((1, halo, D_CONV), lambda bb, i: (bb, jnp.maximum(i * nh - 1, 0), 0)),
                  pl.BlockSpec((1, halo, D_CONV), lambda bb, i: (bb, jnp.minimum((i + 1) * nh, last_halo), 0)),
                  pl.BlockSpec((tm, D_FOURIER), lambda bb, i: (i, bb)),
                  _const_spec(conv_w.shape),
                  _const_spec(w_out.shape)],
        out_specs=pl.BlockSpec((1, tm, d), lambda bb, i: (bb, i, 0)),
        out_shape=jax.ShapeDtypeStruct(h.shape, F32),
        compiler_params=_cparams(("parallel", "parallel")),
        name="even_out",
    )(h, mod, g, gb, z, z, z, yf, conv_w, w_out)


def _dft_tables(t):
    n = jnp.arange(t, dtype=jnp.int32)
    ang = ((n[:, None] * n[None, :]) % t).astype(F32) * (2.0 * math.pi / t)
    w_pos = jnp.concatenate([jnp.cos(ang), -jnp.sin(ang)], axis=1).astype(BF16)
    return w_pos


def _channel_dft_table():
    n = jnp.arange(D_FG, dtype=jnp.int32)
    ang = ((n[:, None] * n[None, :]) % D_FG).astype(F32) * (2.0 * math.pi / D_FG)
    return jnp.concatenate([jnp.cos(ang), jnp.sin(ang)], axis=1).astype(BF16)


def _even_mixer(h, mod, g, w_in, conv_w, w_out, dft_c, w_pos):
    b, t, _ = h.shape
    gb, z, a = _even_in(h, mod, g, w_in, dft_c)
    yf = _matmul(w_pos, a.reshape(2 * t, b * D_FOURIER), 1.0 / math.sqrt(t * D_FG), BF16)
    return _even_out(h, mod, g, gb, z, yf, conv_w, w_out)


def _odd_in_kernel(h_ref, mod_ref, g_ref, wtok_ref, wt_ref, wukp_ref, wuvt_ref, wuqt_ref, place_ref, gsel_ref,
                   gkv_row_ref, gkv_col_ref, gq_col_ref,
                   cosk_ref, sink_ref, tkr_ref, cosq_ref, sinq_ref, cosm_ref, sinm_ref,
                   kh_ref, vtd_ref, vtm_ref, kn_ref, *q_refs, with_q):
    xm = _modulated(h_ref[0], mod_ref[0], g_ref[...], 1).astype(BF16)
    tm = xm.shape[0]

    ut = _dot(xm, wtok_ref[...])
    k = ut[:, :DA_QCOLS]
    ckv = ut[:, DA_QCOLS:DA_QCOLS + MLA_KV_RANK]
    kr2 = ut[:, DA_QCOLS + MLA_KV_RANK:]
    lane = lax.broadcasted_iota(jnp.int32, k.shape, 1)
    first_half = (lane % (DA_DK // 2)) < (DA_DK // 4)
    k_sw = jnp.where(first_half, pltpu.roll(k, DA_QCOLS - DA_DK // 4, 1), pltpu.roll(k, DA_DK // 4, 1))
    reps = DA_QCOLS // HEAD_PAD
    cosk = jnp.concatenate([cosk_ref[...]] * reps, axis=1)
    sink = jnp.concatenate([sink_ref[...]] * reps, axis=1)
    k_rot = k * cosk + k_sw * sink
    ckvn = _rms_rows(ckv, gkv_row_ref[...]).astype(BF16)
    k_nope = _dot(ckvn, wukp_ref[...])
    pr = kr2 * tkr_ref[...]
    pr_hi = pr.astype(BF16)
    pr_lo = (pr - pr_hi.astype(F32)).astype(BF16)
    k_mla = k_nope + _dot(pr_hi, place_ref[...]) + _dot(pr_lo, place_ref[...])
    k_odd = pltpu.roll(k_rot, DA_QCOLS - DA_DK, 1)
    lane_h = lax.broadcasted_iota(jnp.int32, (tm, HEAD_PAD), 1)
    sq = []
    for mp in range(N_MAPS):
        if mp < 2 * DA_HEADS:
            src = (k_rot if mp % 2 == 0 else k_odd)[:, (mp // 2) * HEAD_PAD:(mp // 2 + 1) * HEAD_PAD]
            kd = jnp.where(lane_h < DA_DK, src, 0.0)
            one_lane = DA_DK
        else:
            hh = mp - 2 * DA_HEADS
            kd = k_mla[:, hh * HEAD_PAD:(hh + 1) * HEAD_PAD]
            one_lane = MLA_DQK
        kh_ref[0, mp] = jnp.where(lane_h == one_lane, 1.0, kd).astype(BF16)
        kf = kd.astype(BF16).astype(F32)
        sq.append((kf * kf).astype(BF16))
    kn_ref[0] = _dot_nt(gsel_ref[...], jnp.concatenate(sq, axis=1))

    r0 = DA_VCOLS
    vckv = _dot_nt(wt_ref[:r0 + MLA_KV_RANK, :], xm)
    for hh in range(DA_HEADS):
        vtd_ref[0, hh] = vckv[hh * DA_DV:(hh + 1) * DA_DV].astype(BF16)
    ckv_t = vckv[r0:]
    ms = jnp.mean(ckv_t * ckv_t, axis=0, keepdims=True)
    ckvn_t = (ckv_t * lax.rsqrt(ms + EPS) * gkv_col_ref[...]).astype(BF16)
    vm_t = _dot(wuvt_ref[...], ckvn_t)
    for hh in range(MLA_HEADS):
        vtm_ref[0, hh] = vm_t[hh * MLA_DV:(hh + 1) * MLA_DV].astype(BF16)

    if with_q:
        qtd_ref, qtm_ref = q_refs
        r1 = r0 + MLA_KV_RANK
        q_t = _dot_nt(wt_ref[r1:, :], xm)
        cosq, sinq = cosq_ref[...], sinq_ref[...]
        qd = DA_DK // 4
        for mp in range(2 * DA_HEADS):
            q = q_t[mp * DA_DK:(mp + 1) * DA_DK]
            q_sw = jnp.concatenate([q[qd:2 * qd], q[:qd], q[3 * qd:], q[2 * qd:3 * qd]], axis=0)
            q_rot = (q * cosq + q_sw * sinq) * (DA_SCALE * LOG2E)
            qtd_ref[0, mp // 2, (mp % 2) * DA_DK:(mp % 2 + 1) * DA_DK, :] = q_rot.astype(BF16)
        cq_t = q_t[DA_QCOLS:]
        ms = jnp.mean(cq_t * cq_t, axis=0, keepdims=True)
        cqn_t = (cq_t * lax.rsqrt(ms + EPS) * gq_col_ref[...]).astype(BF16)
        qm_t = _dot(wuqt_ref[...], cqn_t) * (MLA_SCALE * LOG2E)
        cosm, sinm = cosm_ref[...], sinm_ref[...]
        rd = MLA_ROPE // 4
        for hh in range(MLA_HEADS):
            base = hh * HEAD_PAD
            qtm_ref[0, hh, :MLA_NOPE, :] = qm_t[base:base + MLA_NOPE].astype(BF16)
            r = qm_t[base + MLA_NOPE:base + MLA_DQK]
            r_sw = jnp.concatenate([r[rd:2 * rd], r[:rd], r[3 * rd:], r[2 * rd:3 * rd]], axis=0)
            qtm_ref[0, hh, MLA_NOPE:MLA_DQK, :] = (r * cosm + r_sw * sinm).astype(BF16)
            qtm_ref[0, hh, MLA_DQK:, :] = jnp.zeros((HEAD_PAD - MLA_DQK, tm), BF16)


def _odd_in(h, mod, g, wts, tabs, kv_prev, tok_off, with_q):
    b, t, d = h.shape
    tk_total = tabs["tk_total"]
    tm = _tile(t, 256)
    assert tok_off % tm == 0
    off = tok_off // tm
    row_tab = lambda w: pl.BlockSpec((tm, w), lambda bb, i: (i, 0))
    col_tab = lambda r: pl.BlockSpec((r, tm), lambda bb, i: (0, i))
    consts = [wts["wtok"], wts["wt"], wts["wukp"], wts["wuvt"], wts["wuqt"], wts["place"], wts["gsel"],
              wts["gkv_row"], wts["gkv_col"], wts["gq_col"]]
    in_specs = ([pl.BlockSpec((1, tm, d), lambda bb, i: (bb, i, 0)), _mod_spec(mod), _const_spec(g.shape)]
                + [_const_spec(c.shape) for c in consts]
                + [row_tab(HEAD_PAD), row_tab(HEAD_PAD), row_tab(2 * MLA_ROPE),
                   col_tab(DA_DK), col_tab(DA_DK), col_tab(MLA_ROPE), col_tab(MLA_ROPE)])
    args = [h, mod, g] + consts + [tabs["cosk"], tabs["sink"], tabs["tkr"],
                                   tabs["cosq"], tabs["sinq"], tabs["cosm"], tabs["sinm"]]
    out_specs = [pl.BlockSpec((1, N_MAPS, tm, HEAD_PAD), lambda bb, i: (bb, 0, i + off, 0)),
                 pl.BlockSpec((1, DA_HEADS, DA_DV, tm), lambda bb, i: (bb, 0, 0, i + off)),
                 pl.BlockSpec((1, MLA_HEADS, MLA_DV, tm), lambda bb, i: (bb, 0, 0, i + off)),
                 pl.BlockSpec((1, N_MAPS, tm), lambda bb, i: (bb, 0, i + off))]
    out_shape = [jax.ShapeDtypeStruct((b, N_MAPS, tk_total, HEAD_PAD), BF16),
                 jax.ShapeDtypeStruct((b, DA_HEADS, DA_DV, tk_total), BF16),
                 jax.ShapeDtypeStruct((b, MLA_HEADS, MLA_DV, tk_total), BF16),
                 jax.ShapeDtypeStruct((b, N_MAPS, tk_total), F32)]
    if with_q:
        out_specs += [pl.BlockSpec((1, DA_HEADS, HEAD_PAD, tm), lambda bb, i: (bb, 0, 0, i)),
                      pl.BlockSpec((1, MLA_HEADS, HEAD_PAD, tm), lambda bb, i: (bb, 0, 0, i))]
        out_shape += [jax.ShapeDtypeStruct((b, DA_HEADS, HEAD_PAD, t), BF16),
                      jax.ShapeDtypeStruct((b, MLA_HEADS, HEAD_PAD, t), BF16)]
    aliases = {}
    kernel_fn = functools.partial(_odd_in_kernel, with_q=with_q)
    if kv_prev is not None:
        n_in = len(args)
        in_specs += [pl.BlockSpec(memory_space=pl.ANY)] * N_KV_ARRAYS
        args += list(kv_prev)
        aliases = {n_in + a: a for a in range(N_KV_ARRAYS)}
        kernel_fn = functools.partial(_odd_in_alias_kernel, n_in=n_in, with_q=with_q)
    return pl.pallas_call(
        kernel_fn,
        grid=(b, t // tm),
        in_specs=in_specs,
        out_specs=out_specs,
        out_shape=out_shape,
        input_output_aliases=aliases,
        compiler_params=_cparams(("parallel", "parallel")),
        name="odd_in",
    )(*args)


def _odd_in_alias_kernel(*refs, n_in, with_q):
    _odd_in_kernel(*refs[:n_in], *refs[n_in + N_KV_ARRAYS:], with_q=with_q)


def _attn_kernel(lam_ref, gsub_ref, kn_ref, qtd_ref, qtm_ref, kh_ref, vtd_ref, vtm_ref,
                 otd_ref, otm_ref, lmin_ref, qa_ref, m_ref, l_ref, accd_ref, accm_ref, *, lam_init, safe):
    j = pl.program_id(2)
    tq = qtd_ref.shape[-1]

    @pl.when(j == 0)
    def _():
        m_ref[...] = jnp.full_like(m_ref, NEG_BIG)
        l_ref[...] = jnp.zeros_like(l_ref)
        accd_ref[...] = jnp.zeros_like(accd_ref)
        accm_ref[...] = jnp.zeros_like(accm_ref)
        kmax = jnp.sqrt(jnp.max(kn_ref[0], axis=1, keepdims=True))
        shift_row = lax.broadcasted_iota(jnp.int32, (16, tq), 0) == 0
        for idx in range(N_MAPS):
            if idx < 2 * DA_HEADS:
                q = qtd_ref[0, idx // 2, (idx % 2) * DA_DK:(idx % 2 + 1) * DA_DK, :]
            else:
                q = qtm_ref[0, idx - 2 * DA_HEADS, :MLA_DQK, :]
            if safe:
                shift_blk = jnp.zeros((16, tq), BF16)
            else:
                qf = q.astype(F32)
                qn = jnp.sqrt(jnp.sum(qf * qf, axis=0, keepdims=True))
                shift = -(BOUND_SLACK * kmax[idx:idx + 1]) * qn
                shift_blk = jnp.where(shift_row, shift, 0.0).astype(BF16)
            pad = jnp.zeros((HEAD_PAD - q.shape[0] - 16, tq), BF16)
            qa_ref[idx] = jnp.concatenate([q, shift_blk, pad], axis=0)

    def one_map(idx, v_t, acc_ref, acc_idx):
        s = _dot(kh_ref[0, idx], qa_ref[idx])
        if safe:
            m_old = m_ref[idx]
            m_new = jnp.maximum(m_old, jnp.max(s, axis=0, keepdims=True))
            alpha = jnp.exp2(m_old - m_new)
            m_ref[idx] = m_new
            p = jnp.exp2(s - m_new)
            l_ref[idx] = alpha * l_ref[idx] + jnp.sum(p.reshape(-1, 8, tq), axis=0)
            acc_ref[acc_idx] = acc_ref[acc_idx] * alpha + _dot(v_t, p.astype(BF16))
        else:
            p = jnp.exp2(s)
            l_ref[idx] += jnp.sum(p.reshape(-1, 8, tq), axis=0)
            acc_ref[acc_idx] += _dot(v_t, p.astype(BF16))

    def da_head(hh, carry):
        v_t = vtd_ref[0, hh]
        one_map(2 * hh, v_t, accd_ref, 2 * hh)
        one_map(2 * hh + 1, v_t, accd_ref, 2 * hh + 1)
        return carry

    def mla_head(hh, carry):
        one_map(2 * DA_HEADS + hh, vtm_ref[0, hh], accm_ref, hh)
        return carry

    lax.fori_loop(0, DA_HEADS // DA_HEADS_PER_ITER,
                  lambda it, c: [da_head(it * DA_HEADS_PER_ITER + u, c) for u in range(DA_HEADS_PER_ITER)][-1], 0)
    lax.fori_loop(0, MLA_HEADS // MLA_HEADS_PER_ITER,
                  lambda it, c: [mla_head(it * MLA_HEADS_PER_ITER + u, c) for u in range(MLA_HEADS_PER_ITER)][-1], 0)

    @pl.when(j == pl.num_programs(2) - 1)
    def _():
        lp = lam_ref[...]
        lam = (jnp.exp(jnp.sum(lp[0:1] * lp[1:2], axis=1, keepdims=True))
               - jnp.exp(jnp.sum(lp[2:3] * lp[3:4], axis=1, keepdims=True)) + lam_init)
        gsub = gsub_ref[...]
        lsum = [jnp.sum(l_ref[idx], axis=0, keepdims=True) for idx in range(N_MAPS)]
        for hh in range(DA_HEADS):
            o = accd_ref[2 * hh] / lsum[2 * hh] - lam * (accd_ref[2 * hh + 1] / lsum[2 * hh + 1])
            ms = jnp.mean(o * o, axis=0, keepdims=True)
            otd_ref[0, hh] = (o * lax.rsqrt(ms + EPS) * gsub * (1.0 - lam_init)).astype(otd_ref.dtype)
        for hh in range(MLA_HEADS):
            otm_ref[0, hh] = (accm_ref[hh] / lsum[2 * DA_HEADS + hh]).astype(otm_ref.dtype)
        lmin = lsum[0]
        for idx in range(1, N_MAPS):
            lmin = jnp.minimum(lmin, lsum[idx])
        lmin_ref[0, 0] = jnp.broadcast_to(jnp.min(lmin, axis=1, keepdims=True), lmin_ref.shape[2:])


def _attention_call(lam_p, gsub_col, qtd, qtm, kv, lam_init, tq, tk, kv_off, n_kv, safe):
    kh, vtd, vtm, kn = kv
    b, _, _, t = qtd.shape
    nq = t // tq
    assert kv_off % n_kv == 0
    return pl.pallas_call(
        functools.partial(_attn_kernel, lam_init=lam_init, safe=safe),
        grid=(b, nq, n_kv),
        in_specs=[_const_spec(lam_p.shape),
                  _const_spec(gsub_col.shape),
                  pl.BlockSpec((1, N_MAPS, n_kv * tk), lambda bb, i, j: (bb, 0, kv_off // n_kv)),
                  pl.BlockSpec((1, DA_HEADS, HEAD_PAD, tq), lambda bb, i, j: (bb, 0, 0, i)),
                  pl.BlockSpec((1, MLA_HEADS, HEAD_PAD, tq), lambda bb, i, j: (bb, 0, 0, i)),
                  pl.BlockSpec((1, N_MAPS, tk, HEAD_PAD), lambda bb, i, j: (bb, 0, kv_off + j, 0)),
                  pl.BlockSpec((1, DA_HEADS, DA_DV, tk), lambda bb, i, j: (bb, 0, 0, kv_off + j)),
                  pl.BlockSpec((1, MLA_HEADS, MLA_DV, tk), lambda bb, i, j: (bb, 0, 0, kv_off + j))],
        out_specs=[pl.BlockSpec((1, DA_HEADS, DA_DV, tq), lambda bb, i, j: (bb, 0, 0, i)),
                   pl.BlockSpec((1, MLA_HEADS, MLA_DV, tq), lambda bb, i, j: (bb, 0, 0, i)),
                   pl.BlockSpec((1, 1, 8, 128), lambda bb, i, j: (bb, i, 0, 0))],
        out_shape=[jax.ShapeDtypeStruct((b, DA_HEADS, DA_DV, t), BF16),
                   jax.ShapeDtypeStruct((b, MLA_HEADS, MLA_DV, t), BF16),
                   jax.ShapeDtypeStruct((b, nq, 8, 128), F32)],
        scratch_shapes=[pltpu.VMEM((N_MAPS, HEAD_PAD, tq), BF16),
                        pltpu.VMEM((N_MAPS, 1, tq), F32),
                        pltpu.VMEM((N_MAPS, 8, tq), F32),
                        pltpu.VMEM((2 * DA_HEADS, DA_DV, tq), F32),
                        pltpu.VMEM((MLA_HEADS, MLA_DV, tq), F32)],
        compiler_params=_cparams(("parallel", "parallel", "arbitrary")),
        name="attention_safe" if safe else "attention",
    )(lam_p, gsub_col, kn, qtd, qtm, kh, vtd, vtm)


def _attention(lam_p, gsub_col, qtd, qtm, kv, lam_init, tq, tk, kv_off, n_kv):
    args = (lam_p, gsub_col, qtd, qtm, kv)
    otd, otm, lmin = _attention_call(*args, lam_init, tq, tk, kv_off, n_kv, False)
    ok = jnp.min(lmin) >= MIN_DENOMINATOR
    return lax.cond(ok, lambda *_: (otd, otm),
                    lambda *a: tuple(_attention_call(*a, lam_init, tq, tk, kv_off, n_kv, True)[:2]), *args)


def _odd_out_kernel(h_ref, mod_ref, g_ref, otd_ref, otm_ref, wt_ref, o_ref):
    y_t = _dot(wt_ref[:, :DA_VCOLS], otd_ref[0]) + _dot(wt_ref[:, DA_VCOLS:], otm_ref[0])
    y = y_t.T
    o_ref[0] = h_ref[0] + mod_ref[0][5:6] * _rms_rows(y, g_ref[3:4])


def _odd_out(h, mod, g, otd, otm, w_out_t):
    b, t, d = h.shape
    tm = _tile(t, 512)
    otd = otd.reshape(b, DA_VCOLS, t)
    otm = otm.reshape(b, MLA_HEADS * MLA_DV, t)
    return pl.pallas_call(
        _odd_out_kernel,
        grid=(b, t // tm),
        in_specs=[pl.BlockSpec((1, tm, d), lambda bb, i: (bb, i, 0)),
                  _mod_spec(mod),
                  _const_spec(g.shape),
                  pl.BlockSpec((1, DA_VCOLS, tm), lambda bb, i: (bb, 0, i)),
                  pl.BlockSpec((1, MLA_HEADS * MLA_DV, tm), lambda bb, i: (bb, 0, i)),
                  _const_spec(w_out_t.shape)],
        out_specs=pl.BlockSpec((1, tm, d), lambda bb, i: (bb, i, 0)),
        out_shape=jax.ShapeDtypeStruct(h.shape, F32),
        compiler_params=_cparams(("parallel", "parallel")),
        name="odd_out",
    )(h, mod, g, otd, otm, w_out_t)


def _rope_angles(n_tokens, rot_dim):
    rows = n_tokens // GRID_W
    row = jnp.broadcast_to(jnp.arange(rows)[:, None], (rows, GRID_W)).reshape(-1).astype(F32)
    col = jnp.broadcast_to(jnp.arange(GRID_W)[None, :], (rows, GRID_W)).reshape(-1).astype(F32)
    n_freq = rot_dim // 4
    freqs = ROPE_THETA ** (-jnp.arange(n_freq, dtype=F32) / n_freq)
    return row[:, None] * freqs, col[:, None] * freqs


def _rope_cos_sin(n_tokens, rot_dim, identity):
    if identity:
        return jnp.ones((n_tokens, rot_dim), F32), jnp.zeros((n_tokens, rot_dim), F32)
    ar, ac = _rope_angles(n_tokens, rot_dim)
    cos = jnp.concatenate([jnp.cos(ar), jnp.cos(ar), jnp.cos(ac), jnp.cos(ac)], axis=1)
    sin = jnp.concatenate([-jnp.sin(ar), jnp.sin(ar), -jnp.sin(ac), jnp.sin(ac)], axis=1)
    return cos, sin


def _rope_tables(n_tokens, identity, tk_total):
    cd, sd = _rope_cos_sin(n_tokens, DA_DK, identity)
    cm, sm = _rope_cos_sin(n_tokens, MLA_ROPE, identity)
    return {
        "cosk": jnp.concatenate([cd, cd], axis=1), "sink": jnp.concatenate([sd, sd], axis=1),
        "tkr": jnp.concatenate([cm, sm], axis=1),
        "cosq": cd.T, "sinq": sd.T, "cosm": cm.T, "sinm": sm.T,
        "tk_total": tk_total,
    }


def _swap_perm(rot_dim):
    q = rot_dim // 4
    return jnp.concatenate([jnp.arange(q, 2 * q), jnp.arange(0, q), jnp.arange(3 * q, 4 * q), jnp.arange(2 * q, 3 * q)])


def _odd_weights(w_in, g_q, w_uq, g_kv, w_uk, w_uv):
    w_q = w_in[:, :DA_QCOLS]
    w_cq = w_in[:, DA_QCOLS:Q_COLS]
    w_k = w_in[:, Q_COLS:Q_COLS + DA_QCOLS]
    w_v = w_in[:, Q_COLS + DA_QCOLS:Q_COLS + DA_QCOLS + DA_VCOLS]
    w_ckv = w_in[:, Q_COLS + DA_QCOLS + DA_VCOLS:Q_COLS + DA_QCOLS + DA_VCOLS + MLA_KV_RANK]
    w_kr = w_in[:, Q_COLS + DA_QCOLS + DA_VCOLS + MLA_KV_RANK:]
    wtok = jnp.concatenate([w_k, w_ckv, w_kr, w_kr[:, _swap_perm(MLA_ROPE)]], axis=1).astype(BF16)
    wt = jnp.concatenate([w_v, w_ckv, w_q, w_cq], axis=1).T.astype(BF16)
    pad_k = jnp.zeros((MLA_KV_RANK, MLA_HEADS, HEAD_PAD), F32)
    wukp = pad_k.at[:, :, :MLA_NOPE].set(w_uk.reshape(MLA_KV_RANK, MLA_HEADS, MLA_NOPE))
    wukp = wukp.reshape(MLA_KV_RANK, MLA_HEADS * HEAD_PAD).astype(BF16)
    pad_q = jnp.zeros((MLA_Q_RANK, MLA_HEADS, HEAD_PAD), F32)
    wuqp = pad_q.at[:, :, :MLA_DQK].set(w_uq.reshape(MLA_Q_RANK, MLA_HEADS, MLA_DQK))
    wuqt = wuqp.reshape(MLA_Q_RANK, MLA_HEADS * HEAD_PAD).T.astype(BF16)
    eye = jnp.eye(MLA_ROPE, dtype=F32)
    place = jnp.zeros((2, MLA_ROPE, MLA_HEADS, HEAD_PAD), F32)
    place = place.at[:, :, :, MLA_NOPE:MLA_DQK].set(jnp.broadcast_to(eye[None, :, None, :], (2, MLA_ROPE, MLA_HEADS, MLA_ROPE)))
    place = place.reshape(2 * MLA_ROPE, MLA_HEADS * HEAD_PAD).astype(BF16)
    return {
        "wtok": wtok, "wt": wt, "wukp": wukp, "wuvt": w_uv.T.astype(BF16), "wuqt": wuqt, "place": place,
        "gsel": jnp.kron(jnp.eye(N_MAPS, dtype=F32), jnp.ones((1, HEAD_PAD), F32)).astype(BF16),
        "gkv_row": g_kv.reshape(1, -1).astype(F32), "gkv_col": g_kv.reshape(-1, 1).astype(F32),
        "gq_col": g_q.reshape(-1, 1).astype(F32),
    }


def kernel(x, c, ctx, c_ctx, w_mod, b_mod, norm_g, w_ffn_in, w_ffn_out, w_in_even, conv_w, w_out_even,
           w_in_odd, g_q_mla, w_uq, g_kv_mla, w_uk, w_uv, lam_q1, lam_k1, lam_q2, lam_k2, g_subln, w_out_odd):
    b, t, d = x.shape
    tc = ctx.shape[1]
    depth = w_mod.shape[0]
    tk_total = t + tc

    rows = -(-(b + 1) // 8) * 8
    cond = jnp.zeros((rows, d), F32).at[:b].set(c).at[b].set(c_ctx)
    mod = _adaln(cond, w_mod, b_mod).reshape(depth, rows, N_MOD, d)

    dft_c = _channel_dft_table()
    w_pos_x, w_pos_c = _dft_tables(t), _dft_tables(tc)
    tabs_x = _rope_tables(t, False, tk_total)
    tabs_c = _rope_tables(tc, True, tk_total)
    tq = _tile(t, 1024)
    tk = 768 if tk_total % 768 == 0 else _tile(tk_total, 512)
    assert t % tc == 0 and tk_total % tk == 0

    h, hc = x, ctx
    for l in range(depth):
        last = l == depth - 1
        odd = l % 2 == 1
        ctx_live = (not last) or odd
        g = norm_g[l]
        m_x, m_c = mod[l, :b], mod[l, b:b + 1]
        wi0, wo0 = w_ffn_in[l, 0].astype(BF16), w_ffn_out[l, 0].astype(BF16)
        h = _ffn(h, m_x, g, wi0, wo0, 0)
        if ctx_live:
            hc = _ffn(hc, m_c, g, wi0, wo0, 0)

        if not odd:
            e = l // 2
            w_in, w_out = w_in_even[e].astype(BF16), w_out_even[e].astype(BF16)
            h = _even_mixer(h, m_x, g, w_in, conv_w[e], w_out, dft_c, w_pos_x)
            if ctx_live:
                hc = _even_mixer(hc, m_c, g, w_in, conv_w[e], w_out, dft_c, w_pos_c)
        else:
            o = l // 2
            lam_init = 0.8 - 0.6 * math.exp(-0.3 * l)
            wts = _odd_weights(w_in_odd[o], g_q_mla[o], w_uq[o], g_kv_mla[o], w_uk[o], w_uv[o])
            lam_p = jnp.stack([lam_q1[o], lam_k1[o], lam_q2[o], lam_k2[o]]).astype(F32)
            gsub_col = g_subln[o].reshape(-1, 1).astype(F32)
            w_out_t = w_out_odd[o].T.astype(BF16)
            outs_x = _odd_in(h, m_x, g, wts, tabs_x, None, 0, True)
            qtd, qtm = outs_x[N_KV_ARRAYS:]
            outs_c = _odd_in(hc, m_c, g, wts, tabs_c, outs_x[:N_KV_ARRAYS], t, not last)
            kv = outs_c[:N_KV_ARRAYS]
            otd, otm = _attention(lam_p, gsub_col, qtd, qtm, kv, lam_init, tq, tk, 0, tk_total // tk)
            if not last:
                ocd, ocm = _attention_call(lam_p, gsub_col, outs_c[N_KV_ARRAYS], outs_c[N_KV_ARRAYS + 1], kv,
                                           lam_init, tc, tc, t // tc, 1, True)[:2]
                hc = _odd_out(hc, m_c, g, ocd, ocm, w_out_t)
            h = _odd_out(h, m_x, g, otd, otm, w_out_t)

        wi1, wo1 = w_ffn_in[l, 1].astype(BF16), w_ffn_out[l, 1].astype(BF16)
        h = _ffn(h, m_x, g, wi1, wo1, 2)
        if not last:
            hc = _ffn(hc, m_c, g, wi1, wo1, 2)
    return h
```

```python
import functools
import math

import jax
import jax.numpy as jnp
from jax import lax
from jax.experimental import pallas as pl
from jax.experimental.pallas import tpu as pltpu

F32 = jnp.float32
BF16 = jnp.bfloat16

D_MODEL = 1024
GRID_W = 64
N_MOD = 9
FFN_RES = 0.5
EPS = 1e-6
ROPE_THETA = 10000.0

D_CONV = 512
D_FOURIER = 512
FOURIER_GROUPS = 4
D_FG = D_FOURIER // FOURIER_GROUPS

DA_HEADS = 8
DA_DK = 64
DA_DV = 128
DA_SCALE = DA_DK ** -0.5
MLA_HEADS = 8
MLA_NOPE = 64
MLA_ROPE = 32
MLA_DQK = MLA_NOPE + MLA_ROPE
MLA_DV = 64
MLA_Q_RANK = 384
MLA_KV_RANK = 256
MLA_SCALE = MLA_DQK ** -0.5
DA_QCOLS = DA_HEADS * 2 * DA_DK
DA_VCOLS = DA_HEADS * DA_DV
Q_COLS = DA_QCOLS + MLA_Q_RANK
HEAD_PAD = 128
N_MAPS = 2 * DA_HEADS + MLA_HEADS
LOG2E = 1.4426950408889634
NEG_BIG = -1e30
N_KV_ARRAYS = 4
DA_HEADS_PER_ITER = 2
MLA_HEADS_PER_ITER = 4
BOUND_SLACK = 1.02
MIN_DENOMINATOR = 2.0 ** -40

VMEM_LIMIT_V7X = 56 * 1024 * 1024


def _cparams(sem):
    return pltpu.CompilerParams(dimension_semantics=sem, vmem_limit_bytes=VMEM_LIMIT_V7X)


def _tile(n, pref):
    if n <= pref:
        return n
    t = pref - pref % 128
    while t >= 128:
        if n % t == 0:
            return t
        t -= 128
    return n


def _const_spec(shape):
    nd = len(shape)
    return pl.BlockSpec(shape, lambda *_: (0,) * nd, pipeline_mode=pl.Buffered(1))


def _mod_spec(mod):
    if mod.shape[0] == 1:
        return pl.BlockSpec((1, N_MOD, D_MODEL), lambda b, *_: (0, 0, 0))
    return pl.BlockSpec((1, N_MOD, D_MODEL), lambda b, *_: (b, 0, 0))


def _rms_rows(x, g):
    ms = jnp.mean(x * x, axis=-1, keepdims=True)
    return x * lax.rsqrt(ms + EPS) * g


def _modulated(x, mod, g, slot):
    shift = mod[3 * slot:3 * slot + 1]
    scale = mod[3 * slot + 1:3 * slot + 2]
    return _rms_rows(x, g[2 * slot:2 * slot + 1]) * (1.0 + scale) + shift


def _dot(a, b):
    return jnp.dot(a, b, preferred_element_type=F32)


def _dot_nt(a, b):
    return lax.dot_general(a, b, (((1,), (1,)), ((), ())), preferred_element_type=F32)


def _adaln_kernel(c_ref, w_ref, b_ref, o_ref):
    c = c_ref[...]
    a = c * jax.nn.sigmoid(c)
    o_ref[0] = jnp.dot(a, w_ref[0], preferred_element_type=F32,
                       precision=lax.Precision.HIGHEST) + b_ref[0]


def _adaln(cond, w_mod, b_mod):
    depth, d, n = w_mod.shape
    rows = cond.shape[0]
    tn = _tile(n, 1152)
    return pl.pallas_call(
        _adaln_kernel,
        grid=(depth, n // tn),
        in_specs=[pl.BlockSpec((rows, d), lambda l, j: (0, 0)),
                  pl.BlockSpec((1, d, tn), lambda l, j: (l, 0, j)),
                  pl.BlockSpec((1, 1, tn), lambda l, j: (l, 0, j))],
        out_specs=pl.BlockSpec((1, rows, tn), lambda l, j: (l, 0, j)),
        out_shape=jax.ShapeDtypeStruct((depth, rows, n), F32),
        compiler_params=_cparams(("parallel", "parallel")),
        name="adaln",
    )(cond, w_mod, b_mod.reshape(depth, 1, n))


def _ffn_kernel(h_ref, mod_ref, g_ref, win_ref, wout_ref, o_ref, *, slot, d_ff):
    x = h_ref[0]
    mod = mod_ref[0]
    g = g_ref[...]
    xm = _modulated(x, mod, g, slot).astype(BF16)
    gate = _dot(xm, win_ref[:, :d_ff])
    up = _dot(xm, win_ref[:, d_ff:])
    act = (gate * jax.nn.sigmoid(gate) * up).astype(BF16)
    y = _dot(act, wout_ref[...])
    res_gate = mod[3 * slot + 2:3 * slot + 3]
    o_ref[0] = x + FFN_RES * res_gate * _rms_rows(y, g[2 * slot + 1:2 * slot + 2])


def _ffn(h, mod, g, w_in, w_out, slot):
    b, t, d = h.shape
    d_ff = w_out.shape[0]
    tm = _tile(t, 512)
    return pl.pallas_call(
        functools.partial(_ffn_kernel, slot=slot, d_ff=d_ff),
        grid=(b, t // tm),
        in_specs=[pl.BlockSpec((1, tm, d), lambda bb, i: (bb, i, 0)),
                  _mod_spec(mod),
                  _const_spec(g.shape),
                  _const_spec(w_in.shape),
                  _const_spec(w_out.shape)],
        out_specs=pl.BlockSpec((1, tm, d), lambda bb, i: (bb, i, 0)),
        out_shape=jax.ShapeDtypeStruct(h.shape, F32),
        compiler_params=_cparams(("parallel", "parallel")),
        name="ffn",
    )(h, mod, g, w_in, w_out)


def _even_in_kernel(h_ref, mod_ref, g_ref, w_ref, dft_ref, gb_ref, z_ref, a_ref, *, channel_dft):
    xm = _modulated(h_ref[0], mod_ref[0], g_ref[...], 1).astype(BF16)
    u = _dot(xm, w_ref[...])
    gb_ref[0] = u[:, :D_CONV].astype(BF16)
    z_ref[0] = (u[:, D_CONV:2 * D_CONV] * u[:, 2 * D_CONV:3 * D_CONV]).astype(BF16)
    xf = u[:, 3 * D_CONV:].astype(BF16)
    if not channel_dft:
        a_ref[0] = xf
        return
    for gi in range(FOURIER_GROUPS):
        pq = _dot(xf[:, gi * D_FG:(gi + 1) * D_FG], dft_ref[...])
        a_ref[0, :, gi * D_FG:(gi + 1) * D_FG] = pq[:, :D_FG].astype(BF16)
        a_ref[1, :, gi * D_FG:(gi + 1) * D_FG] = pq[:, D_FG:].astype(BF16)


def _even_in(h, mod, g, w_in, dft_c, channel_dft):
    b, t, d = h.shape
    tm = _tile(t, 512)
    if channel_dft:
        a_spec = pl.BlockSpec((2, tm, D_FOURIER), lambda bb, i: (0, i, bb))
        a_shape = jax.ShapeDtypeStruct((2, t, b * D_FOURIER), BF16)
    else:
        a_spec = pl.BlockSpec((1, tm, D_FOURIER), lambda bb, i: (bb, i, 0))
        a_shape = jax.ShapeDtypeStruct((b, t, D_FOURIER), BF16)
    return pl.pallas_call(
        functools.partial(_even_in_kernel, channel_dft=channel_dft),
        grid=(b, t // tm),
        in_specs=[pl.BlockSpec((1, tm, d), lambda bb, i: (bb, i, 0)),
                  _mod_spec(mod),
                  _const_spec(g.shape),
                  _const_spec(w_in.shape),
                  _const_spec(dft_c.shape)],
        out_specs=[pl.BlockSpec((1, tm, D_CONV), lambda bb, i: (bb, i, 0)),
                   pl.BlockSpec((1, tm, D_CONV), lambda bb, i: (bb, i, 0)),
                   a_spec],
        out_shape=[jax.ShapeDtypeStruct((b, t, D_CONV), BF16),
                   jax.ShapeDtypeStruct((b, t, D_CONV), BF16),
                   a_shape],
        compiler_params=_cparams(("parallel", "parallel")),
        name="even_in",
    )(h, mod, g, w_in, dft_c)


DFT_COLS = 64


def _dft_rows_kernel(x_ref, f_ref, tc_ref, ts_ref, o_ref):
    res = _dot(f_ref[...], x_ref[0])
    r = f_ref.shape[0] // 2
    reps = D_FOURIER // tc_ref.shape[2]
    for c in range(tc_ref.shape[0]):
        a_re = res[:r, c * D_FOURIER:(c + 1) * D_FOURIER]
        a_im = res[r:, c * D_FOURIER:(c + 1) * D_FOURIER]
        tc = jnp.concatenate([tc_ref[c]] * reps, axis=1)
        ts = jnp.concatenate([ts_ref[c]] * reps, axis=1)
        o_ref[0, 0, c] = (a_re * tc + a_im * ts).astype(BF16)
        o_ref[0, 1, c] = (a_im * tc - a_re * ts).astype(BF16)


def _dft_cols_kernel(b_ref, m_ref, cs_ref, o_ref, *, scale):
    z = _dot(m_ref[...], jnp.concatenate([b_ref[0, 0], b_ref[0, 1]], axis=0))
    kb = z.shape[1] // D_FOURIER
    for gi in range(FOURIER_GROUPS):
        lanes = [slice(k * D_FOURIER + gi * D_FG, k * D_FOURIER + (gi + 1) * D_FG) for k in range(kb)]
        z_re = jnp.concatenate([z[:DFT_COLS, ln] for ln in lanes], axis=0)
        z_im = jnp.concatenate([z[DFT_COLS:, ln] for ln in lanes], axis=0)
        y = _dot(jnp.concatenate([z_re, z_im], axis=1).astype(BF16), cs_ref[...]) * scale
        for k in range(kb):
            o_ref[0, :, lanes[k]] = y[k * DFT_COLS:(k + 1) * DFT_COLS].astype(o_ref.dtype)


def _factored_dft(xf, tabs):
    b, t, _ = xf.shape
    r = t // DFT_COLS
    cb, kb = 8, 8
    rows = pl.pallas_call(
        _dft_rows_kernel,
        grid=(b, DFT_COLS // cb),
        in_specs=[pl.BlockSpec((1, r, cb * D_FOURIER), lambda bb, j: (bb, 0, j)),
                  _const_spec(tabs["f_rows"].shape),
                  pl.BlockSpec((cb, r, HEAD_PAD), lambda bb, j: (j, 0, 0)),
                  pl.BlockSpec((cb, r, HEAD_PAD), lambda bb, j: (j, 0, 0))],
        out_specs=pl.BlockSpec((1, 2, cb, r, D_FOURIER), lambda bb, j: (bb, 0, j, 0, 0)),
        out_shape=jax.ShapeDtypeStruct((b, 2, DFT_COLS, r, D_FOURIER), BF16),
        compiler_params=_cparams(("parallel", "parallel")),
        name="dft_rows",
    )(xf.reshape(b, r, DFT_COLS * D_FOURIER), tabs["f_rows"], tabs["tw_cos"], tabs["tw_sin"])
    out = pl.pallas_call(
        functools.partial(_dft_cols_kernel, scale=1.0 / math.sqrt(t * D_FG)),
        grid=(b, r // kb),
        in_specs=[pl.BlockSpec((1, 2, DFT_COLS, kb * D_FOURIER), lambda bb, j: (bb, 0, 0, j)),
                  _const_spec(tabs["m_cols"].shape),
                  _const_spec(tabs["cs_chan"].shape)],
        out_specs=pl.BlockSpec((1, DFT_COLS, kb * D_FOURIER), lambda bb, j: (bb, 0, j)),
        out_shape=jax.ShapeDtypeStruct((b, DFT_COLS, r * D_FOURIER), BF16),
        compiler_params=_cparams(("parallel", "parallel")),
        name="dft_cols",
    )(rows.reshape(b, 2, DFT_COLS, r * D_FOURIER), tabs["m_cols"], tabs["cs_chan"])
    return out.reshape(b, t, D_FOURIER)


def _cos_sin(num, den):
    ang = (num % den).astype(F32) * (2.0 * math.pi / den)
    return jnp.cos(ang), jnp.sin(ang)


def _factored_dft_tables(t):
    r = t // DFT_COLS
    i_r = jnp.arange(r, dtype=jnp.int32)
    i_c = jnp.arange(DFT_COLS, dtype=jnp.int32)
    i_g = jnp.arange(D_FG, dtype=jnp.int32)
    c_r, s_r = _cos_sin(i_r[:, None] * i_r[None, :], r)
    c_t, s_t = _cos_sin(i_c[:, None] * i_r[None, :], t)
    c_c, s_c = _cos_sin(i_c[:, None] * i_c[None, :], DFT_COLS)
    c_g, s_g = _cos_sin(i_g[:, None] * i_g[None, :], D_FG)
    lanes = lambda a: jnp.broadcast_to(a[:, :, None], (DFT_COLS, r, HEAD_PAD))
    return {
        "f_rows": jnp.concatenate([c_r, -s_r], axis=0).astype(BF16),
        "tw_cos": lanes(c_t), "tw_sin": lanes(s_t),
        "m_cols": jnp.block([[c_c, s_c], [-s_c, c_c]]).astype(BF16),
        "cs_chan": jnp.concatenate([c_g, s_g], axis=0).astype(BF16),
    }


def _matmul_kernel(a_ref, b_ref, o_ref, acc_ref, *, scale):
    k = pl.program_id(2)

    @pl.when(k == 0)
    def _():
        acc_ref[...] = jnp.zeros_like(acc_ref)

    acc_ref[...] += _dot(a_ref[...], b_ref[...])

    @pl.when(k == pl.num_programs(2) - 1)
    def _():
        o_ref[...] = (acc_ref[...] * scale).astype(o_ref.dtype)


def _matmul(a, b, scale, out_dtype):
    m, kk = a.shape
    n = b.shape[1]
    bm, bn, bk = _tile(m, 1024), _tile(n, 1024), _tile(kk, 2048)
    return pl.pallas_call(
        functools.partial(_matmul_kernel, scale=scale),
        grid=(m // bm, n // bn, kk // bk),
        in_specs=[pl.BlockSpec((bm, bk), lambda i, j, k: (i, k)),
                  pl.BlockSpec((bk, bn), lambda i, j, k: (k, j))],
        out_specs=pl.BlockSpec((bm, bn), lambda i, j, k: (i, j)),
        out_shape=jax.ShapeDtypeStruct((m, n), out_dtype),
        scratch_shapes=[pltpu.VMEM((bm, bn), F32)],
        compiler_params=_cparams(("parallel", "parallel", "arbitrary")),
        name="dft_matmul",
    )(a, b)


def _even_out_kernel(h_ref, mod_ref, g_ref, gb_ref, z_ref, zp_ref, zn_ref, yf_ref, cw_ref, w_ref, o_ref):
    i = pl.program_id(1)
    x = h_ref[0]
    mod = mod_ref[0]
    g = g_ref[...]
    z = z_ref[0].astype(F32)
    tm = z.shape[0]
    halo = zp_ref.shape[1]
    prev_row = jnp.where(i > 0, zp_ref[0, halo - 1:halo, :].astype(F32), 0.0)
    next_row = jnp.where(i < pl.num_programs(1) - 1, zn_ref[0, 0:1, :].astype(F32), 0.0)
    row = lax.broadcasted_iota(jnp.int32, z.shape, 0)
    z_before = jnp.where(row == 0, prev_row, pltpu.roll(z, 1, 0))
    z_after = jnp.where(row == tm - 1, next_row, pltpu.roll(z, tm - 1, 0))
    cw = cw_ref[...]
    conv = z_before * cw[0:1] + z * cw[1:2] + z_after * cw[2:3]
    y_conv = (gb_ref[0].astype(F32) * conv).astype(BF16)
    y = _dot(y_conv, w_ref[:D_CONV, :]) + _dot(yf_ref[0], w_ref[D_CONV:, :])
    o_ref[0] = x + mod[5:6] * _rms_rows(y, g[3:4])


def _even_out(h, mod, g, gb, z, yf, conv_w, w_out):
    b, t, d = h.shape
    tm = _tile(t, 512)
    halo = 16
    nh = tm // halo
    last_halo = t // halo - 1
    return pl.pallas_call(
        _even_out_kernel,
        grid=(b, t // tm),
        in_specs=[pl.BlockSpec((1, tm, d), lambda bb, i: (bb, i, 0)),
                  _mod_spec(mod),
                  _const_spec(g.shape),
                  pl.BlockSpec((1, tm, D_CONV), lambda bb, i: (bb, i, 0)),
                  pl.BlockSpec((1, tm, D_CONV), lambda bb, i: (bb, i, 0)),
                  pl.BlockSpec((1, halo, D_CONV), lambda bb, i: (bb, jnp.maximum(i * nh - 1, 0), 0)),
                  pl.BlockSpec((1, halo, D_CONV), lambda bb, i: (bb, jnp.minimum((i + 1) * nh, last_halo), 0)),
                  pl.BlockSpec((1, tm, D_FOURIER), lambda bb, i: (bb, i, 0)),
                  _const_spec(conv_w.shape),
                  _const_spec(w_out.shape)],
        out_specs=pl.BlockSpec((1, tm, d), lambda bb, i: (bb, i, 0)),
        out_shape=jax.ShapeDtypeStruct(h.shape, F32),
        compiler_params=_cparams(("parallel", "parallel")),
        name="even_out",
    )(h, mod, g, gb, z, z, z, yf, conv_w, w_out)


def _dft_tables(t):
    n = jnp.arange(t, dtype=jnp.int32)
    ang = ((n[:, None] * n[None, :]) % t).astype(F32) * (2.0 * math.pi / t)
    w_pos = jnp.concatenate([jnp.cos(ang), -jnp.sin(ang)], axis=1).astype(BF16)
    return w_pos


def _channel_dft_table():
    n = jnp.arange(D_FG, dtype=jnp.int32)
    ang = ((n[:, None] * n[None, :]) % D_FG).astype(F32) * (2.0 * math.pi / D_FG)
    return jnp.concatenate([jnp.cos(ang), jnp.sin(ang)], axis=1).astype(BF16)


def _even_mixer(h, mod, g, w_in, conv_w, w_out, dft_c, dft_pos):
    b, t, _ = h.shape
    if isinstance(dft_pos, dict):
        gb, z, xf = _even_in(h, mod, g, w_in, dft_c, False)
        yf = _factored_dft(xf, dft_pos)
    else:
        gb, z, a = _even_in(h, mod, g, w_in, dft_c, True)
        yf = _matmul(dft_pos, a.reshape(2 * t, b * D_FOURIER), 1.0 / math.sqrt(t * D_FG), BF16)
        yf = yf.reshape(t, b, D_FOURIER).transpose(1, 0, 2)
    return _even_out(h, mod, g, gb, z, yf, conv_w, w_out)


def _odd_in_kernel(h_ref, mod_ref, g_ref, wtok_ref, wt_ref, wukp_ref, wuvt_ref, wuqt_ref, place_ref, gsel_ref,
                   gkv_row_ref, gkv_col_ref, gq_col_ref,
                   cosk_ref, sink_ref, tkr_ref, cosq_ref, sinq_ref, cosm_ref, sinm_ref,
                   kh_ref, vtd_ref, vtm_ref, kn_ref, *q_refs, with_q):
    xm = _modulated(h_ref[0], mod_ref[0], g_ref[...], 1).astype(BF16)
    tm = xm.shape[0]

    ut = _dot(xm, wtok_ref[...])
    k = ut[:, :DA_QCOLS]
    ckv = ut[:, DA_QCOLS:DA_QCOLS + MLA_KV_RANK]
    kr2 = ut[:, DA_QCOLS + MLA_KV_RANK:]
    lane = lax.broadcasted_iota(jnp.int32, k.shape, 1)
    first_half = (lane % (DA_DK // 2)) < (DA_DK // 4)
    k_sw = jnp.where(first_half, pltpu.roll(k, DA_QCOLS - DA_DK // 4, 1), pltpu.roll(k, DA_DK // 4, 1))
    reps = DA_QCOLS // HEAD_PAD
    cosk = jnp.concatenate([cosk_ref[...]] * reps, axis=1)
    sink = jnp.concatenate([sink_ref[...]] * reps, axis=1)
    k_rot = k * cosk + k_sw * sink
    ckvn = _rms_rows(ckv, gkv_row_ref[...]).astype(BF16)
    k_nope = _dot(ckvn, wukp_ref[...])
    pr = kr2 * tkr_ref[...]
    pr_hi = pr.astype(BF16)
    pr_lo = (pr - pr_hi.astype(F32)).astype(BF16)
    k_mla = k_nope + _dot(pr_hi, place_ref[...]) + _dot(pr_lo, place_ref[...])
    k_odd = pltpu.roll(k_rot, DA_QCOLS - DA_DK, 1)
    lane_h = lax.broadcasted_iota(jnp.int32, (tm, HEAD_PAD), 1)
    sq = []
    for mp in range(N_MAPS):
        if mp < 2 * DA_HEADS:
            src = (k_rot if mp % 2 == 0 else k_odd)[:, (mp // 2) * HEAD_PAD:(mp // 2 + 1) * HEAD_PAD]
            kd = jnp.where(lane_h < DA_DK, src, 0.0)
            one_lane = DA_DK
        else:
            hh = mp - 2 * DA_HEADS
            kd = k_mla[:, hh * HEAD_PAD:(hh + 1) * HEAD_PAD]
            one_lane = MLA_DQK
        kh_ref[0, mp] = jnp.where(lane_h == one_lane, 1.0, kd).astype(BF16)
        kf = kd.astype(BF16).astype(F32)
        sq.append((kf * kf).astype(BF16))
    kn_ref[0] = _dot_nt(gsel_ref[...], jnp.concatenate(sq, axis=1))

    r0 = DA_VCOLS
    vckv = _dot_nt(wt_ref[:r0 + MLA_KV_RANK, :], xm)
    for hh in range(DA_HEADS):
        vtd_ref[0, hh] = vckv[hh * DA_DV:(hh + 1) * DA_DV].astype(BF16)
    ckv_t = vckv[r0:]
    ms = jnp.mean(ckv_t * ckv_t, axis=0, keepdims=True)
    ckvn_t = (ckv_t * lax.rsqrt(ms + EPS) * gkv_col_ref[...]).astype(BF16)
    vm_t = _dot(wuvt_ref[...], ckvn_t)
    for hh in range(MLA_HEADS):
        vtm_ref[0, hh] = vm_t[hh * MLA_DV:(hh + 1) * MLA_DV].astype(BF16)

    if with_q:
        qtd_ref, qtm_ref = q_refs
        r1 = r0 + MLA_KV_RANK
        q_t = _dot_nt(wt_ref[r1:, :], xm)
        cosq, sinq = cosq_ref[...], sinq_ref[...]
        qd = DA_DK // 4
        for mp in range(2 * DA_HEADS):
            q = q_t[mp * DA_DK:(mp + 1) * DA_DK]
            q_sw = jnp.concatenate([q[qd:2 * qd], q[:qd], q[3 * qd:], q[2 * qd:3 * qd]], axis=0)
            q_rot = (q * cosq + q_sw * sinq) * (DA_SCALE * LOG2E)
            qtd_ref[0, mp // 2, (mp % 2) * DA_DK:(mp % 2 + 1) * DA_DK, :] = q_rot.astype(BF16)
        cq_t = q_t[DA_QCOLS:]
        ms = jnp.mean(cq_t * cq_t, axis=0, keepdims=True)
        cqn_t = (cq_t * lax.rsqrt(ms + EPS) * gq_col_ref[...]).astype(BF16)
        qm_t = _dot(wuqt_ref[...], cqn_t) * (MLA_SCALE * LOG2E)
        cosm, sinm = cosm_ref[...], sinm_ref[...]
        rd = MLA_ROPE // 4
        for hh in range(MLA_HEADS):
            base = hh * HEAD_PAD
            qtm_ref[0, hh, :MLA_NOPE, :] = qm_t[base:base + MLA_NOPE].astype(BF16)
            r = qm_t[base + MLA_NOPE:base + MLA_DQK]
            r_sw = jnp.concatenate([r[rd:2 * rd], r[:rd], r[3 * rd:], r[2 * rd:3 * rd]], axis=0)
            qtm_ref[0, hh, MLA_NOPE:MLA_DQK, :] = (r * cosm + r_sw * sinm).astype(BF16)
            qtm_ref[0, hh, MLA_DQK:, :] = jnp.zeros((HEAD_PAD - MLA_DQK, tm), BF16)


def _odd_in(h, mod, g, wts, tabs, kv_prev, tok_off, with_q):
    b, t, d = h.shape
    tk_total = tabs["tk_total"]
    tm = _tile(t, 256)
    assert tok_off % tm == 0
    off = tok_off // tm
    row_tab = lambda w: pl.BlockSpec((tm, w), lambda bb, i: (i, 0))
    col_tab = lambda r: pl.BlockSpec((r, tm), lambda bb, i: (0, i))
    consts = [wts["wtok"], wts["wt"], wts["wukp"], wts["wuvt"], wts["wuqt"], wts["place"], wts["gsel"],
              wts["gkv_row"], wts["gkv_col"], wts["gq_col"]]
    in_specs = ([pl.BlockSpec((1, tm, d), lambda bb, i: (bb, i, 0)), _mod_spec(mod), _const_spec(g.shape)]
                + [_const_spec(c.shape) for c in consts]
                + [row_tab(HEAD_PAD), row_tab(HEAD_PAD), row_tab(2 * MLA_ROPE),
                   col_tab(DA_DK), col_tab(DA_DK), col_tab(MLA_ROPE), col_tab(MLA_ROPE)])
    args = [h, mod, g] + consts + [tabs["cosk"], tabs["sink"], tabs["tkr"],
                                   tabs["cosq"], tabs["sinq"], tabs["cosm"], tabs["sinm"]]
    out_specs = [pl.BlockSpec((1, N_MAPS, tm, HEAD_PAD), lambda bb, i: (bb, 0, i + off, 0)),
                 pl.BlockSpec((1, DA_HEADS, DA_DV, tm), lambda bb, i: (bb, 0, 0, i + off)),
                 pl.BlockSpec((1, MLA_HEADS, MLA_DV, tm), lambda bb, i: (bb, 0, 0, i + off)),
                 pl.BlockSpec((1, N_MAPS, tm), lambda bb, i: (bb, 0, i + off))]
    out_shape = [jax.ShapeDtypeStruct((b, N_MAPS, tk_total, HEAD_PAD), BF16),
                 jax.ShapeDtypeStruct((b, DA_HEADS, DA_DV, tk_total), BF16),
                 jax.ShapeDtypeStruct((b, MLA_HEADS, MLA_DV, tk_total), BF16),
                 jax.ShapeDtypeStruct((b, N_MAPS, tk_total), F32)]
    if with_q:
        out_specs += [pl.BlockSpec((1, DA_HEADS, HEAD_PAD, tm), lambda bb, i: (bb, 0, 0, i)),
                      pl.BlockSpec((1, MLA_HEADS, HEAD_PAD, tm), lambda bb, i: (bb, 0, 0, i))]
        out_shape += [jax.ShapeDtypeStruct((b, DA_HEADS, HEAD_PAD, t), BF16),
                      jax.ShapeDtypeStruct((b, MLA_HEADS, HEAD_PAD, t), BF16)]
    aliases = {}
    kernel_fn = functools.partial(_odd_in_kernel, with_q=with_q)
    if kv_prev is not None:
        n_in = len(args)
        in_specs += [pl.BlockSpec(memory_space=pl.ANY)] * N_KV_ARRAYS
        args += list(kv_prev)
        aliases = {n_in + a: a for a in range(N_KV_ARRAYS)}
        kernel_fn = functools.partial(_odd_in_alias_kernel, n_in=n_in, with_q=with_q)
    return pl.pallas_call(
        kernel_fn,
        grid=(b, t // tm),
        in_specs=in_specs,
        out_specs=out_specs,
        out_shape=out_shape,
        input_output_aliases=aliases,
        compiler_params=_cparams(("parallel", "parallel")),
        name="odd_in",
    )(*args)


def _odd_in_alias_kernel(*refs, n_in, with_q):
    _odd_in_kernel(*refs[:n_in], *refs[n_in + N_KV_ARRAYS:], with_q=with_q)


def _attn_kernel(lam_ref, gsub_ref, kn_ref, qtd_ref, qtm_ref, kh_ref, vtd_ref, vtm_ref,
                 otd_ref, otm_ref, lmin_ref, qa_ref, m_ref, l_ref, accd_ref, accm_ref, *, lam_init, safe):
    j = pl.program_id(2)
    tq = qtd_ref.shape[-1]

    @pl.when(j == 0)
    def _():
        m_ref[...] = jnp.full_like(m_ref, NEG_BIG)
        l_ref[...] = jnp.zeros_like(l_ref)
        accd_ref[...] = jnp.zeros_like(accd_ref)
        accm_ref[...] = jnp.zeros_like(accm_ref)
        kmax = jnp.sqrt(jnp.max(kn_ref[0], axis=1, keepdims=True))
        shift_row = lax.broadcasted_iota(jnp.int32, (16, tq), 0) == 0
        for idx in range(N_MAPS):
            if idx < 2 * DA_HEADS:
                q = qtd_ref[0, idx // 2, (idx % 2) * DA_DK:(idx % 2 + 1) * DA_DK, :]
            else:
                q = qtm_ref[0, idx - 2 * DA_HEADS, :MLA_DQK, :]
            if safe:
                shift_blk = jnp.zeros((16, tq), BF16)
            else:
                qf = q.astype(F32)
                qn = jnp.sqrt(jnp.sum(qf * qf, axis=0, keepdims=True))
                shift = -(BOUND_SLACK * kmax[idx:idx + 1]) * qn
                shift_blk = jnp.where(shift_row, shift, 0.0).astype(BF16)
            pad = jnp.zeros((HEAD_PAD - q.shape[0] - 16, tq), BF16)
            qa_ref[idx] = jnp.concatenate([q, shift_blk, pad], axis=0)

    def one_map(idx, v_t, acc_ref, acc_idx):
        s = _dot(kh_ref[0, idx], qa_ref[idx])
        if safe:
            m_old = m_ref[idx]
            m_new = jnp.maximum(m_old, jnp.max(s, axis=0, keepdims=True))
            alpha = jnp.exp2(m_old - m_new)
            m_ref[idx] = m_new
            p = jnp.exp2(s - m_new)
            l_ref[idx] = alpha * l_ref[idx] + jnp.sum(p.reshape(-1, 8, tq), axis=0)
            acc_ref[acc_idx] = acc_ref[acc_idx] * alpha + _dot(v_t, p.astype(BF16))
        else:
            p = jnp.exp2(s)
            l_ref[idx] += jnp.sum(p.reshape(-1, 8, tq), axis=0)
            acc_ref[acc_idx] += _dot(v_t, p.astype(BF16))

    def da_head(hh, carry):
        v_t = vtd_ref[0, hh]
        one_map(2 * hh, v_t, accd_ref, 2 * hh)
        one_map(2 * hh + 1, v_t, accd_ref, 2 * hh + 1)
        return carry

    def mla_head(hh, carry):
        one_map(2 * DA_HEADS + hh, vtm_ref[0, hh], accm_ref, hh)
        return carry

    lax.fori_loop(0, DA_HEADS // DA_HEADS_PER_ITER,
                  lambda it, c: [da_head(it * DA_HEADS_PER_ITER + u, c) for u in range(DA_HEADS_PER_ITER)][-1], 0)
    lax.fori_loop(0, MLA_HEADS // MLA_HEADS_PER_ITER,
                  lambda it, c: [mla_head(it * MLA_HEADS_PER_ITER + u, c) for u in range(MLA_HEADS_PER_ITER)][-1], 0)

    @pl.when(j == pl.num_programs(2) - 1)
    def _():
        lp = lam_ref[...]
        lam = (jnp.exp(jnp.sum(lp[0:1] * lp[1:2], axis=1, keepdims=True))
               - jnp.exp(jnp.sum(lp[2:3] * lp[3:4], axis=1, keepdims=True)) + lam_init)
        gsub = gsub_ref[...]
        lsum = [jnp.sum(l_ref[idx], axis=0, keepdims=True) for idx in range(N_MAPS)]
        for hh in range(DA_HEADS):
            o = accd_ref[2 * hh] / lsum[2 * hh] - lam * (accd_ref[2 * hh + 1] / lsum[2 * hh + 1])
            ms = jnp.mean(o * o, axis=0, keepdims=True)
            otd_ref[0, hh] = (o * lax.rsqrt(ms + EPS) * gsub * (1.0 - lam_init)).astype(otd_ref.dtype)
        for hh in range(MLA_HEADS):
            otm_ref[0, hh] = (accm_ref[hh] / lsum[2 * DA_HEADS + hh]).astype(otm_ref.dtype)
        lmin = lsum[0]
        for idx in range(1, N_MAPS):
            lmin = jnp.minimum(lmin, lsum[idx])
        lmin_ref[0, 0] = jnp.broadcast_to(jnp.min(lmin, axis=1, keepdims=True), lmin_ref.shape[2:])


def _attention_call(lam_p, gsub_col, qtd, qtm, kv, lam_init, tq, tk, kv_off, n_kv, safe):
    kh, vtd, vtm, kn = kv
    b, _, _, t = qtd.shape
    nq = t // tq
    assert kv_off % n_kv == 0
    return pl.pallas_call(
        functools.partial(_attn_kernel, lam_init=lam_init, safe=safe),
        grid=(b, nq, n_kv),
        in_specs=[_const_spec(lam_p.shape),
                  _const_spec(gsub_col.shape),
                  pl.BlockSpec((1, N_MAPS, n_kv * tk), lambda bb, i, j: (bb, 0, kv_off // n_kv)),
                  pl.BlockSpec((1, DA_HEADS, HEAD_PAD, tq), lambda bb, i, j: (bb, 0, 0, i)),
                  pl.BlockSpec((1, MLA_HEADS, HEAD_PAD, tq), lambda bb, i, j: (bb, 0, 0, i)),
                  pl.BlockSpec((1, N_MAPS, tk, HEAD_PAD), lambda bb, i, j: (bb, 0, kv_off + j, 0)),
                  pl.BlockSpec((1, DA_HEADS, DA_DV, tk), lambda bb, i, j: (bb, 0, 0, kv_off + j)),
                  pl.BlockSpec((1, MLA_HEADS, MLA_DV, tk), lambda bb, i, j: (bb, 0, 0, kv_off + j))],
        out_specs=[pl.BlockSpec((1, DA_HEADS, DA_DV, tq), lambda bb, i, j: (bb, 0, 0, i)),
                   pl.BlockSpec((1, MLA_HEADS, MLA_DV, tq), lambda bb, i, j: (bb, 0, 0, i)),
                   pl.BlockSpec((1, 1, 8, 128), lambda bb, i, j: (bb, i, 0, 0))],
        out_shape=[jax.ShapeDtypeStruct((b, DA_HEADS, DA_DV, t), BF16),
                   jax.ShapeDtypeStruct((b, MLA_HEADS, MLA_DV, t), BF16),
                   jax.ShapeDtypeStruct((b, nq, 8, 128), F32)],
        scratch_shapes=[pltpu.VMEM((N_MAPS, HEAD_PAD, tq), BF16),
                        pltpu.VMEM((N_MAPS, 1, tq), F32),
                        pltpu.VMEM((N_MAPS, 8, tq), F32),
                        pltpu.VMEM((2 * DA_HEADS, DA_DV, tq), F32),
                        pltpu.VMEM((MLA_HEADS, MLA_DV, tq), F32)],
        compiler_params=_cparams(("parallel", "parallel", "arbitrary")),
        name="attention_safe" if safe else "attention",
    )(lam_p, gsub_col, kn, qtd, qtm, kh, vtd, vtm)


def _attention(lam_p, gsub_col, qtd, qtm, kv, lam_init, tq, tk, kv_off, n_kv):
    args = (lam_p, gsub_col, qtd, qtm, kv)
    otd, otm, lmin = _attention_call(*args, lam_init, tq, tk, kv_off, n_kv, False)
    ok = jnp.min(lmin) >= MIN_DENOMINATOR
    return lax.cond(ok, lambda *_: (otd, otm),
                    lambda *a: tuple(_attention_call(*a, lam_init, tq, tk, kv_off, n_kv, True)[:2]), *args)


def _odd_out_kernel(h_ref, mod_ref, g_ref, otd_ref, otm_ref, wt_ref, o_ref):
    y_t = _dot(wt_ref[:, :DA_VCOLS], otd_ref[0]) + _dot(wt_ref[:, DA_VCOLS:], otm_ref[0])
    y = y_t.T
    o_ref[0] = h_ref[0] + mod_ref[0][5:6] * _rms_rows(y, g_ref[3:4])


def _odd_out(h, mod, g, otd, otm, w_out_t):
    b, t, d = h.shape
    tm = _tile(t, 512)
    otd = otd.reshape(b, DA_VCOLS, t)
    otm = otm.reshape(b, MLA_HEADS * MLA_DV, t)
    return pl.pallas_call(
        _odd_out_kernel,
        grid=(b, t // tm),
        in_specs=[pl.BlockSpec((1, tm, d), lambda bb, i: (bb, i, 0)),
                  _mod_spec(mod),
                  _const_spec(g.shape),
                  pl.BlockSpec((1, DA_VCOLS, tm), lambda bb, i: (bb, 0, i)),
                  pl.BlockSpec((1, MLA_HEADS * MLA_DV, tm), lambda bb, i: (bb, 0, i)),
                  _const_spec(w_out_t.shape)],
        out_specs=pl.BlockSpec((1, tm, d), lambda bb, i: (bb, i, 0)),
        out_shape=jax.ShapeDtypeStruct(h.shape, F32),
        compiler_params=_cparams(("parallel", "parallel")),
        name="odd_out",
    )(h, mod, g, otd, otm, w_out_t)


def _rope_angles(n_tokens, rot_dim):
    rows = n_tokens // GRID_W
    row = jnp.broadcast_to(jnp.arange(rows)[:, None], (rows, GRID_W)).reshape(-1).astype(F32)
    col = jnp.broadcast_to(jnp.arange(GRID_W)[None, :], (rows, GRID_W)).reshape(-1).astype(F32)
    n_freq = rot_dim // 4
    freqs = ROPE_THETA ** (-jnp.arange(n_freq, dtype=F32) / n_freq)
    return row[:, None] * freqs, col[:, None] * freqs


def _rope_cos_sin(n_tokens, rot_dim, identity):
    if identity:
        return jnp.ones((n_tokens, rot_dim), F32), jnp.zeros((n_tokens, rot_dim), F32)
    ar, ac = _rope_angles(n_tokens, rot_dim)
    cos = jnp.concatenate([jnp.cos(ar), jnp.cos(ar), jnp.cos(ac), jnp.cos(ac)], axis=1)
    sin = jnp.concatenate([-jnp.sin(ar), jnp.sin(ar), -jnp.sin(ac), jnp.sin(ac)], axis=1)
    return cos, sin


def _rope_tables(n_tokens, identity, tk_total):
    cd, sd = _rope_cos_sin(n_tokens, DA_DK, identity)
    cm, sm = _rope_cos_sin(n_tokens, MLA_ROPE, identity)
    return {
        "cosk": jnp.concatenate([cd, cd], axis=1), "sink": jnp.concatenate([sd, sd], axis=1),
        "tkr": jnp.concatenate([cm, sm], axis=1),
        "cosq": cd.T, "sinq": sd.T, "cosm": cm.T, "sinm": sm.T,
        "tk_total": tk_total,
    }


def _swap_perm(rot_dim):
    q = rot_dim // 4
    return jnp.concatenate([jnp.arange(q, 2 * q), jnp.arange(0, q), jnp.arange(3 * q, 4 * q), jnp.arange(2 * q, 3 * q)])


def _odd_weights(w_in, g_q, w_uq, g_kv, w_uk, w_uv):
    w_q = w_in[:, :DA_QCOLS]
    w_cq = w_in[:, DA_QCOLS:Q_COLS]
    w_k = w_in[:, Q_COLS:Q_COLS + DA_QCOLS]
    w_v = w_in[:, Q_COLS + DA_QCOLS:Q_COLS + DA_QCOLS + DA_VCOLS]
    w_ckv = w_in[:, Q_COLS + DA_QCOLS + DA_VCOLS:Q_COLS + DA_QCOLS + DA_VCOLS + MLA_KV_RANK]
    w_kr = w_in[:, Q_COLS + DA_QCOLS + DA_VCOLS + MLA_KV_RANK:]
    wtok = jnp.concatenate([w_k, w_ckv, w_kr, w_kr[:, _swap_perm(MLA_ROPE)]], axis=1).astype(BF16)
    wt = jnp.concatenate([w_v, w_ckv, w_q, w_cq], axis=1).T.astype(BF16)
    pad_k = jnp.zeros((MLA_KV_RANK, MLA_HEADS, HEAD_PAD), F32)
    wukp = pad_k.at[:, :, :MLA_NOPE].set(w_uk.reshape(MLA_KV_RANK, MLA_HEADS, MLA_NOPE))
    wukp = wukp.reshape(MLA_KV_RANK, MLA_HEADS * HEAD_PAD).astype(BF16)
    pad_q = jnp.zeros((MLA_Q_RANK, MLA_HEADS, HEAD_PAD), F32)
    wuqp = pad_q.at[:, :, :MLA_DQK].set(w_uq.reshape(MLA_Q_RANK, MLA_HEADS, MLA_DQK))
    wuqt = wuqp.reshape(MLA_Q_RANK, MLA_HEADS * HEAD_PAD).T.astype(BF16)
    eye = jnp.eye(MLA_ROPE, dtype=F32)
    place = jnp.zeros((2, MLA_ROPE, MLA_HEADS, HEAD_PAD), F32)
    place = place.at[:, :, :, MLA_NOPE:MLA_DQK].set(jnp.broadcast_to(eye[None, :, None, :], (2, MLA_ROPE, MLA_HEADS, MLA_ROPE)))
    place = place.reshape(2 * MLA_ROPE, MLA_HEADS * HEAD_PAD).astype(BF16)
    return {
        "wtok": wtok, "wt": wt, "wukp": wukp, "wuvt": w_uv.T.astype(BF16), "wuqt": wuqt, "place": place,
        "gsel": jnp.kron(jnp.eye(N_MAPS, dtype=F32), jnp.ones((1, HEAD_PAD), F32)).astype(BF16),
        "gkv_row": g_kv.reshape(1, -1).astype(F32), "gkv_col": g_kv.reshape(-1, 1).astype(F32),
        "gq_col": g_q.reshape(-1, 1).astype(F32),
    }


def kernel(x, c, ctx, c_ctx, w_mod, b_mod, norm_g, w_ffn_in, w_ffn_out, w_in_even, conv_w, w_out_even,
           w_in_odd, g_q_mla, w_uq, g_kv_mla, w_uk, w_uv, lam_q1, lam_k1, lam_q2, lam_k2, g_subln, w_out_odd):
    b, t, d = x.shape
    tc = ctx.shape[1]
    depth = w_mod.shape[0]
    tk_total = t + tc

    rows = -(-(b + 1) // 8) * 8
    cond = jnp.zeros((rows, d), F32).at[:b].set(c).at[b].set(c_ctx)
    mod = _adaln(cond, w_mod, b_mod).reshape(depth, rows, N_MOD, d)

    dft_c = _channel_dft_table()
    factored = lambda n: n % (DFT_COLS * 16) == 0
    w_pos_x = _factored_dft_tables(t) if factored(t) else _dft_tables(t)
    w_pos_c = _factored_dft_tables(tc) if factored(tc) else _dft_tables(tc)
    tabs_x = _rope_tables(t, False, tk_total)
    tabs_c = _rope_tables(tc, True, tk_total)
    tq = _tile(t, 1024)
    tk = 768 if tk_total % 768 == 0 else _tile(tk_total, 512)
    assert t % tc == 0 and tk_total % tk == 0

    h, hc = x, ctx
    for l in range(depth):
        last = l == depth - 1
        odd = l % 2 == 1
        ctx_live = (not last) or odd
        g = norm_g[l]
        m_x, m_c = mod[l, :b], mod[l, b:b + 1]
        wi0, wo0 = w_ffn_in[l, 0].astype(BF16), w_ffn_out[l, 0].astype(BF16)
        h = _ffn(h, m_x, g, wi0, wo0, 0)
        if ctx_live:
            hc = _ffn(hc, m_c, g, wi0, wo0, 0)

        if not odd:
            e = l // 2
            w_in, w_out = w_in_even[e].astype(BF16), w_out_even[e].astype(BF16)
            h = _even_mixer(h, m_x, g, w_in, conv_w[e], w_out, dft_c, w_pos_x)
            if ctx_live:
                hc = _even_mixer(hc, m_c, g, w_in, conv_w[e], w_out, dft_c, w_pos_c)
        else:
            o = l // 2
            lam_init = 0.8 - 0.6 * math.exp(-0.3 * l)
            wts = _odd_weights(w_in_odd[o], g_q_mla[o], w_uq[o], g_kv_mla[o], w_uk[o], w_uv[o])
            lam_p = jnp.stack([lam_q1[o], lam_k1[o], lam_q2[o], lam_k2[o]]).astype(F32)
            gsub_col = g_subln[o].reshape(-1, 1).astype(F32)
            w_out_t = w_out_odd[o].T.astype(BF16)
            outs_x = _odd_in(h, m_x, g, wts, tabs_x, None, 0, True)
            qtd, qtm = outs_x[N_KV_ARRAYS:]
            outs_c = _odd_in(hc, m_c, g, wts, tabs_c, outs_x[:N_KV_ARRAYS], t, not last)
            kv = outs_c[:N_KV_ARRAYS]
            otd, otm = _attention(lam_p, gsub_col, qtd, qtm, kv, lam_init, tq, tk, 0, tk_total // tk)
            if not last:
                ocd, ocm = _attention_call(lam_p, gsub_col, outs_c[N_KV_ARRAYS], outs_c[N_KV_ARRAYS + 1], kv,
                                           lam_init, tc, tc, t // tc, 1, True)[:2]
                hc = _odd_out(hc, m_c, g, ocd, ocm, w_out_t)
            h = _odd_out(h, m_x, g, otd, otm, w_out_t)

        wi1, wo1 = w_ffn_in[l, 1].astype(BF16), w_ffn_out[l, 1].astype(BF16)
        h = _ffn(h, m_x, g, wi1, wo1, 2)
        if not last:
            hc = _ffn(hc, m_c, g, wi1, wo1, 2)
    return h
```

```python
import functools
import math

import jax
import jax.numpy as jnp
from jax import lax
from jax.experimental import pallas as pl
from jax.experimental.pallas import tpu as pltpu

F32 = jnp.float32
BF16 = jnp.bfloat16

D_MODEL = 1024
GRID_W = 64
N_MOD = 9
FFN_RES = 0.5
EPS = 1e-6
ROPE_THETA = 10000.0

D_CONV = 512
D_FOURIER = 512
FOURIER_GROUPS = 4
D_FG = D_FOURIER // FOURIER_GROUPS

DA_HEADS = 8
DA_DK = 64
DA_DV = 128
DA_SCALE = DA_DK ** -0.5
MLA_HEADS = 8
MLA_NOPE = 64
MLA_ROPE = 32
MLA_DQK = MLA_NOPE + MLA_ROPE
MLA_DV = 64
MLA_Q_RANK = 384
MLA_KV_RANK = 256
MLA_SCALE = MLA_DQK ** -0.5
DA_QCOLS = DA_HEADS * 2 * DA_DK
DA_VCOLS = DA_HEADS * DA_DV
Q_COLS = DA_QCOLS + MLA_Q_RANK
HEAD_PAD = 128
N_MAPS = 2 * DA_HEADS + MLA_HEADS
LOG2E = 1.4426950408889634
NEG_BIG = -1e30
N_KV_ARRAYS = 4
FFN_ROW_SPLITS = 2
DA_HEADS_PER_ITER = 2
MLA_HEADS_PER_ITER = 4
BOUND_SLACK = 1.02
MIN_DENOMINATOR = 2.0 ** -40

VMEM_LIMIT_V7X = 56 * 1024 * 1024


def _cparams(sem):
    return pltpu.CompilerParams(dimension_semantics=sem, vmem_limit_bytes=VMEM_LIMIT_V7X)


def _tile(n, pref):
    if n <= pref:
        return n
    t = pref - pref % 128
    while t >= 128:
        if n % t == 0:
            return t
        t -= 128
    return n


def _const_spec(shape):
    nd = len(shape)
    return pl.BlockSpec(shape, lambda *_: (0,) * nd, pipeline_mode=pl.Buffered(1))


def _mod_spec(mod):
    if mod.shape[0] == 1:
        return pl.BlockSpec((1, N_MOD, D_MODEL), lambda b, *_: (0, 0, 0))
    return pl.BlockSpec((1, N_MOD, D_MODEL), lambda b, *_: (b, 0, 0))


def _rms_rows(x, g):
    ms = jnp.mean(x * x, axis=-1, keepdims=True)
    return x * lax.rsqrt(ms + EPS) * g


def _modulated(x, mod, g, slot):
    shift = mod[3 * slot:3 * slot + 1]
    scale = mod[3 * slot + 1:3 * slot + 2]
    return _rms_rows(x, g[2 * slot:2 * slot + 1]) * (1.0 + scale) + shift


def _dot(a, b):
    return jnp.dot(a, b, preferred_element_type=F32)


def _dot_nt(a, b):
    return lax.dot_general(a, b, (((1,), (1,)), ((), ())), preferred_element_type=F32)


def _adaln_kernel(c_ref, w_ref, b_ref, o_ref):
    c = c_ref[...]
    a = c * jax.nn.sigmoid(c)
    o_ref[0] = jnp.dot(a, w_ref[0], preferred_element_type=F32,
                       precision=lax.Precision.HIGHEST) + b_ref[0]


def _adaln(cond, w_mod, b_mod):
    depth, d, n = w_mod.shape
    rows = cond.shape[0]
    tn = _tile(n, 1152)
    return pl.pallas_call(
        _adaln_kernel,
        grid=(depth, n // tn),
        in_specs=[pl.BlockSpec((rows, d), lambda l, j: (0, 0)),
                  pl.BlockSpec((1, d, tn), lambda l, j: (l, 0, j)),
                  pl.BlockSpec((1, 1, tn), lambda l, j: (l, 0, j))],
        out_specs=pl.BlockSpec((1, rows, tn), lambda l, j: (l, 0, j)),
        out_shape=jax.ShapeDtypeStruct((depth, rows, n), F32),
        compiler_params=_cparams(("parallel", "parallel")),
        name="adaln",
    )(cond, w_mod, b_mod.reshape(depth, 1, n))


def _ffn_kernel(h_ref, mod_ref, g_ref, win_ref, wout_ref, o_ref, *, slot, d_ff):
    mod = mod_ref[0]
    g = g_ref[...]
    res_gate = mod[3 * slot + 2:3 * slot + 3]
    tm = h_ref.shape[1]
    rows = tm // FFN_ROW_SPLITS
    for part in range(FFN_ROW_SPLITS):
        x = h_ref[0, part * rows:(part + 1) * rows, :]
        xm = _modulated(x, mod, g, slot).astype(BF16)
        gate = _dot(xm, win_ref[:, :d_ff])
        up = _dot(xm, win_ref[:, d_ff:])
        act = (gate * jax.nn.sigmoid(gate) * up).astype(BF16)
        y = _dot(act, wout_ref[...])
        o_ref[0, part * rows:(part + 1) * rows, :] = (
            x + FFN_RES * res_gate * _rms_rows(y, g[2 * slot + 1:2 * slot + 2]))


def _ffn(h, mod, g, w_in, w_out, slot):
    b, t, d = h.shape
    d_ff = w_out.shape[0]
    tm = _tile(t, 512)
    return pl.pallas_call(
        functools.partial(_ffn_kernel, slot=slot, d_ff=d_ff),
        grid=(b, t // tm),
        in_specs=[pl.BlockSpec((1, tm, d), lambda bb, i: (bb, i, 0)),
                  _mod_spec(mod),
                  _const_spec(g.shape),
                  _const_spec(w_in.shape),
                  _const_spec(w_out.shape)],
        out_specs=pl.BlockSpec((1, tm, d), lambda bb, i: (bb, i, 0)),
        out_shape=jax.ShapeDtypeStruct(h.shape, F32),
        compiler_params=_cparams(("parallel", "parallel")),
        name="ffn",
    )(h, mod, g, w_in, w_out)


def _even_in_kernel(h_ref, mod_ref, g_ref, w_ref, dft_ref, gb_ref, z_ref, a_ref, *, channel_dft):
    xm = _modulated(h_ref[0], mod_ref[0], g_ref[...], 1).astype(BF16)
    u = _dot(xm, w_ref[...])
    gb_ref[0] = u[:, :D_CONV].astype(BF16)
    z_ref[0] = (u[:, D_CONV:2 * D_CONV] * u[:, 2 * D_CONV:3 * D_CONV]).astype(BF16)
    xf = u[:, 3 * D_CONV:].astype(BF16)
    if not channel_dft:
        a_ref[0] = xf
        return
    for gi in range(FOURIER_GROUPS):
        pq = _dot(xf[:, gi * D_FG:(gi + 1) * D_FG], dft_ref[...])
        a_ref[0, :, gi * D_FG:(gi + 1) * D_FG] = pq[:, :D_FG].astype(BF16)
        a_ref[1, :, gi * D_FG:(gi + 1) * D_FG] = pq[:, D_FG:].astype(BF16)


def _even_in(h, mod, g, w_in, dft_c, channel_dft):
    b, t, d = h.shape
    tm = _tile(t, 512)
    if channel_dft:
        a_spec = pl.BlockSpec((2, tm, D_FOURIER), lambda bb, i: (0, i, bb))
        a_shape = jax.ShapeDtypeStruct((2, t, b * D_FOURIER), BF16)
    else:
        a_spec = pl.BlockSpec((1, tm, D_FOURIER), lambda bb, i: (bb, i, 0))
        a_shape = jax.ShapeDtypeStruct((b, t, D_FOURIER), BF16)
    return pl.pallas_call(
        functools.partial(_even_in_kernel, channel_dft=channel_dft),
        grid=(b, t // tm),
        in_specs=[pl.BlockSpec((1, tm, d), lambda bb, i: (bb, i, 0)),
                  _mod_spec(mod),
                  _const_spec(g.shape),
                  _const_spec(w_in.shape),
                  _const_spec(dft_c.shape)],
        out_specs=[pl.BlockSpec((1, tm, D_CONV), lambda bb, i: (bb, i, 0)),
                   pl.BlockSpec((1, tm, D_CONV), lambda bb, i: (bb, i, 0)),
                   a_spec],
        out_shape=[jax.ShapeDtypeStruct((b, t, D_CONV), BF16),
                   jax.ShapeDtypeStruct((b, t, D_CONV), BF16),
                   a_shape],
        compiler_params=_cparams(("parallel", "parallel")),
        name="even_in",
    )(h, mod, g, w_in, dft_c)


DFT_COLS = 64


def _dft_rows_kernel(x_ref, f_ref, tc_ref, ts_ref, o_ref):
    res = _dot(f_ref[...], x_ref[0])
    r = f_ref.shape[0] // 2
    reps = D_FOURIER // tc_ref.shape[2]
    for c in range(tc_ref.shape[0]):
        a_re = res[:r, c * D_FOURIER:(c + 1) * D_FOURIER]
        a_im = res[r:, c * D_FOURIER:(c + 1) * D_FOURIER]
        tc = jnp.concatenate([tc_ref[c]] * reps, axis=1)
        ts = jnp.concatenate([ts_ref[c]] * reps, axis=1)
        o_ref[0, 0, c] = (a_re * tc + a_im * ts).astype(BF16)
        o_ref[0, 1, c] = (a_im * tc - a_re * ts).astype(BF16)


def _dft_cols_kernel(b_ref, m_ref, cs_ref, o_ref, *, scale):
    z = _dot(m_ref[...], jnp.concatenate([b_ref[0, 0], b_ref[0, 1]], axis=0))
    kb = z.shape[1] // D_FOURIER
    for gi in range(FOURIER_GROUPS):
        lanes = [slice(k * D_FOURIER + gi * D_FG, k * D_FOURIER + (gi + 1) * D_FG) for k in range(kb)]
        z_re = jnp.concatenate([z[:DFT_COLS, ln] for ln in lanes], axis=0)
        z_im = jnp.concatenate([z[DFT_COLS:, ln] for ln in lanes], axis=0)
        y = _dot(jnp.concatenate([z_re, z_im], axis=1).astype(BF16), cs_ref[...]) * scale
        for k in range(kb):
            o_ref[0, :, lanes[k]] = y[k * DFT_COLS:(k + 1) * DFT_COLS].astype(o_ref.dtype)


def _factored_dft(xf, tabs):
    b, t, _ = xf.shape
    r = t // DFT_COLS
    cb, kb = 8, 8
    rows = pl.pallas_call(
        _dft_rows_kernel,
        grid=(b, DFT_COLS // cb),
        in_specs=[pl.BlockSpec((1, r, cb * D_FOURIER), lambda bb, j: (bb, 0, j)),
                  _const_spec(tabs["f_rows"].shape),
                  pl.BlockSpec((cb, r, HEAD_PAD), lambda bb, j: (j, 0, 0)),
                  pl.BlockSpec((cb, r, HEAD_PAD), lambda bb, j: (j, 0, 0))],
        out_specs=pl.BlockSpec((1, 2, cb, r, D_FOURIER), lambda bb, j: (bb, 0, j, 0, 0)),
        out_shape=jax.ShapeDtypeStruct((b, 2, DFT_COLS, r, D_FOURIER), BF16),
        compiler_params=_cparams(("parallel", "parallel")),
        name="dft_rows",
    )(xf.reshape(b, r, DFT_COLS * D_FOURIER), tabs["f_rows"], tabs["tw_cos"], tabs["tw_sin"])
    out = pl.pallas_call(
        functools.partial(_dft_cols_kernel, scale=1.0 / math.sqrt(t * D_FG)),
        grid=(b, r // kb),
        in_specs=[pl.BlockSpec((1, 2, DFT_COLS, kb * D_FOURIER), lambda bb, j: (bb, 0, 0, j)),
                  _const_spec(tabs["m_cols"].shape),
                  _const_spec(tabs["cs_chan"].shape)],
        out_specs=pl.BlockSpec((1, DFT_COLS, kb * D_FOURIER), lambda bb, j: (bb, 0, j)),
        out_shape=jax.ShapeDtypeStruct((b, DFT_COLS, r * D_FOURIER), BF16),
        compiler_params=_cparams(("parallel", "parallel")),
        name="dft_cols",
    )(rows.reshape(b, 2, DFT_COLS, r * D_FOURIER), tabs["m_cols"], tabs["cs_chan"])
    return out.reshape(b, t, D_FOURIER)


def _cos_sin(num, den):
    ang = (num % den).astype(F32) * (2.0 * math.pi / den)
    return jnp.cos(ang), jnp.sin(ang)


def _factored_dft_tables(t):
    r = t // DFT_COLS
    i_r = jnp.arange(r, dtype=jnp.int32)
    i_c = jnp.arange(DFT_COLS, dtype=jnp.int32)
    i_g = jnp.arange(D_FG, dtype=jnp.int32)
    c_r, s_r = _cos_sin(i_r[:, None] * i_r[None, :], r)
    c_t, s_t = _cos_sin(i_c[:, None] * i_r[None, :], t)
    c_c, s_c = _cos_sin(i_c[:, None] * i_c[None, :], DFT_COLS)
    c_g, s_g = _cos_sin(i_g[:, None] * i_g[None, :], D_FG)
    lanes = lambda a: jnp.broadcast_to(a[:, :, None], (DFT_COLS, r, HEAD_PAD))
    return {
        "f_rows": jnp.concatenate([c_r, -s_r], axis=0).astype(BF16),
        "tw_cos": lanes(c_t), "tw_sin": lanes(s_t),
        "m_cols": jnp.block([[c_c, s_c], [-s_c, c_c]]).astype(BF16),
        "cs_chan": jnp.concatenate([c_g, s_g], axis=0).astype(BF16),
    }


def _matmul_kernel(a_ref, b_ref, o_ref, acc_ref, *, scale):
    k = pl.program_id(2)

    @pl.when(k == 0)
    def _():
        acc_ref[...] = jnp.zeros_like(acc_ref)

    acc_ref[...] += _dot(a_ref[...], b_ref[...])

    @pl.when(k == pl.num_programs(2) - 1)
    def _():
        o_ref[...] = (acc_ref[...] * scale).astype(o_ref.dtype)


def _matmul(a, b, scale, out_dtype):
    m, kk = a.shape
    n = b.shape[1]
    bm, bn, bk = _tile(m, 1024), _tile(n, 1024), _tile(kk, 2048)
    return pl.pallas_call(
        functools.partial(_matmul_kernel, scale=scale),
        grid=(m // bm, n // bn, kk // bk),
        in_specs=[pl.BlockSpec((bm, bk), lambda i, j, k: (i, k)),
                  pl.BlockSpec((bk, bn), lambda i, j, k: (k, j))],
        out_specs=pl.BlockSpec((bm, bn), lambda i, j, k: (i, j)),
        out_shape=jax.ShapeDtypeStruct((m, n), out_dtype),
        scratch_shapes=[pltpu.VMEM((bm, bn), F32)],
        compiler_params=_cparams(("parallel", "parallel", "arbitrary")),
        name="dft_matmul",
    )(a, b)


def _even_out_kernel(h_ref, mod_ref, g_ref, gb_ref, z_ref, zp_ref, zn_ref, yf_ref, cw_ref, w_ref, o_ref):
    i = pl.program_id(1)
    x = h_ref[0]
    mod = mod_ref[0]
    g = g_ref[...]
    z = z_ref[0].astype(F32)
    tm = z.shape[0]
    halo = zp_ref.shape[1]
    prev_row = jnp.where(i > 0, zp_ref[0, halo - 1:halo, :].astype(F32), 0.0)
    next_row = jnp.where(i < pl.num_programs(1) - 1, zn_ref[0, 0:1, :].astype(F32), 0.0)
    row = lax.broadcasted_iota(jnp.int32, z.shape, 0)
    z_before = jnp.where(row == 0, prev_row, pltpu.roll(z, 1, 0))
    z_after = jnp.where(row == tm - 1, next_row, pltpu.roll(z, tm - 1, 0))
    cw = cw_ref[...]
    conv = z_before * cw[0:1] + z * cw[1:2] + z_after * cw[2:3]
    y_conv = (gb_ref[0].astype(F32) * conv).astype(BF16)
    y = _dot(y_conv, w_ref[:D_CONV, :]) + _dot(yf_ref[0], w_ref[D_CONV:, :])
    o_ref[0] = x + mod[5:6] * _rms_rows(y, g[3:4])


def _even_out(h, mod, g, gb, z, yf, conv_w, w_out):
    b, t, d = h.shape
    tm = _tile(t, 512)
    halo = 16
    nh = tm // halo
    last_halo = t // halo - 1
    return pl.pallas_call(
        _even_out_kernel,
        grid=(b, t // tm),
        in_specs=[pl.BlockSpec((1, tm, d), lambda bb, i: (bb, i, 0)),
                  _mod_spec(mod),
                  _const_spec(g.shape),
                  pl.BlockSpec((1, tm, D_CONV), lambda bb, i: (bb, i, 0)),
                  pl.BlockSpec((1, tm, D_CONV), lambda bb, i: (bb, i, 0)),
                  pl.BlockSpec((1, halo, D_CONV), lambda bb, i: (bb, jnp.maximum(i * nh - 1, 0), 0)),
                  pl.BlockSpec((1, halo, D_CONV), lambda bb, i: (bb, jnp.minimum((i + 1) * nh, last_halo), 0)),
                  pl.BlockSpec((1, tm, D_FOURIER), lambda bb, i: (bb, i, 0)),
                  _const_spec(conv_w.shape),
                  _const_spec(w_out.shape)],
        out_specs=pl.BlockSpec((1, tm, d), lambda bb, i: (bb, i, 0)),
        out_shape=jax.ShapeDtypeStruct(h.shape, F32),
        compiler_params=_cparams(("parallel", "parallel")),
        name="even_out",
    )(h, mod, g, gb, z, z, z, yf, conv_w, w_out)


def _dft_tables(t):
    n = jnp.arange(t, dtype=jnp.int32)
    ang = ((n[:, None] * n[None, :]) % t).astype(F32) * (2.0 * math.pi / t)
    w_pos = jnp.concatenate([jnp.cos(ang), -jnp.sin(ang)], axis=1).astype(BF16)
    return w_pos


def _channel_dft_table():
    n = jnp.arange(D_FG, dtype=jnp.int32)
    ang = ((n[:, None] * n[None, :]) % D_FG).astype(F32) * (2.0 * math.pi / D_FG)
    return jnp.concatenate([jnp.cos(ang), jnp.sin(ang)], axis=1).astype(BF16)


def _even_mixer(h, mod, g, w_in, conv_w, w_out, dft_c, dft_pos):
    b, t, _ = h.shape
    if isinstance(dft_pos, dict):
        gb, z, xf = _even_in(h, mod, g, w_in, dft_c, False)
        yf = _factored_dft(xf, dft_pos)
    else:
        gb, z, a = _even_in(h, mod, g, w_in, dft_c, True)
        yf = _matmul(dft_pos, a.reshape(2 * t, b * D_FOURIER), 1.0 / math.sqrt(t * D_FG), BF16)
        yf = yf.reshape(t, b, D_FOURIER).transpose(1, 0, 2)
    return _even_out(h, mod, g, gb, z, yf, conv_w, w_out)


def _odd_in_kernel(h_ref, mod_ref, g_ref, wtok_ref, wt_ref, wukp_ref, wuvt_ref, wuqt_ref, place_ref, gsel_ref,
                   gkv_row_ref, gkv_col_ref, gq_col_ref,
                   cosk_ref, sink_ref, tkr_ref, cosq_ref, sinq_ref, cosm_ref, sinm_ref,
                   kh_ref, vtd_ref, vtm_ref, kn_ref, *q_refs, with_q):
    xm = _modulated(h_ref[0], mod_ref[0], g_ref[...], 1).astype(BF16)
    tm = xm.shape[0]

    ut = _dot(xm, wtok_ref[...])
    k = ut[:, :DA_QCOLS]
    ckv = ut[:, DA_QCOLS:DA_QCOLS + MLA_KV_RANK]
    kr2 = ut[:, DA_QCOLS + MLA_KV_RANK:]
    lane = lax.broadcasted_iota(jnp.int32, k.shape, 1)
    first_half = (lane % (DA_DK // 2)) < (DA_DK // 4)
    k_sw = jnp.where(first_half, pltpu.roll(k, DA_QCOLS - DA_DK // 4, 1), pltpu.roll(k, DA_DK // 4, 1))
    reps = DA_QCOLS // HEAD_PAD
    cosk = jnp.concatenate([cosk_ref[...]] * reps, axis=1)
    sink = jnp.concatenate([sink_ref[...]] * reps, axis=1)
    k_rot = k * cosk + k_sw * sink
    ckvn = _rms_rows(ckv, gkv_row_ref[...]).astype(BF16)
    k_nope = _dot(ckvn, wukp_ref[...])
    pr = kr2 * tkr_ref[...]
    pr_hi = pr.astype(BF16)
    pr_lo = (pr - pr_hi.astype(F32)).astype(BF16)
    k_mla = k_nope + _dot(pr_hi, place_ref[...]) + _dot(pr_lo, place_ref[...])
    k_odd = pltpu.roll(k_rot, DA_QCOLS - DA_DK, 1)
    lane_h = lax.broadcasted_iota(jnp.int32, (tm, HEAD_PAD), 1)
    sq = []
    for mp in range(N_MAPS):
        if mp < 2 * DA_HEADS:
            src = (k_rot if mp % 2 == 0 else k_odd)[:, (mp // 2) * HEAD_PAD:(mp // 2 + 1) * HEAD_PAD]
            kd = jnp.where(lane_h < DA_DK, src, 0.0)
            one_lane = DA_DK
        else:
            hh = mp - 2 * DA_HEADS
            kd = k_mla[:, hh * HEAD_PAD:(hh + 1) * HEAD_PAD]
            one_lane = MLA_DQK
        kh_ref[0, mp] = jnp.where(lane_h == one_lane, 1.0, kd).astype(BF16)
        kf = kd.astype(BF16).astype(F32)
        sq.append((kf * kf).astype(BF16))
    kn_ref[0] = _dot_nt(gsel_ref[...], jnp.concatenate(sq, axis=1))

    r0 = DA_VCOLS
    vckv = _dot_nt(wt_ref[:r0 + MLA_KV_RANK, :], xm)
    for hh in range(DA_HEADS):
        vtd_ref[0, hh] = vckv[hh * DA_DV:(hh + 1) * DA_DV].astype(BF16)
    ckv_t = vckv[r0:]
    ms = jnp.mean(ckv_t * ckv_t, axis=0, keepdims=True)
    ckvn_t = (ckv_t * lax.rsqrt(ms + EPS) * gkv_col_ref[...]).astype(BF16)
    vm_t = _dot(wuvt_ref[...], ckvn_t)
    for hh in range(MLA_HEADS):
        vtm_ref[0, hh] = vm_t[hh * MLA_DV:(hh + 1) * MLA_DV].astype(BF16)

    if with_q:
        qtd_ref, qtm_ref = q_refs
        r1 = r0 + MLA_KV_RANK
        q_t = _dot_nt(wt_ref[r1:, :], xm)
        cosq, sinq = cosq_ref[...], sinq_ref[...]
        qd = DA_DK // 4
        for mp in range(2 * DA_HEADS):
            q = q_t[mp * DA_DK:(mp + 1) * DA_DK]
            q_sw = jnp.concatenate([q[qd:2 * qd], q[:qd], q[3 * qd:], q[2 * qd:3 * qd]], axis=0)
            q_rot = (q * cosq + q_sw * sinq) * (DA_SCALE * LOG2E)
            qtd_ref[0, mp // 2, (mp % 2) * DA_DK:(mp % 2 + 1) * DA_DK, :] = q_rot.astype(BF16)
        cq_t = q_t[DA_QCOLS:]
        ms = jnp.mean(cq_t * cq_t, axis=0, keepdims=True)
        cqn_t = (cq_t * lax.rsqrt(ms + EPS) * gq_col_ref[...]).astype(BF16)
        qm_t = _dot(wuqt_ref[...], cqn_t) * (MLA_SCALE * LOG2E)
        cosm, sinm = cosm_ref[...], sinm_ref[...]
        rd = MLA_ROPE // 4
        for hh in range(MLA_HEADS):
            base = hh * HEAD_PAD
            qtm_ref[0, hh, :MLA_NOPE, :] = qm_t[base:base + MLA_NOPE].astype(BF16)
            r = qm_t[base + MLA_NOPE:base + MLA_DQK]
            r_sw = jnp.concatenate([r[rd:2 * rd], r[:rd], r[3 * rd:], r[2 * rd:3 * rd]], axis=0)
            qtm_ref[0, hh, MLA_NOPE:MLA_DQK, :] = (r * cosm + r_sw * sinm).astype(BF16)
            qtm_ref[0, hh, MLA_DQK:, :] = jnp.zeros((HEAD_PAD - MLA_DQK, tm), BF16)


def _odd_in(h, mod, g, wts, tabs, kv_prev, tok_off, with_q):
    b, t, d = h.shape
    tk_total = tabs["tk_total"]
    tm = _tile(t, 512)
    assert tok_off % tm == 0
    off = tok_off // tm
    row_tab = lambda w: pl.BlockSpec((tm, w), lambda bb, i: (i, 0))
    col_tab = lambda r: pl.BlockSpec((r, tm), lambda bb, i: (0, i))
    consts = [wts["wtok"], wts["wt"], wts["wukp"], wts["wuvt"], wts["wuqt"], wts["place"], wts["gsel"],
              wts["gkv_row"], wts["gkv_col"], wts["gq_col"]]
    in_specs = ([pl.BlockSpec((1, tm, d), lambda bb, i: (bb, i, 0)), _mod_spec(mod), _const_spec(g.shape)]
                + [_const_spec(c.shape) for c in consts]
                + [row_tab(HEAD_PAD), row_tab(HEAD_PAD), row_tab(2 * MLA_ROPE),
                   col_tab(DA_DK), col_tab(DA_DK), col_tab(MLA_ROPE), col_tab(MLA_ROPE)])
    args = [h, mod, g] + consts + [tabs["cosk"], tabs["sink"], tabs["tkr"],
                                   tabs["cosq"], tabs["sinq"], tabs["cosm"], tabs["sinm"]]
    out_specs = [pl.BlockSpec((1, N_MAPS, tm, HEAD_PAD), lambda bb, i: (bb, 0, i + off, 0)),
                 pl.BlockSpec((1, DA_HEADS, DA_DV, tm), lambda bb, i: (bb, 0, 0, i + off)),
                 pl.BlockSpec((1, MLA_HEADS, MLA_DV, tm), lambda bb, i: (bb, 0, 0, i + off)),
                 pl.BlockSpec((1, N_MAPS, tm), lambda bb, i: (bb, 0, i + off))]
    out_shape = [jax.ShapeDtypeStruct((b, N_MAPS, tk_total, HEAD_PAD), BF16),
                 jax.ShapeDtypeStruct((b, DA_HEADS, DA_DV, tk_total), BF16),
                 jax.ShapeDtypeStruct((b, MLA_HEADS, MLA_DV, tk_total), BF16),
                 jax.ShapeDtypeStruct((b, N_MAPS, tk_total), F32)]
    if with_q:
        out_specs += [pl.BlockSpec((1, DA_HEADS, HEAD_PAD, tm), lambda bb, i: (bb, 0, 0, i)),
                      pl.BlockSpec((1, MLA_HEADS, HEAD_PAD, tm), lambda bb, i: (bb, 0, 0, i))]
        out_shape += [jax.ShapeDtypeStruct((b, DA_HEADS, HEAD_PAD, t), BF16),
                      jax.ShapeDtypeStruct((b, MLA_HEADS, HEAD_PAD, t), BF16)]
    aliases = {}
    kernel_fn = functools.partial(_odd_in_kernel, with_q=with_q)
    if kv_prev is not None:
        n_in = len(args)
        in_specs += [pl.BlockSpec(memory_space=pl.ANY)] * N_KV_ARRAYS
        args += list(kv_prev)
        aliases = {n_in + a: a for a in range(N_KV_ARRAYS)}
        kernel_fn = functools.partial(_odd_in_alias_kernel, n_in=n_in, with_q=with_q)
    return pl.pallas_call(
        kernel_fn,
        grid=(b, t // tm),
        in_specs=in_specs,
        out_specs=out_specs,
        out_shape=out_shape,
        input_output_aliases=aliases,
        compiler_params=_cparams(("parallel", "parallel")),
        name="odd_in",
    )(*args)


def _odd_in_alias_kernel(*refs, n_in, with_q):
    _odd_in_kernel(*refs[:n_in], *refs[n_in + N_KV_ARRAYS:], with_q=with_q)


def _attn_kernel(lam_ref, gsub_ref, kn_ref, qtd_ref, qtm_ref, kh_ref, vtd_ref, vtm_ref,
                 otd_ref, otm_ref, lmin_ref, qa_ref, m_ref, l_ref, accd_ref, accm_ref, *, lam_init, safe):
    j = pl.program_id(2)
    tq = qtd_ref.shape[-1]

    @pl.when(j == 0)
    def _():
        m_ref[...] = jnp.full_like(m_ref, NEG_BIG)
        l_ref[...] = jnp.zeros_like(l_ref)
        accd_ref[...] = jnp.zeros_like(accd_ref)
        accm_ref[...] = jnp.zeros_like(accm_ref)
        kmax = jnp.sqrt(jnp.max(kn_ref[0], axis=1, keepdims=True))
        shift_row = lax.broadcasted_iota(jnp.int32, (16, tq), 0) == 0
        for idx in range(N_MAPS):
            if idx < 2 * DA_HEADS:
                q = qtd_ref[0, idx // 2, (idx % 2) * DA_DK:(idx % 2 + 1) * DA_DK, :]
            else:
                q = qtm_ref[0, idx - 2 * DA_HEADS, :MLA_DQK, :]
            if safe:
                shift_blk = jnp.zeros((16, tq), BF16)
            else:
                qf = q.astype(F32)
                qn = jnp.sqrt(jnp.sum(qf * qf, axis=0, keepdims=True))
                shift = -(BOUND_SLACK * kmax[idx:idx + 1]) * qn
                shift_blk = jnp.where(shift_row, shift, 0.0).astype(BF16)
            pad = jnp.zeros((HEAD_PAD - q.shape[0] - 16, tq), BF16)
            qa_ref[idx] = jnp.concatenate([q, shift_blk, pad], axis=0)

    def one_map(idx, v_t, acc_ref, acc_idx):
        s = _dot(kh_ref[0, idx], qa_ref[idx])
        if safe:
            m_old = m_ref[idx]
            m_new = jnp.maximum(m_old, jnp.max(s, axis=0, keepdims=True))
            alpha = jnp.exp2(m_old - m_new)
            m_ref[idx] = m_new
            p = jnp.exp2(s - m_new)
            l_ref[idx] = alpha * l_ref[idx] + jnp.sum(p.reshape(-1, 8, tq), axis=0)
            acc_ref[acc_idx] = acc_ref[acc_idx] * alpha + _dot(v_t, p.astype(BF16))
        else:
            p = jnp.exp2(s)
            l_ref[idx] += jnp.sum(p.reshape(-1, 8, tq), axis=0)
            acc_ref[acc_idx] += _dot(v_t, p.astype(BF16))

    def da_head(hh, carry):
        v_t = vtd_ref[0, hh]
        one_map(2 * hh, v_t, accd_ref, 2 * hh)
        one_map(2 * hh + 1, v_t, accd_ref, 2 * hh + 1)
        return carry

    def mla_head(hh, carry):
        one_map(2 * DA_HEADS + hh, vtm_ref[0, hh], accm_ref, hh)
        return carry

    lax.fori_loop(0, DA_HEADS // DA_HEADS_PER_ITER,
                  lambda it, c: [da_head(it * DA_HEADS_PER_ITER + u, c) for u in range(DA_HEADS_PER_ITER)][-1], 0)
    lax.fori_loop(0, MLA_HEADS // MLA_HEADS_PER_ITER,
                  lambda it, c: [mla_head(it * MLA_HEADS_PER_ITER + u, c) for u in range(MLA_HEADS_PER_ITER)][-1], 0)

    @pl.when(j == pl.num_programs(2) - 1)
    def _():
        lp = lam_ref[...]
        lam = (jnp.exp(jnp.sum(lp[0:1] * lp[1:2], axis=1, keepdims=True))
               - jnp.exp(jnp.sum(lp[2:3] * lp[3:4], axis=1, keepdims=True)) + lam_init)
        gsub = gsub_ref[...]
        lsum = [jnp.sum(l_ref[idx], axis=0, keepdims=True) for idx in range(N_MAPS)]
        for hh in range(DA_HEADS):
            o = accd_ref[2 * hh] / lsum[2 * hh] - lam * (accd_ref[2 * hh + 1] / lsum[2 * hh + 1])
            ms = jnp.mean(o * o, axis=0, keepdims=True)
            otd_ref[0, hh] = (o * lax.rsqrt(ms + EPS) * gsub * (1.0 - lam_init)).astype(otd_ref.dtype)
        for hh in range(MLA_HEADS):
            otm_ref[0, hh] = (accm_ref[hh] / lsum[2 * DA_HEADS + hh]).astype(otm_ref.dtype)
        lmin = lsum[0]
        for idx in range(1, N_MAPS):
            lmin = jnp.minimum(lmin, lsum[idx])
        lmin_ref[0, 0] = jnp.broadcast_to(jnp.min(lmin, axis=1, keepdims=True), lmin_ref.shape[2:])


def _attention_call(lam_p, gsub_col, qtd, qtm, kv, lam_init, tq, tk, kv_off, n_kv, safe):
    kh, vtd, vtm, kn = kv
    b, _, _, t = qtd.shape
    nq = t // tq
    assert kv_off % n_kv == 0
    return pl.pallas_call(
        functools.partial(_attn_kernel, lam_init=lam_init, safe=safe),
        grid=(b, nq, n_kv),
        in_specs=[_const_spec(lam_p.shape),
                  _const_spec(gsub_col.shape),
                  pl.BlockSpec((1, N_MAPS, n_kv * tk), lambda bb, i, j: (bb, 0, kv_off // n_kv)),
                  pl.BlockSpec((1, DA_HEADS, HEAD_PAD, tq), lambda bb, i, j: (bb, 0, 0, i)),
                  pl.BlockSpec((1, MLA_HEADS, HEAD_PAD, tq), lambda bb, i, j: (bb, 0, 0, i)),
                  pl.BlockSpec((1, N_MAPS, tk, HEAD_PAD), lambda bb, i, j: (bb, 0, kv_off + j, 0)),
                  pl.BlockSpec((1, DA_HEADS, DA_DV, tk), lambda bb, i, j: (bb, 0, 0, kv_off + j)),
                  pl.BlockSpec((1, MLA_HEADS, MLA_DV, tk), lambda bb, i, j: (bb, 0, 0, kv_off + j))],
        out_specs=[pl.BlockSpec((1, DA_HEADS, DA_DV, tq), lambda bb, i, j: (bb, 0, 0, i)),
                   pl.BlockSpec((1, MLA_HEADS, MLA_DV, tq), lambda bb, i, j: (bb, 0, 0, i)),
                   pl.BlockSpec((1, 1, 8, 128), lambda bb, i, j: (bb, i, 0, 0))],
        out_shape=[jax.ShapeDtypeStruct((b, DA_HEADS, DA_DV, t), BF16),
                   jax.ShapeDtypeStruct((b, MLA_HEADS, MLA_DV, t), BF16),
                   jax.ShapeDtypeStruct((b, nq, 8, 128), F32)],
        scratch_shapes=[pltpu.VMEM((N_MAPS, HEAD_PAD, tq), BF16),
                        pltpu.VMEM((N_MAPS, 1, tq), F32),
                        pltpu.VMEM((N_MAPS, 8, tq), F32),
                        pltpu.VMEM((2 * DA_HEADS, DA_DV, tq), F32),
                        pltpu.VMEM((MLA_HEADS, MLA_DV, tq), F32)],
        compiler_params=_cparams(("parallel", "parallel", "arbitrary")),
        name="attention_safe" if safe else "attention",
    )(lam_p, gsub_col, kn, qtd, qtm, kh, vtd, vtm)


def _attention(lam_p, gsub_col, qtd, qtm, kv, lam_init, tq, tk, kv_off, n_kv):
    args = (lam_p, gsub_col, qtd, qtm, kv)
    otd, otm, lmin = _attention_call(*args, lam_init, tq, tk, kv_off, n_kv, False)
    ok = jnp.min(lmin) >= MIN_DENOMINATOR
    return lax.cond(ok, lambda *_: (otd, otm),
                    lambda *a: tuple(_attention_call(*a, lam_init, tq, tk, kv_off, n_kv, True)[:2]), *args)


def _odd_out_kernel(h_ref, mod_ref, g_ref, otd_ref, otm_ref, wt_ref, o_ref):
    y_t = _dot(wt_ref[:, :DA_VCOLS], otd_ref[0]) + _dot(wt_ref[:, DA_VCOLS:], otm_ref[0])
    y = y_t.T
    o_ref[0] = h_ref[0] + mod_ref[0][5:6] * _rms_rows(y, g_ref[3:4])


def _odd_out(h, mod, g, otd, otm, w_out_t):
    b, t, d = h.shape
    tm = _tile(t, 512)
    otd = otd.reshape(b, DA_VCOLS, t)
    otm = otm.reshape(b, MLA_HEADS * MLA_DV, t)
    return pl.pallas_call(
        _odd_out_kernel,
        grid=(b, t // tm),
        in_specs=[pl.BlockSpec((1, tm, d), lambda bb, i: (bb, i, 0)),
                  _mod_spec(mod),
                  _const_spec(g.shape),
                  pl.BlockSpec((1, DA_VCOLS, tm), lambda bb, i: (bb, 0, i)),
                  pl.BlockSpec((1, MLA_HEADS * MLA_DV, tm), lambda bb, i: (bb, 0, i)),
                  _const_spec(w_out_t.shape)],
        out_specs=pl.BlockSpec((1, tm, d), lambda bb, i: (bb, i, 0)),
        out_shape=jax.ShapeDtypeStruct(h.shape, F32),
        compiler_params=_cparams(("parallel", "parallel")),
        name="odd_out",
    )(h, mod, g, otd, otm, w_out_t)


def _rope_angles(n_tokens, rot_dim):
    rows = n_tokens // GRID_W
    row = jnp.broadcast_to(jnp.arange(rows)[:, None], (rows, GRID_W)).reshape(-1).astype(F32)
    col = jnp.broadcast_to(jnp.arange(GRID_W)[None, :], (rows, GRID_W)).reshape(-1).astype(F32)
    n_freq = rot_dim // 4
    freqs = ROPE_THETA ** (-jnp.arange(n_freq, dtype=F32) / n_freq)
    return row[:, None] * freqs, col[:, None] * freqs


def _rope_cos_sin(n_tokens, rot_dim, identity):
    if identity:
        return jnp.ones((n_tokens, rot_dim), F32), jnp.zeros((n_tokens, rot_dim), F32)
    ar, ac = _rope_angles(n_tokens, rot_dim)
    cos = jnp.concatenate([jnp.cos(ar), jnp.cos(ar), jnp.cos(ac), jnp.cos(ac)], axis=1)
    sin = jnp.concatenate([-jnp.sin(ar), jnp.sin(ar), -jnp.sin(ac), jnp.sin(ac)], axis=1)
    return cos, sin


def _rope_tables(n_tokens, identity, tk_total):
    cd, sd = _rope_cos_sin(n_tokens, DA_DK, identity)
    cm, sm = _rope_cos_sin(n_tokens, MLA_ROPE, identity)
    return {
        "cosk": jnp.concatenate([cd, cd], axis=1), "sink": jnp.concatenate([sd, sd], axis=1),
        "tkr": jnp.concatenate([cm, sm], axis=1),
        "cosq": cd.T, "sinq": sd.T, "cosm": cm.T, "sinm": sm.T,
        "tk_total": tk_total,
    }


def _swap_perm(rot_dim):
    q = rot_dim // 4
    return jnp.concatenate([jnp.arange(q, 2 * q), jnp.arange(0, q), jnp.arange(3 * q, 4 * q), jnp.arange(2 * q, 3 * q)])


def _odd_weights(w_in, g_q, w_uq, g_kv, w_uk, w_uv):
    w_q = w_in[:, :DA_QCOLS]
    w_cq = w_in[:, DA_QCOLS:Q_COLS]
    w_k = w_in[:, Q_COLS:Q_COLS + DA_QCOLS]
    w_v = w_in[:, Q_COLS + DA_QCOLS:Q_COLS + DA_QCOLS + DA_VCOLS]
    w_ckv = w_in[:, Q_COLS + DA_QCOLS + DA_VCOLS:Q_COLS + DA_QCOLS + DA_VCOLS + MLA_KV_RANK]
    w_kr = w_in[:, Q_COLS + DA_QCOLS + DA_VCOLS + MLA_KV_RANK:]
    wtok = jnp.concatenate([w_k, w_ckv, w_kr, w_kr[:, _swap_perm(MLA_ROPE)]], axis=1).astype(BF16)
    wt = jnp.concatenate([w_v, w_ckv, w_q, w_cq], axis=1).T.astype(BF16)
    pad_k = jnp.zeros((MLA_KV_RANK, MLA_HEADS, HEAD_PAD), F32)
    wukp = pad_k.at[:, :, :MLA_NOPE].set(w_uk.reshape(MLA_KV_RANK, MLA_HEADS, MLA_NOPE))
    wukp = wukp.reshape(MLA_KV_RANK, MLA_HEADS * HEAD_PAD).astype(BF16)
    pad_q = jnp.zeros((MLA_Q_RANK, MLA_HEADS, HEAD_PAD), F32)
    wuqp = pad_q.at[:, :, :MLA_DQK].set(w_uq.reshape(MLA_Q_RANK, MLA_HEADS, MLA_DQK))
    wuqt = wuqp.reshape(MLA_Q_RANK, MLA_HEADS * HEAD_PAD).T.astype(BF16)
    eye = jnp.eye(MLA_ROPE, dtype=F32)
    place = jnp.zeros((2, MLA_ROPE, MLA_HEADS, HEAD_PAD), F32)
    place = place.at[:, :, :, MLA_NOPE:MLA_DQK].set(jnp.broadcast_to(eye[None, :, None, :], (2, MLA_ROPE, MLA_HEADS, MLA_ROPE)))
    place = place.reshape(2 * MLA_ROPE, MLA_HEADS * HEAD_PAD).astype(BF16)
    return {
        "wtok": wtok, "wt": wt, "wukp": wukp, "wuvt": w_uv.T.astype(BF16), "wuqt": wuqt, "place": place,
        "gsel": jnp.kron(jnp.eye(N_MAPS, dtype=F32), jnp.ones((1, HEAD_PAD), F32)).astype(BF16),
        "gkv_row": g_kv.reshape(1, -1).astype(F32), "gkv_col": g_kv.reshape(-1, 1).astype(F32),
        "gq_col": g_q.reshape(-1, 1).astype(F32),
    }


def kernel(x, c, ctx, c_ctx, w_mod, b_mod, norm_g, w_ffn_in, w_ffn_out, w_in_even, conv_w, w_out_even,
           w_in_odd, g_q_mla, w_uq, g_kv_mla, w_uk, w_uv, lam_q1, lam_k1, lam_q2, lam_k2, g_subln, w_out_odd):
    b, t, d = x.shape
    tc = ctx.shape[1]
    depth = w_mod.shape[0]
    tk_total = t + tc

    rows = -(-(b + 1) // 8) * 8
    cond = jnp.zeros((rows, d), F32).at[:b].set(c).at[b].set(c_ctx)
    mod = _adaln(cond, w_mod, b_mod).reshape(depth, rows, N_MOD, d)

    dft_c = _channel_dft_table()
    factored = lambda n: n % (DFT_COLS * 16) == 0
    w_pos_x = _factored_dft_tables(t) if factored(t) else _dft_tables(t)
    w_pos_c = _factored_dft_tables(tc) if factored(tc) else _dft_tables(tc)
    tabs_x = _rope_tables(t, False, tk_total)
    tabs_c = _rope_tables(tc, True, tk_total)
    tq = _tile(t, 1024)
    tk = 768 if tk_total % 768 == 0 else _tile(tk_total, 512)
    assert t % tc == 0 and tk_total % tk == 0

    h, hc = x, ctx
    for l in range(depth):
        last = l == depth - 1
        odd = l % 2 == 1
        ctx_live = (not last) or odd
        g = norm_g[l]
        m_x, m_c = mod[l, :b], mod[l, b:b + 1]
        wi0, wo0 = w_ffn_in[l, 0].astype(BF16), w_ffn_out[l, 0].astype(BF16)
        h = _ffn(h, m_x, g, wi0, wo0, 0)
        if ctx_live:
            hc = _ffn(hc, m_c, g, wi0, wo0, 0)

        if not odd:
            e = l // 2
            w_in, w_out = w_in_even[e].astype(BF16), w_out_even[e].astype(BF16)
            h = _even_mixer(h, m_x, g, w_in, conv_w[e], w_out, dft_c, w_pos_x)
            if ctx_live:
                hc = _even_mixer(hc, m_c, g, w_in, conv_w[e], w_out, dft_c, w_pos_c)
        else:
            o = l // 2
            lam_init = 0.8 - 0.6 * math.exp(-0.3 * l)
            wts = _odd_weights(w_in_odd[o], g_q_mla[o], w_uq[o], g_kv_mla[o], w_uk[o], w_uv[o])
            lam_p = jnp.stack([lam_q1[o], lam_k1[o], lam_q2[o], lam_k2[o]]).astype(F32)
            gsub_col = g_subln[o].reshape(-1, 1).astype(F32)
            w_out_t = w_out_odd[o].T.astype(BF16)
            outs_x = _odd_in(h, m_x, g, wts, tabs_x, None, 0, True)
            qtd, qtm = outs_x[N_KV_ARRAYS:]
            outs_c = _odd_in(hc, m_c, g, wts, tabs_c, outs_x[:N_KV_ARRAYS], t, not last)
            kv = outs_c[:N_KV_ARRAYS]
            otd, otm = _attention(lam_p, gsub_col, qtd, qtm, kv, lam_init, tq, tk, 0, tk_total // tk)
            if not last:
                ocd, ocm = _attention_call(lam_p, gsub_col, outs_c[N_KV_ARRAYS], outs_c[N_KV_ARRAYS + 1], kv,
                                           lam_init, tc, tc, t // tc, 1, True)[:2]
                hc = _odd_out(hc, m_c, g, ocd, ocm, w_out_t)
            h = _odd_out(h, m_x, g, otd, otm, w_out_t)

        wi1, wo1 = w_ffn_in[l, 1].astype(BF16), w_ffn_out[l, 1].astype(BF16)
        h = _ffn(h, m_x, g, wi1, wo1, 2)
        if not last:
            hc = _ffn(hc, m_c, g, wi1, wo1, 2)
    return h
```

```python
import functools
import math

import jax
import jax.numpy as jnp
from jax import lax
from jax.experimental import pallas as pl
from jax.experimental.pallas import tpu as pltpu

F32 = jnp.float32
BF16 = jnp.bfloat16

D_MODEL = 1024
GRID_W = 64
N_MOD = 9
FFN_RES = 0.5
EPS = 1e-6
ROPE_THETA = 10000.0

D_CONV = 512
D_FOURIER = 512
FOURIER_GROUPS = 4
D_FG = D_FOURIER // FOURIER_GROUPS

DA_HEADS = 8
DA_DK = 64
DA_DV = 128
DA_SCALE = DA_DK ** -0.5
MLA_HEADS = 8
MLA_NOPE = 64
MLA_ROPE = 32
MLA_DQK = MLA_NOPE + MLA_ROPE
MLA_DV = 64
MLA_Q_RANK = 384
MLA_KV_RANK = 256
MLA_SCALE = MLA_DQK ** -0.5
DA_QCOLS = DA_HEADS * 2 * DA_DK
DA_VCOLS = DA_HEADS * DA_DV
Q_COLS = DA_QCOLS + MLA_Q_RANK
HEAD_PAD = 128
N_MAPS = 2 * DA_HEADS + MLA_HEADS
LOG2E = 1.4426950408889634
NEG_BIG = -1e30
N_KV_ARRAYS = 5
F8 = jnp.float8_e4m3fn
F8_SAFE_MAX = 400.0
DA_SPLIT_SCALE = math.sqrt(DA_SCALE * LOG2E)
FFN_ROW_SPLITS = 2
FFN_GROUP_ROWS = 256
DA_HEADS_PER_ITER = 2
MLA_HEADS_PER_ITER = 4
ATTN_Q_CHUNK = 256
BOUND_SLACK = 1.02
MIN_DENOMINATOR = 2.0 ** -40

VMEM_LIMIT_V7X = 56 * 1024 * 1024


def _cparams(sem):
    return pltpu.CompilerParams(dimension_semantics=sem, vmem_limit_bytes=VMEM_LIMIT_V7X)


def _tile(n, pref):
    if n <= pref:
        return n
    t = pref - pref % 128
    while t >= 128:
        if n % t == 0:
            return t
        t -= 128
    return n


def _const_spec(shape):
    nd = len(shape)
    return pl.BlockSpec(shape, lambda *_: (0,) * nd, pipeline_mode=pl.Buffered(1))


def _mod_spec(mod):
    if mod.shape[0] == 1:
        return pl.BlockSpec((1, N_MOD, D_MODEL), lambda b, *_: (0, 0, 0))
    return pl.BlockSpec((1, N_MOD, D_MODEL), lambda b, *_: (b, 0, 0))


def _rms_rows(x, g):
    ms = jnp.mean(x * x, axis=-1, keepdims=True)
    return x * lax.rsqrt(ms + EPS) * g


def _modulated(x, mod, g, slot):
    shift = mod[3 * slot:3 * slot + 1]
    scale = mod[3 * slot + 1:3 * slot + 2]
    return _rms_rows(x, g[2 * slot:2 * slot + 1]) * (1.0 + scale) + shift


def _split8(x):
    hi = x.astype(F8).astype(F32)
    lo = (x - hi).astype(F8).astype(F32)
    return hi, lo


def _dot(a, b):
    return jnp.dot(a, b, preferred_element_type=F32)


def _dot_nt(a, b):
    return lax.dot_general(a, b, (((1,), (1,)), ((), ())), preferred_element_type=F32)


def _adaln_kernel(c_ref, w_ref, b_ref, o_ref):
    c = c_ref[...]
    a = c * jax.nn.sigmoid(c)
    o_ref[0] = jnp.dot(a, w_ref[0], preferred_element_type=F32,
                       precision=lax.Precision.HIGHEST) + b_ref[0]


def _adaln(cond, w_mod, b_mod):
    depth, d, n = w_mod.shape
    rows = cond.shape[0]
    tn = _tile(n, 1152)
    return pl.pallas_call(
        _adaln_kernel,
        grid=(depth, n // tn),
        in_specs=[pl.BlockSpec((rows, d), lambda l, j: (0, 0)),
                  pl.BlockSpec((1, d, tn), lambda l, j: (l, 0, j)),
                  pl.BlockSpec((1, 1, tn), lambda l, j: (l, 0, j))],
        out_specs=pl.BlockSpec((1, rows, tn), lambda l, j: (l, 0, j)),
        out_shape=jax.ShapeDtypeStruct((depth, rows, n), F32),
        compiler_params=_cparams(("parallel", "parallel")),
        name="adaln",
    )(cond, w_mod, b_mod.reshape(depth, 1, n))


def _ffn_kernel(h_ref, mod_ref, g_ref, win_ref, wout_ref, o_ref, *, slot, d_ff):
    mod = mod_ref[0]
    g = g_ref[...]
    res_gate = mod[3 * slot + 2:3 * slot + 3]
    tm = h_ref.shape[1]
    n_groups = max(1, tm // FFN_GROUP_ROWS)
    rows = tm // n_groups
    for part in range(n_groups):
        x = h_ref[0, part * rows:(part + 1) * rows, :]
        xm = _modulated(x, mod, g, slot).astype(BF16)
        gate = _dot(xm, win_ref[:, :d_ff])
        up = _dot(xm, win_ref[:, d_ff:])
        act = (gate * jax.nn.sigmoid(gate) * up).astype(BF16)
        y = _dot(act, wout_ref[...])
        o_ref[0, part * rows:(part + 1) * rows, :] = (
            x + FFN_RES * res_gate * _rms_rows(y, g[2 * slot + 1:2 * slot + 2]))


def _ffn(h, mod, g, w_in, w_out, slot):
    b, t, d = h.shape
    d_ff = w_out.shape[0]
    tm = _tile(t, FFN_ROW_SPLITS * FFN_GROUP_ROWS)
    return pl.pallas_call(
        functools.partial(_ffn_kernel, slot=slot, d_ff=d_ff),
        grid=(b, t // tm),
        in_specs=[pl.BlockSpec((1, tm, d), lambda bb, i: (bb, i, 0)),
                  _mod_spec(mod),
                  _const_spec(g.shape),
                  _const_spec(w_in.shape),
                  _const_spec(w_out.shape)],
        out_specs=pl.BlockSpec((1, tm, d), lambda bb, i: (bb, i, 0)),
        out_shape=jax.ShapeDtypeStruct(h.shape, F32),
        compiler_params=_cparams(("parallel", "parallel")),
        name="ffn",
    )(h, mod, g, w_in, w_out)


def _even_in_kernel(h_ref, mod_ref, g_ref, w_ref, dft_ref, gb_ref, z_ref, a_ref, *, channel_dft):
    xm = _modulated(h_ref[0], mod_ref[0], g_ref[...], 1).astype(BF16)
    u = _dot(xm, w_ref[...])
    gb_ref[0] = u[:, :D_CONV].astype(BF16)
    z_ref[0] = (u[:, D_CONV:2 * D_CONV] * u[:, 2 * D_CONV:3 * D_CONV]).astype(BF16)
    xf = u[:, 3 * D_CONV:].astype(BF16)
    if not channel_dft:
        a_ref[0] = xf
        return
    for gi in range(FOURIER_GROUPS):
        pq = _dot(xf[:, gi * D_FG:(gi + 1) * D_FG], dft_ref[...])
        a_ref[0, :, gi * D_FG:(gi + 1) * D_FG] = pq[:, :D_FG].astype(BF16)
        a_ref[1, :, gi * D_FG:(gi + 1) * D_FG] = pq[:, D_FG:].astype(BF16)


def _even_in(h, mod, g, w_in, dft_c, channel_dft):
    b, t, d = h.shape
    tm = _tile(t, 512)
    if channel_dft:
        a_spec = pl.BlockSpec((2, tm, D_FOURIER), lambda bb, i: (0, i, bb))
        a_shape = jax.ShapeDtypeStruct((2, t, b * D_FOURIER), BF16)
    else:
        a_spec = pl.BlockSpec((1, tm, D_FOURIER), lambda bb, i: (bb, i, 0))
        a_shape = jax.ShapeDtypeStruct((b, t, D_FOURIER), BF16)
    return pl.pallas_call(
        functools.partial(_even_in_kernel, channel_dft=channel_dft),
        grid=(b, t // tm),
        in_specs=[pl.BlockSpec((1, tm, d), lambda bb, i: (bb, i, 0)),
                  _mod_spec(mod),
                  _const_spec(g.shape),
                  _const_spec(w_in.shape),
                  _const_spec(dft_c.shape)],
        out_specs=[pl.BlockSpec((1, tm, D_CONV), lambda bb, i: (bb, i, 0)),
                   pl.BlockSpec((1, tm, D_CONV), lambda bb, i: (bb, i, 0)),
                   a_spec],
        out_shape=[jax.ShapeDtypeStruct((b, t, D_CONV), BF16),
                   jax.ShapeDtypeStruct((b, t, D_CONV), BF16),
                   a_shape],
        compiler_params=_cparams(("parallel", "parallel")),
        name="even_in",
    )(h, mod, g, w_in, dft_c)


DFT_COLS = 64


def _dft_rows_kernel(x_ref, f_ref, tc_ref, ts_ref, o_ref):
    res = _dot(f_ref[...], x_ref[0])
    r = f_ref.shape[0] // 2
    reps = D_FOURIER // tc_ref.shape[2]
    for c in range(tc_ref.shape[0]):
        a_re = res[:r, c * D_FOURIER:(c + 1) * D_FOURIER]
        a_im = res[r:, c * D_FOURIER:(c + 1) * D_FOURIER]
        tc = jnp.concatenate([tc_ref[c]] * reps, axis=1)
        ts = jnp.concatenate([ts_ref[c]] * reps, axis=1)
        o_ref[0, 0, c] = (a_re * tc + a_im * ts).astype(BF16)
        o_ref[0, 1, c] = (a_im * tc - a_re * ts).astype(BF16)


def _dft_cols_kernel(b_ref, m_ref, cs_ref, o_ref, *, scale):
    z = _dot(m_ref[...], jnp.concatenate([b_ref[0, 0], b_ref[0, 1]], axis=0))
    kb = z.shape[1] // D_FOURIER
    for gi in range(FOURIER_GROUPS):
        lanes = [slice(k * D_FOURIER + gi * D_FG, k * D_FOURIER + (gi + 1) * D_FG) for k in range(kb)]
        z_re = jnp.concatenate([z[:DFT_COLS, ln] for ln in lanes], axis=0)
        z_im = jnp.concatenate([z[DFT_COLS:, ln] for ln in lanes], axis=0)
        y = _dot(jnp.concatenate([z_re, z_im], axis=1).astype(BF16), cs_ref[...]) * scale
        for k in range(kb):
            o_ref[0, :, lanes[k]] = y[k * DFT_COLS:(k + 1) * DFT_COLS].astype(o_ref.dtype)


def _factored_dft(xf, tabs):
    b, t, _ = xf.shape
    r = t // DFT_COLS
    cb, kb = 8, 8
    rows = pl.pallas_call(
        _dft_rows_kernel,
        grid=(b, DFT_COLS // cb),
        in_specs=[pl.BlockSpec((1, r, cb * D_FOURIER), lambda bb, j: (bb, 0, j)),
                  _const_spec(tabs["f_rows"].shape),
                  pl.BlockSpec((cb, r, HEAD_PAD), lambda bb, j: (j, 0, 0)),
                  pl.BlockSpec((cb, r, HEAD_PAD), lambda bb, j: (j, 0, 0))],
        out_specs=pl.BlockSpec((1, 2, cb, r, D_FOURIER), lambda bb, j: (bb, 0, j, 0, 0)),
        out_shape=jax.ShapeDtypeStruct((b, 2, DFT_COLS, r, D_FOURIER), BF16),
        compiler_params=_cparams(("parallel", "parallel")),
        name="dft_rows",
    )(xf.reshape(b, r, DFT_COLS * D_FOURIER), tabs["f_rows"], tabs["tw_cos"], tabs["tw_sin"])
    out = pl.pallas_call(
        functools.partial(_dft_cols_kernel, scale=1.0 / math.sqrt(t * D_FG)),
        grid=(b, r // kb),
        in_specs=[pl.BlockSpec((1, 2, DFT_COLS, kb * D_FOURIER), lambda bb, j: (bb, 0, 0, j)),
                  _const_spec(tabs["m_cols"].shape),
                  _const_spec(tabs["cs_chan"].shape)],
        out_specs=pl.BlockSpec((1, DFT_COLS, kb * D_FOURIER), lambda bb, j: (bb, 0, j)),
        out_shape=jax.ShapeDtypeStruct((b, DFT_COLS, r * D_FOURIER), BF16),
        compiler_params=_cparams(("parallel", "parallel")),
        name="dft_cols",
    )(rows.reshape(b, 2, DFT_COLS, r * D_FOURIER), tabs["m_cols"], tabs["cs_chan"])
    return out.reshape(b, t, D_FOURIER)


def _cos_sin(num, den):
    ang = (num % den).astype(F32) * (2.0 * math.pi / den)
    return jnp.cos(ang), jnp.sin(ang)


def _factored_dft_tables(t):
    r = t // DFT_COLS
    i_r = jnp.arange(r, dtype=jnp.int32)
    i_c = jnp.arange(DFT_COLS, dtype=jnp.int32)
    i_g = jnp.arange(D_FG, dtype=jnp.int32)
    c_r, s_r = _cos_sin(i_r[:, None] * i_r[None, :], r)
    c_t, s_t = _cos_sin(i_c[:, None] * i_r[None, :], t)
    c_c, s_c = _cos_sin(i_c[:, None] * i_c[None, :], DFT_COLS)
    c_g, s_g = _cos_sin(i_g[:, None] * i_g[None, :], D_FG)
    lanes = lambda a: jnp.broadcast_to(a[:, :, None], (DFT_COLS, r, HEAD_PAD))
    return {
        "f_rows": jnp.concatenate([c_r, -s_r], axis=0).astype(BF16),
        "tw_cos": lanes(c_t), "tw_sin": lanes(s_t),
        "m_cols": jnp.block([[c_c, s_c], [-s_c, c_c]]).astype(BF16),
        "cs_chan": jnp.concatenate([c_g, s_g], axis=0).astype(BF16),
    }


def _matmul_kernel(a_ref, b_ref, o_ref, acc_ref, *, scale):
    k = pl.program_id(2)

    @pl.when(k == 0)
    def _():
        acc_ref[...] = jnp.zeros_like(acc_ref)

    acc_ref[...] += _dot(a_ref[...], b_ref[...])

    @pl.when(k == pl.num_programs(2) - 1)
    def _():
        o_ref[...] = (acc_ref[...] * scale).astype(o_ref.dtype)


def _matmul(a, b, scale, out_dtype):
    m, kk = a.shape
    n = b.shape[1]
    bm, bn, bk = _tile(m, 1024), _tile(n, 1024), _tile(kk, 2048)
    return pl.pallas_call(
        functools.partial(_matmul_kernel, scale=scale),
        grid=(m // bm, n // bn, kk // bk),
        in_specs=[pl.BlockSpec((bm, bk), lambda i, j, k: (i, k)),
                  pl.BlockSpec((bk, bn), lambda i, j, k: (k, j))],
        out_specs=pl.BlockSpec((bm, bn), lambda i, j, k: (i, j)),
        out_shape=jax.ShapeDtypeStruct((m, n), out_dtype),
        scratch_shapes=[pltpu.VMEM((bm, bn), F32)],
        compiler_params=_cparams(("parallel", "parallel", "arbitrary")),
        name="dft_matmul",
    )(a, b)


def _even_out_kernel(h_ref, mod_ref, g_ref, gb_ref, z_ref, zp_ref, zn_ref, yf_ref, cw_ref, w_ref, o_ref):
    i = pl.program_id(1)
    x = h_ref[0]
    mod = mod_ref[0]
    g = g_ref[...]
    z = z_ref[0].astype(F32)
    tm = z.shape[0]
    halo = zp_ref.shape[1]
    prev_row = jnp.where(i > 0, zp_ref[0, halo - 1:halo, :].astype(F32), 0.0)
    next_row = jnp.where(i < pl.num_programs(1) - 1, zn_ref[0, 0:1, :].astype(F32), 0.0)
    row = lax.broadcasted_iota(jnp.int32, z.shape, 0)
    z_before = jnp.where(row == 0, prev_row, pltpu.roll(z, 1, 0))
    z_after = jnp.where(row == tm - 1, next_row, pltpu.roll(z, tm - 1, 0))
    cw = cw_ref[...]
    conv = z_before * cw[0:1] + z * cw[1:2] + z_after * cw[2:3]
    y_conv = (gb_ref[0].astype(F32) * conv).astype(BF16)
    y = _dot(y_conv, w_ref[:D_CONV, :]) + _dot(yf_ref[0], w_ref[D_CONV:, :])
    o_ref[0] = x + mod[5:6] * _rms_rows(y, g[3:4])


def _even_out(h, mod, g, gb, z, yf, conv_w, w_out):
    b, t, d = h.shape
    tm = _tile(t, 512)
    halo = 16
    nh = tm // halo
    last_halo = t // halo - 1
    return pl.pallas_call(
        _even_out_kernel,
        grid=(b, t // tm),
        in_specs=[pl.BlockSpec((1, tm, d), lambda bb, i: (bb, i, 0)),
                  _mod_spec(mod),
                  _const_spec(g.shape),
                  pl.BlockSpec((1, tm, D_CONV), lambda bb, i: (bb, i, 0)),
                  pl.BlockSpec((1, tm, D_CONV), lambda bb, i: (bb, i, 0)),
                  pl.BlockSpec((1, halo, D_CONV), lambda bb, i: (bb, jnp.maximum(i * nh - 1, 0), 0)),
                  pl.BlockSpec((1, halo, D_CONV), lambda bb, i: (bb, jnp.minimum((i + 1) * nh, last_halo), 0)),
                  pl.BlockSpec((1, tm, D_FOURIER), lambda bb, i: (bb, i, 0)),
                  _const_spec(conv_w.shape),
                  _const_spec(w_out.shape)],
        out_specs=pl.BlockSpec((1, tm, d), lambda bb, i: (bb, i, 0)),
        out_shape=jax.ShapeDtypeStruct(h.shape, F32),
        compiler_params=_cparams(("parallel", "parallel")),
        name="even_out",
    )(h, mod, g, gb, z, z, z, yf, conv_w, w_out)


def _dft_tables(t):
    n = jnp.arange(t, dtype=jnp.int32)
    ang = ((n[:, None] * n[None, :]) % t).astype(F32) * (2.0 * math.pi / t)
    w_pos = jnp.concatenate([jnp.cos(ang), -jnp.sin(ang)], axis=1).astype(BF16)
    return w_pos


def _channel_dft_table():
    n = jnp.arange(D_FG, dtype=jnp.int32)
    ang = ((n[:, None] * n[None, :]) % D_FG).astype(F32) * (2.0 * math.pi / D_FG)
    return jnp.concatenate([jnp.cos(ang), jnp.sin(ang)], axis=1).astype(BF16)


def _even_mixer(h, mod, g, w_in, conv_w, w_out, dft_c, dft_pos):
    b, t, _ = h.shape
    if isinstance(dft_pos, dict):
        gb, z, xf = _even_in(h, mod, g, w_in, dft_c, False)
        yf = _factored_dft(xf, dft_pos)
    else:
        gb, z, a = _even_in(h, mod, g, w_in, dft_c, True)
        yf = _matmul(dft_pos, a.reshape(2 * t, b * D_FOURIER), 1.0 / math.sqrt(t * D_FG), BF16)
        yf = yf.reshape(t, b, D_FOURIER).transpose(1, 0, 2)
    return _even_out(h, mod, g, gb, z, yf, conv_w, w_out)


def _odd_in_kernel(h_ref, mod_ref, g_ref, wtok_ref, wt_ref, wukp_ref, wuvt_ref, wuqt_ref, place_ref, gsel_ref,
                   gkv_row_ref, gkv_col_ref, gq_col_ref,
                   cosk_ref, sink_ref, tkr_ref, cosq_ref, sinq_ref, cosm_ref, sinm_ref,
                   khd_ref, khm_ref, vtd_ref, vtm_ref, kn_ref, *q_refs, with_q, split8):
    xm = _modulated(h_ref[0], mod_ref[0], g_ref[...], 1).astype(BF16)
    tm = xm.shape[0]

    ut = _dot(xm, wtok_ref[...])
    k = ut[:, :DA_QCOLS]
    ckv = ut[:, DA_QCOLS:DA_QCOLS + MLA_KV_RANK]
    kr2 = ut[:, DA_QCOLS + MLA_KV_RANK:]
    lane = lax.broadcasted_iota(jnp.int32, k.shape, 1)
    first_half = (lane % (DA_DK // 2)) < (DA_DK // 4)
    k_sw = jnp.where(first_half, pltpu.roll(k, DA_QCOLS - DA_DK // 4, 1), pltpu.roll(k, DA_DK // 4, 1))
    reps = DA_QCOLS // HEAD_PAD
    cosk = jnp.concatenate([cosk_ref[...]] * reps, axis=1)
    sink = jnp.concatenate([sink_ref[...]] * reps, axis=1)
    k_rot = k * cosk + k_sw * sink
    ckvn = _rms_rows(ckv, gkv_row_ref[...]).astype(BF16)
    k_nope = _dot(ckvn, wukp_ref[...])
    pr = kr2 * tkr_ref[...]
    pr_hi = pr.astype(BF16)
    pr_lo = (pr - pr_hi.astype(F32)).astype(BF16)
    k_mla = k_nope + _dot(pr_hi, place_ref[...]) + _dot(pr_lo, place_ref[...])
    k_odd = pltpu.roll(k_rot, DA_QCOLS - DA_DK, 1)
    lane_h = lax.broadcasted_iota(jnp.int32, (tm, HEAD_PAD), 1)
    sq = []
    for mp in range(2 * DA_HEADS):
        src = (k_rot if mp % 2 == 0 else k_odd)[:, (mp // 2) * HEAD_PAD:(mp // 2 + 1) * HEAD_PAD]
        kd = jnp.where(lane_h < DA_DK, src, 0.0)
        if split8:
            k_hi, k_lo = _split8(kd * DA_SPLIT_SCALE)
            khd_ref[0, mp] = jnp.where(lane_h < DA_DK, k_hi, pltpu.roll(k_lo, DA_DK, 1)).astype(F8)
            kf = k_hi + k_lo
        else:
            khd_ref[0, mp] = jnp.where(lane_h == DA_DK, 1.0, kd).astype(BF16)
            kf = kd.astype(BF16).astype(F32)
        sq.append((kf * kf).astype(BF16))
    for hh in range(MLA_HEADS):
        kd = k_mla[:, hh * HEAD_PAD:(hh + 1) * HEAD_PAD]
        khm_ref[0, hh] = jnp.where(lane_h == MLA_DQK, 1.0, kd).astype(BF16)
        kf = kd.astype(BF16).astype(F32)
        sq.append((kf * kf).astype(BF16))
    kn_ref[0] = _dot_nt(gsel_ref[...], jnp.concatenate(sq, axis=1))

    r0 = DA_VCOLS
    vckv = _dot_nt(wt_ref[:r0 + MLA_KV_RANK, :], xm)
    for hh in range(DA_HEADS):
        vtd_ref[0, hh] = vckv[hh * DA_DV:(hh + 1) * DA_DV].astype(BF16)
    ckv_t = vckv[r0:]
    ms = jnp.mean(ckv_t * ckv_t, axis=0, keepdims=True)
    ckvn_t = (ckv_t * lax.rsqrt(ms + EPS) * gkv_col_ref[...]).astype(BF16)
    vm_t = _dot(wuvt_ref[...], ckvn_t)
    for hh in range(MLA_HEADS):
        vtm_ref[0, hh] = vm_t[hh * MLA_DV:(hh + 1) * MLA_DV].astype(BF16)

    if with_q:
        qtd_ref, qtm_ref = q_refs
        r1 = r0 + MLA_KV_RANK
        q_t = _dot_nt(wt_ref[r1:, :], xm)
        cosq, sinq = cosq_ref[...], sinq_ref[...]
        qd = DA_DK // 4
        for mp in range(2 * DA_HEADS):
            q = q_t[mp * DA_DK:(mp + 1) * DA_DK]
            q_sw = jnp.concatenate([q[qd:2 * qd], q[:qd], q[3 * qd:], q[2 * qd:3 * qd]], axis=0)
            q_rot = q * cosq + q_sw * sinq
            if split8:
                q_hi, q_lo = _split8(q_rot * DA_SPLIT_SCALE)
                qtd_ref[0, mp, :DA_DK, :] = q_hi.astype(F8)
                qtd_ref[0, mp, DA_DK:, :] = q_lo.astype(F8)
            else:
                qtd_ref[0, mp // 2, (mp % 2) * DA_DK:(mp % 2 + 1) * DA_DK, :] = (
                    q_rot * (DA_SCALE * LOG2E)).astype(BF16)
        cq_t = q_t[DA_QCOLS:]
        ms = jnp.mean(cq_t * cq_t, axis=0, keepdims=True)
        cqn_t = (cq_t * lax.rsqrt(ms + EPS) * gq_col_ref[...]).astype(BF16)
        qm_t = _dot(wuqt_ref[...], cqn_t) * (MLA_SCALE * LOG2E)
        cosm, sinm = cosm_ref[...], sinm_ref[...]
        rd = MLA_ROPE // 4
        for hh in range(MLA_HEADS):
            base = hh * HEAD_PAD
            qtm_ref[0, hh, :MLA_NOPE, :] = qm_t[base:base + MLA_NOPE].astype(BF16)
            r = qm_t[base + MLA_NOPE:base + MLA_DQK]
            r_sw = jnp.concatenate([r[rd:2 * rd], r[:rd], r[3 * rd:], r[2 * rd:3 * rd]], axis=0)
            qtm_ref[0, hh, MLA_NOPE:MLA_DQK, :] = (r * cosm + r_sw * sinm).astype(BF16)
            qtm_ref[0, hh, MLA_DQK:, :] = jnp.zeros((HEAD_PAD - MLA_DQK, tm), BF16)


def _odd_in(h, mod, g, wts, tabs, kv_prev, tok_off, with_q, split8):
    b, t, d = h.shape
    da_dtype = F8 if split8 else BF16
    tk_total = tabs["tk_total"]
    tm = _tile(t, 512)
    assert tok_off % tm == 0
    off = tok_off // tm
    row_tab = lambda w: pl.BlockSpec((tm, w), lambda bb, i: (i, 0))
    col_tab = lambda r: pl.BlockSpec((r, tm), lambda bb, i: (0, i))
    consts = [wts["wtok"], wts["wt"], wts["wukp"], wts["wuvt"], wts["wuqt"], wts["place"], wts["gsel"],
              wts["gkv_row"], wts["gkv_col"], wts["gq_col"]]
    in_specs = ([pl.BlockSpec((1, tm, d), lambda bb, i: (bb, i, 0)), _mod_spec(mod), _const_spec(g.shape)]
                + [_const_spec(c.shape) for c in consts]
                + [row_tab(HEAD_PAD), row_tab(HEAD_PAD), row_tab(2 * MLA_ROPE),
                   col_tab(DA_DK), col_tab(DA_DK), col_tab(MLA_ROPE), col_tab(MLA_ROPE)])
    args = [h, mod, g] + consts + [tabs["cosk"], tabs["sink"], tabs["tkr"],
                                   tabs["cosq"], tabs["sinq"], tabs["cosm"], tabs["sinm"]]
    out_specs = [pl.BlockSpec((1, 2 * DA_HEADS, tm, HEAD_PAD), lambda bb, i: (bb, 0, i + off, 0)),
                 pl.BlockSpec((1, MLA_HEADS, tm, HEAD_PAD), lambda bb, i: (bb, 0, i + off, 0)),
                 pl.BlockSpec((1, DA_HEADS, DA_DV, tm), lambda bb, i: (bb, 0, 0, i + off)),
                 pl.BlockSpec((1, MLA_HEADS, MLA_DV, tm), lambda bb, i: (bb, 0, 0, i + off)),
                 pl.BlockSpec((1, N_MAPS, tm), lambda bb, i: (bb, 0, i + off))]
    out_shape = [jax.ShapeDtypeStruct((b, 2 * DA_HEADS, tk_total, HEAD_PAD), da_dtype),
                 jax.ShapeDtypeStruct((b, MLA_HEADS, tk_total, HEAD_PAD), BF16),
                 jax.ShapeDtypeStruct((b, DA_HEADS, DA_DV, tk_total), BF16),
                 jax.ShapeDtypeStruct((b, MLA_HEADS, MLA_DV, tk_total), BF16),
                 jax.ShapeDtypeStruct((b, N_MAPS, tk_total), F32)]
    if with_q:
        n_qd = 2 * DA_HEADS if split8 else DA_HEADS
        out_specs += [pl.BlockSpec((1, n_qd, HEAD_PAD, tm), lambda bb, i: (bb, 0, 0, i)),
                      pl.BlockSpec((1, MLA_HEADS, HEAD_PAD, tm), lambda bb, i: (bb, 0, 0, i))]
        out_shape += [jax.ShapeDtypeStruct((b, n_qd, HEAD_PAD, t), da_dtype),
                      jax.ShapeDtypeStruct((b, MLA_HEADS, HEAD_PAD, t), BF16)]
    aliases = {}
    kernel_fn = functools.partial(_odd_in_kernel, with_q=with_q, split8=split8)
    if kv_prev is not None:
        n_in = len(args)
        in_specs += [pl.BlockSpec(memory_space=pl.ANY)] * N_KV_ARRAYS
        args += list(kv_prev)
        aliases = {n_in + a: a for a in range(N_KV_ARRAYS)}
        kernel_fn = functools.partial(_odd_in_alias_kernel, n_in=n_in, with_q=with_q, split8=split8)
    return pl.pallas_call(
        kernel_fn,
        grid=(b, t // tm),
        in_specs=in_specs,
        out_specs=out_specs,
        out_shape=out_shape,
        input_output_aliases=aliases,
        compiler_params=_cparams(("parallel", "parallel")),
        name="odd_in",
    )(*args)


def _odd_in_alias_kernel(*refs, n_in, with_q, split8):
    _odd_in_kernel(*refs[:n_in], *refs[n_in + N_KV_ARRAYS:], with_q=with_q, split8=split8)


def _attn_kernel(lam_ref, gsub_ref, kn_ref, qtd_ref, qtm_ref, khd_ref, khm_ref, vtd_ref, vtm_ref,
                 otd_ref, otm_ref, lmin_ref, qad_ref, qam_ref, shift_ref, m_ref, l_ref, accd_ref, accm_ref,
                 *, lam_init, safe):
    j = pl.program_id(2)
    tq = qtd_ref.shape[-1]

    @pl.when(j == 0)
    def _():
        m_ref[...] = jnp.full_like(m_ref, NEG_BIG)
        l_ref[...] = jnp.zeros_like(l_ref)
        accd_ref[...] = jnp.zeros_like(accd_ref)
        accm_ref[...] = jnp.zeros_like(accm_ref)
        kmax = jnp.sqrt(jnp.max(kn_ref[0], axis=1, keepdims=True))
        shift_row = lax.broadcasted_iota(jnp.int32, (16, tq), 0) == 0
        for idx in range(2 * DA_HEADS):
            if safe:
                q = qtd_ref[0, idx // 2, (idx % 2) * DA_DK:(idx % 2 + 1) * DA_DK, :]
                qad_ref[idx] = jnp.concatenate([q, jnp.zeros((HEAD_PAD - DA_DK, tq), BF16)], axis=0)
            else:
                q_hi, q_lo = qtd_ref[0, idx, :DA_DK, :], qtd_ref[0, idx, DA_DK:, :]
                qad_ref[idx] = jnp.concatenate([q_hi, q_hi, q_lo, q_lo], axis=0)
                qf = q_hi.astype(F32) + q_lo.astype(F32)
                qn = jnp.sqrt(jnp.sum(qf * qf, axis=0, keepdims=True))
                ok = (qn < F8_SAFE_MAX) & (kmax[idx:idx + 1] < F8_SAFE_MAX)
                shift_ref[idx] = jnp.where(ok, (BOUND_SLACK * kmax[idx:idx + 1]) * qn, jnp.inf)
        for hh in range(MLA_HEADS):
            idx = 2 * DA_HEADS + hh
            q = qtm_ref[0, hh, :MLA_DQK, :]
            if safe:
                shift_blk = jnp.zeros((16, tq), BF16)
            else:
                qf = q.astype(F32)
                qn = jnp.sqrt(jnp.sum(qf * qf, axis=0, keepdims=True))
                shift = -(BOUND_SLACK * kmax[idx:idx + 1]) * qn
                shift_blk = jnp.where(shift_row, shift, 0.0).astype(BF16)
            pad = jnp.zeros((HEAD_PAD - MLA_DQK - 16, tq), BF16)
            qam_ref[hh] = jnp.concatenate([q, shift_blk, pad], axis=0)

    def softmax_pv(s, ln, idx, v_t, acc_ref, acc_idx):
        if safe:
            m_old = m_ref[idx, :, ln]
            m_new = jnp.maximum(m_old, jnp.max(s, axis=0, keepdims=True))
            alpha = jnp.exp2(m_old - m_new)
            m_ref[idx, :, ln] = m_new
            p = jnp.exp2(s - m_new)
            l_ref[idx, :, ln] = alpha * l_ref[idx, :, ln] + jnp.sum(p.reshape(-1, 8, s.shape[1]), axis=0)
            acc_ref[acc_idx, :, ln] = acc_ref[acc_idx, :, ln] * alpha + _dot(v_t, p.astype(BF16))
        else:
            p = jnp.exp2(s)
            l_ref[idx, :, ln] += jnp.sum(p.reshape(-1, 8, s.shape[1]), axis=0)
            acc_ref[acc_idx, :, ln] += _dot(v_t, p.astype(BF16))

    chunk = ATTN_Q_CHUNK if (not safe and tq % ATTN_Q_CHUNK == 0) else tq
    q_chunks = [slice(c, c + chunk) for c in range(0, tq, chunk)]

    def da_score(hh, m, ln):
        keys = khd_ref[0, 2 * hh + m]
        if safe:
            return _dot(keys, qad_ref[2 * hh + m, :, ln])
        return _dot(jnp.concatenate([keys, keys], axis=1), qad_ref[2 * hh + m, :, ln])

    def mla_score(hh, ln):
        return _dot(khm_ref[0, hh], qam_ref[hh, :, ln])

    def run_maps(scores, consume):
        s_next = scores[0]()
        for n in range(len(scores)):
            s_cur = s_next
            if n + 1 < len(scores):
                s_next = scores[n + 1]()
            consume[n](s_cur)

    def da_body(it, carry):
        scores, consume = [], []
        for u in range(DA_HEADS_PER_ITER):
            hh = it * DA_HEADS_PER_ITER + u
            for m in range(2):
                for ln in q_chunks:
                    scores.append(functools.partial(da_score, hh, m, ln))
                    consume.append(lambda s, hh=hh, m=m, ln=ln: softmax_pv(
                        s if safe else s - shift_ref[2 * hh + m, :, ln], ln, 2 * hh + m, vtd_ref[0, hh],
                        accd_ref, 2 * hh + m))
        run_maps(scores, consume)
        return carry

    def mla_body(it, carry):
        scores, consume = [], []
        for u in range(MLA_HEADS_PER_ITER):
            hh = it * MLA_HEADS_PER_ITER + u
            for ln in [slice(0, tq)]:
                scores.append(functools.partial(mla_score, hh, ln))
                consume.append(lambda s, hh=hh, ln=ln: softmax_pv(s, ln, 2 * DA_HEADS + hh, vtm_ref[0, hh],
                                                                  accm_ref, hh))
        run_maps(scores, consume)
        return carry

    lax.fori_loop(0, DA_HEADS // DA_HEADS_PER_ITER, da_body, 0)
    lax.fori_loop(0, MLA_HEADS // MLA_HEADS_PER_ITER, mla_body, 0)

    @pl.when(j == pl.num_programs(2) - 1)
    def _():
        lp = lam_ref[...]
        lam = (jnp.exp(jnp.sum(lp[0:1] * lp[1:2], axis=1, keepdims=True))
               - jnp.exp(jnp.sum(lp[2:3] * lp[3:4], axis=1, keepdims=True)) + lam_init)
        gsub = gsub_ref[...]
        lsum = [jnp.sum(l_ref[idx], axis=0, keepdims=True) for idx in range(N_MAPS)]
        linv = [1.0 / ls for ls in lsum]
        for hh in range(DA_HEADS):
            o = accd_ref[2 * hh] * linv[2 * hh] - accd_ref[2 * hh + 1] * (lam * linv[2 * hh + 1])
            ms = jnp.mean(o * o, axis=0, keepdims=True)
            otd_ref[0, hh] = (o * (lax.rsqrt(ms + EPS) * (1.0 - lam_init)) * gsub).astype(otd_ref.dtype)
        for hh in range(MLA_HEADS):
            otm_ref[0, hh] = (accm_ref[hh] * linv[2 * DA_HEADS + hh]).astype(otm_ref.dtype)
        lmin = lsum[0]
        for idx in range(1, N_MAPS):
            lmin = jnp.minimum(lmin, lsum[idx])
        lmin_ref[0, 0] = jnp.broadcast_to(jnp.min(lmin, axis=1, keepdims=True), lmin_ref.shape[2:])


def _attention_call(lam_p, gsub_col, qtd, qtm, kv, lam_init, tq, tk, kv_off, n_kv, safe):
    khd, khm, vtd, vtm, kn = kv
    b, n_qd, _, t = qtd.shape
    nq = t // tq
    assert kv_off % n_kv == 0
    qad_scratch = (pltpu.VMEM((2 * DA_HEADS, HEAD_PAD, tq), BF16) if safe
                   else pltpu.VMEM((2 * DA_HEADS, 4 * DA_DK, tq), F8))
    return pl.pallas_call(
        functools.partial(_attn_kernel, lam_init=lam_init, safe=safe),
        grid=(b, nq, n_kv),
        in_specs=[_const_spec(lam_p.shape),
                  _const_spec(gsub_col.shape),
                  pl.BlockSpec((1, N_MAPS, n_kv * tk), lambda bb, i, j: (bb, 0, kv_off // n_kv)),
                  pl.BlockSpec((1, n_qd, HEAD_PAD, tq), lambda bb, i, j: (bb, 0, 0, i)),
                  pl.BlockSpec((1, MLA_HEADS, HEAD_PAD, tq), lambda bb, i, j: (bb, 0, 0, i)),
                  pl.BlockSpec((1, 2 * DA_HEADS, tk, HEAD_PAD), lambda bb, i, j: (bb, 0, kv_off + j, 0)),
                  pl.BlockSpec((1, MLA_HEADS, tk, HEAD_PAD), lambda bb, i, j: (bb, 0, kv_off + j, 0)),
                  pl.BlockSpec((1, DA_HEADS, DA_DV, tk), lambda bb, i, j: (bb, 0, 0, kv_off + j)),
                  pl.BlockSpec((1, MLA_HEADS, MLA_DV, tk), lambda bb, i, j: (bb, 0, 0, kv_off + j))],
        out_specs=[pl.BlockSpec((1, DA_HEADS, DA_DV, tq), lambda bb, i, j: (bb, 0, 0, i)),
                   pl.BlockSpec((1, MLA_HEADS, MLA_DV, tq), lambda bb, i, j: (bb, 0, 0, i)),
                   pl.BlockSpec((1, 1, 8, 128), lambda bb, i, j: (bb, i, 0, 0))],
        out_shape=[jax.ShapeDtypeStruct((b, DA_HEADS, DA_DV, t), BF16),
                   jax.ShapeDtypeStruct((b, MLA_HEADS, MLA_DV, t), BF16),
                   jax.ShapeDtypeStruct((b, nq, 8, 128), F32)],
        scratch_shapes=[qad_scratch,
                        pltpu.VMEM((MLA_HEADS, HEAD_PAD, tq), BF16),
                        pltpu.VMEM((2 * DA_HEADS, 1, tq), F32),
                        pltpu.VMEM((N_MAPS, 1, tq), F32),
                        pltpu.VMEM((N_MAPS, 8, tq), F32),
                        pltpu.VMEM((2 * DA_HEADS, DA_DV, tq), F32),
                        pltpu.VMEM((MLA_HEADS, MLA_DV, tq), F32)],
        compiler_params=_cparams(("parallel", "parallel", "arbitrary")),
        name="attention_safe" if safe else "attention",
    )(lam_p, gsub_col, kn, qtd, qtm, khd, khm, vtd, vtm)


def _odd_out_kernel(h_ref, mod_ref, g_ref, otd_ref, otm_ref, wt_ref, o_ref):
    y_t = _dot(wt_ref[:, :DA_VCOLS], otd_ref[0]) + _dot(wt_ref[:, DA_VCOLS:], otm_ref[0])
    y = y_t.T
    o_ref[0] = h_ref[0] + mod_ref[0][5:6] * _rms_rows(y, g_ref[3:4])


def _odd_out(h, mod, g, otd, otm, w_out_t):
    b, t, d = h.shape
    tm = _tile(t, 512)
    otd = otd.reshape(b, DA_VCOLS, t)
    otm = otm.reshape(b, MLA_HEADS * MLA_DV, t)
    return pl.pallas_call(
        _odd_out_kernel,
        grid=(b, t // tm),
        in_specs=[pl.BlockSpec((1, tm, d), lambda bb, i: (bb, i, 0)),
                  _mod_spec(mod),
                  _const_spec(g.shape),
                  pl.BlockSpec((1, DA_VCOLS, tm), lambda bb, i: (bb, 0, i)),
                  pl.BlockSpec((1, MLA_HEADS * MLA_DV, tm), lambda bb, i: (bb, 0, i)),
                  _const_spec(w_out_t.shape)],
        out_specs=pl.BlockSpec((1, tm, d), lambda bb, i: (bb, i, 0)),
        out_shape=jax.ShapeDtypeStruct(h.shape, F32),
        compiler_params=_cparams(("parallel", "parallel")),
        name="odd_out",
    )(h, mod, g, otd, otm, w_out_t)


def _rope_angles(n_tokens, rot_dim):
    rows = n_tokens // GRID_W
    row = jnp.broadcast_to(jnp.arange(rows)[:, None], (rows, GRID_W)).reshape(-1).astype(F32)
    col = jnp.broadcast_to(jnp.arange(GRID_W)[None, :], (rows, GRID_W)).reshape(-1).astype(F32)
    n_freq = rot_dim // 4
    freqs = ROPE_THETA ** (-jnp.arange(n_freq, dtype=F32) / n_freq)
    return row[:, None] * freqs, col[:, None] * freqs


def _rope_cos_sin(n_tokens, rot_dim, identity):
    if identity:
        return jnp.ones((n_tokens, rot_dim), F32), jnp.zeros((n_tokens, rot_dim), F32)
    ar, ac = _rope_angles(n_tokens, rot_dim)
    cos = jnp.concatenate([jnp.cos(ar), jnp.cos(ar), jnp.cos(ac), jnp.cos(ac)], axis=1)
    sin = jnp.concatenate([-jnp.sin(ar), jnp.sin(ar), -jnp.sin(ac), jnp.sin(ac)], axis=1)
    return cos, sin


def _rope_tables(n_tokens, identity, tk_total):
    cd, sd = _rope_cos_sin(n_tokens, DA_DK, identity)
    cm, sm = _rope_cos_sin(n_tokens, MLA_ROPE, identity)
    return {
        "cosk": jnp.concatenate([cd, cd], axis=1), "sink": jnp.concatenate([sd, sd], axis=1),
        "tkr": jnp.concatenate([cm, sm], axis=1),
        "cosq": cd.T, "sinq": sd.T, "cosm": cm.T, "sinm": sm.T,
        "tk_total": tk_total,
    }


def _swap_perm(rot_dim):
    q = rot_dim // 4
    return jnp.concatenate([jnp.arange(q, 2 * q), jnp.arange(0, q), jnp.arange(3 * q, 4 * q), jnp.arange(2 * q, 3 * q)])


def _odd_weights(w_in, g_q, w_uq, g_kv, w_uk, w_uv):
    w_q = w_in[:, :DA_QCOLS]
    w_cq = w_in[:, DA_QCOLS:Q_COLS]
    w_k = w_in[:, Q_COLS:Q_COLS + DA_QCOLS]
    w_v = w_in[:, Q_COLS + DA_QCOLS:Q_COLS + DA_QCOLS + DA_VCOLS]
    w_ckv = w_in[:, Q_COLS + DA_QCOLS + DA_VCOLS:Q_COLS + DA_QCOLS + DA_VCOLS + MLA_KV_RANK]
    w_kr = w_in[:, Q_COLS + DA_QCOLS + DA_VCOLS + MLA_KV_RANK:]
    wtok = jnp.concatenate([w_k, w_ckv, w_kr, w_kr[:, _swap_perm(MLA_ROPE)]], axis=1).astype(BF16)
    wt = jnp.concatenate([w_v, w_ckv, w_q, w_cq], axis=1).T.astype(BF16)
    pad_k = jnp.zeros((MLA_KV_RANK, MLA_HEADS, HEAD_PAD), F32)
    wukp = pad_k.at[:, :, :MLA_NOPE].set(w_uk.reshape(MLA_KV_RANK, MLA_HEADS, MLA_NOPE))
    wukp = wukp.reshape(MLA_KV_RANK, MLA_HEADS * HEAD_PAD).astype(BF16)
    pad_q = jnp.zeros((MLA_Q_RANK, MLA_HEADS, HEAD_PAD), F32)
    wuqp = pad_q.at[:, :, :MLA_DQK].set(w_uq.reshape(MLA_Q_RANK, MLA_HEADS, MLA_DQK))
    wuqt = wuqp.reshape(MLA_Q_RANK, MLA_HEADS * HEAD_PAD).T.astype(BF16)
    eye = jnp.eye(MLA_ROPE, dtype=F32)
    place = jnp.zeros((2, MLA_ROPE, MLA_HEADS, HEAD_PAD), F32)
    place = place.at[:, :, :, MLA_NOPE:MLA_DQK].set(jnp.broadcast_to(eye[None, :, None, :], (2, MLA_ROPE, MLA_HEADS, MLA_ROPE)))
    place = place.reshape(2 * MLA_ROPE, MLA_HEADS * HEAD_PAD).astype(BF16)
    return {
        "wtok": wtok, "wt": wt, "wukp": wukp, "wuvt": w_uv.T.astype(BF16), "wuqt": wuqt, "place": place,
        "gsel": jnp.kron(jnp.eye(N_MAPS, dtype=F32), jnp.ones((1, HEAD_PAD), F32)).astype(BF16),
        "gkv_row": g_kv.reshape(1, -1).astype(F32), "gkv_col": g_kv.reshape(-1, 1).astype(F32),
        "gq_col": g_q.reshape(-1, 1).astype(F32),
    }


def kernel(x, c, ctx, c_ctx, w_mod, b_mod, norm_g, w_ffn_in, w_ffn_out, w_in_even, conv_w, w_out_even,
           w_in_odd, g_q_mla, w_uq, g_kv_mla, w_uk, w_uv, lam_q1, lam_k1, lam_q2, lam_k2, g_subln, w_out_odd):
    b, t, d = x.shape
    tc = ctx.shape[1]
    depth = w_mod.shape[0]
    tk_total = t + tc

    rows = -(-(b + 1) // 8) * 8
    cond = jnp.zeros((rows, d), F32).at[:b].set(c).at[b].set(c_ctx)
    mod = _adaln(cond, w_mod, b_mod).reshape(depth, rows, N_MOD, d)

    dft_c = _channel_dft_table()
    factored = lambda n: n % (DFT_COLS * 16) == 0
    w_pos_x = _factored_dft_tables(t) if factored(t) else _dft_tables(t)
    w_pos_c = _factored_dft_tables(tc) if factored(tc) else _dft_tables(tc)
    tabs_x = _rope_tables(t, False, tk_total)
    tabs_c = _rope_tables(tc, True, tk_total)
    tq = _tile(t, 1024)
    tk = 768 if tk_total % 768 == 0 else _tile(tk_total, 512)
    assert t % tc == 0 and tk_total % tk == 0

    h, hc = x, ctx
    for l in range(depth):
        last = l == depth - 1
        odd = l % 2 == 1
        ctx_live = (not last) or odd
        g = norm_g[l]
        m_x, m_c = mod[l, :b], mod[l, b:b + 1]
        wi0, wo0 = w_ffn_in[l, 0].astype(BF16), w_ffn_out[l, 0].astype(BF16)
        h = _ffn(h, m_x, g, wi0, wo0, 0)
        if ctx_live:
            hc = _ffn(hc, m_c, g, wi0, wo0, 0)

        if not odd:
            e = l // 2
            w_in, w_out = w_in_even[e].astype(BF16), w_out_even[e].astype(BF16)
            h = _even_mixer(h, m_x, g, w_in, conv_w[e], w_out, dft_c, w_pos_x)
            if ctx_live:
                hc = _even_mixer(hc, m_c, g, w_in, conv_w[e], w_out, dft_c, w_pos_c)
        else:
            o = l // 2
            lam_init = 0.8 - 0.6 * math.exp(-0.3 * l)
            wts = _odd_weights(w_in_odd[o], g_q_mla[o], w_uq[o], g_kv_mla[o], w_uk[o], w_uv[o])
            lam_p = jnp.stack([lam_q1[o], lam_k1[o], lam_q2[o], lam_k2[o]]).astype(F32)
            gsub_col = g_subln[o].reshape(-1, 1).astype(F32)
            w_out_t = w_out_odd[o].T.astype(BF16)
            def latent_attention(safe, h=h, hc=hc, m_x=m_x, m_c=m_c, g=g, wts=wts, lam_p=lam_p,
                                 gsub_col=gsub_col, lam_init=lam_init):
                outs_x = _odd_in(h, m_x, g, wts, tabs_x, None, 0, True, not safe)
                kv = _odd_in(hc, m_c, g, wts, tabs_c, outs_x[:N_KV_ARRAYS], t, False, not safe)
                return _attention_call(lam_p, gsub_col, *outs_x[N_KV_ARRAYS:], kv, lam_init, tq, tk, 0,
                                       tk_total // tk, safe)

            otd, otm, lmin = latent_attention(False)
            otd, otm = lax.cond(jnp.min(lmin) >= MIN_DENOMINATOR, lambda: (otd, otm),
                                lambda: tuple(latent_attention(True)[:2]))
            if not last:
                outs_c = _odd_in(hc, m_c, g, wts, dict(tabs_c, tk_total=tc), None, 0, True, False)
                ocd, ocm = _attention_call(lam_p, gsub_col, *outs_c[N_KV_ARRAYS:], outs_c[:N_KV_ARRAYS],
                                           lam_init, tc, tc, 0, 1, True)[:2]
                hc = _odd_out(hc, m_c, g, ocd, ocm, w_out_t)
            h = _odd_out(h, m_x, g, otd, otm, w_out_t)

        wi1, wo1 = w_ffn_in[l, 1].astype(BF16), w_ffn_out[l, 1].astype(BF16)
        h = _ffn(h, m_x, g, wi1, wo1, 2)
        if not last:
            hc = _ffn(hc, m_c, g, wi1, wo1, 2)
    return h
```

```python
import functools
import math

import jax
import jax.numpy as jnp
from jax import lax
from jax.experimental import pallas as pl
from jax.experimental.pallas import tpu as pltpu

F32 = jnp.float32
BF16 = jnp.bfloat16

D_MODEL = 1024
GRID_W = 64
N_MOD = 9
FFN_RES = 0.5
EPS = 1e-6
ROPE_THETA = 10000.0

D_CONV = 512
D_FOURIER = 512
FOURIER_GROUPS = 4
D_FG = D_FOURIER // FOURIER_GROUPS

DA_HEADS = 8
DA_DK = 64
DA_DV = 128
DA_SCALE = DA_DK ** -0.5
MLA_HEADS = 8
MLA_NOPE = 64
MLA_ROPE = 32
MLA_DQK = MLA_NOPE + MLA_ROPE
MLA_DV = 64
MLA_Q_RANK = 384
MLA_KV_RANK = 256
MLA_SCALE = MLA_DQK ** -0.5
DA_QCOLS = DA_HEADS * 2 * DA_DK
DA_VCOLS = DA_HEADS * DA_DV
Q_COLS = DA_QCOLS + MLA_Q_RANK
HEAD_PAD = 128
N_MAPS = 2 * DA_HEADS + MLA_HEADS
LOG2E = 1.4426950408889634
NEG_BIG = -1e30
N_KV_ARRAYS = 5
F8 = jnp.float8_e4m3fn
F8_SAFE_MAX = 400.0
DA_SPLIT_SCALE = math.sqrt(DA_SCALE * LOG2E)
FFN_ROW_SPLITS = 2
FFN_GROUP_ROWS = 256
DA_HEADS_PER_ITER = 4
MLA_HEADS_PER_ITER = 8
ATTN_Q_CHUNK = 256
BOUND_SLACK = 1.02
MIN_DENOMINATOR = 2.0 ** -40

VMEM_LIMIT_V7X = 56 * 1024 * 1024


def _cparams(sem):
    return pltpu.CompilerParams(dimension_semantics=sem, vmem_limit_bytes=VMEM_LIMIT_V7X)


def _tile(n, pref):
    if n <= pref:
        return n
    t = pref - pref % 128
    while t >= 128:
        if n % t == 0:
            return t
        t -= 128
    return n


def _const_spec(shape):
    nd = len(shape)
    return pl.BlockSpec(shape, lambda *_: (0,) * nd, pipeline_mode=pl.Buffered(1))


def _mod_spec(mod):
    if mod.shape[0] == 1:
        return pl.BlockSpec((1, N_MOD, D_MODEL), lambda b, *_: (0, 0, 0))
    return pl.BlockSpec((1, N_MOD, D_MODEL), lambda b, *_: (b, 0, 0))


def _rms_rows(x, g):
    ms = jnp.mean(x * x, axis=-1, keepdims=True)
    return x * lax.rsqrt(ms + EPS) * g


def _modulated(x, mod, g, slot):
    shift = mod[3 * slot:3 * slot + 1]
    scale = mod[3 * slot + 1:3 * slot + 2]
    return _rms_rows(x, g[2 * slot:2 * slot + 1]) * (1.0 + scale) + shift


def _split8(x):
    hi = x.astype(F8).astype(F32)
    lo = (x - hi).astype(F8).astype(F32)
    return hi, lo


def _dot(a, b):
    return jnp.dot(a, b, preferred_element_type=F32)


def _dot_nt(a, b):
    return lax.dot_general(a, b, (((1,), (1,)), ((), ())), preferred_element_type=F32)


def _adaln_kernel(c_ref, w_ref, b_ref, o_ref):
    c = c_ref[...]
    a = c * jax.nn.sigmoid(c)
    o_ref[0] = jnp.dot(a, w_ref[0], preferred_element_type=F32,
                       precision=lax.Precision.HIGHEST) + b_ref[0]


def _adaln(cond, w_mod, b_mod):
    depth, d, n = w_mod.shape
    rows = cond.shape[0]
    tn = _tile(n, 1152)
    return pl.pallas_call(
        _adaln_kernel,
        grid=(depth, n // tn),
        in_specs=[pl.BlockSpec((rows, d), lambda l, j: (0, 0)),
                  pl.BlockSpec((1, d, tn), lambda l, j: (l, 0, j)),
                  pl.BlockSpec((1, 1, tn), lambda l, j: (l, 0, j))],
        out_specs=pl.BlockSpec((1, rows, tn), lambda l, j: (l, 0, j)),
        out_shape=jax.ShapeDtypeStruct((depth, rows, n), F32),
        compiler_params=_cparams(("parallel", "parallel")),
        name="adaln",
    )(cond, w_mod, b_mod.reshape(depth, 1, n))


def _ffn_kernel(h_ref, mod_ref, g_ref, win_ref, wout_ref, o_ref, *, slot, d_ff):
    mod = mod_ref[0]
    g = g_ref[...]
    res_gate = mod[3 * slot + 2:3 * slot + 3]
    tm = h_ref.shape[1]
    n_groups = max(1, tm // FFN_GROUP_ROWS)
    rows = tm // n_groups
    for part in range(n_groups):
        x = h_ref[0, part * rows:(part + 1) * rows, :]
        xm = _modulated(x, mod, g, slot).astype(BF16)
        gate = _dot(xm, win_ref[:, :d_ff])
        up = _dot(xm, win_ref[:, d_ff:])
        act = (gate * jax.nn.sigmoid(gate) * up).astype(BF16)
        y = _dot(act, wout_ref[...])
        o_ref[0, part * rows:(part + 1) * rows, :] = (
            x + FFN_RES * res_gate * _rms_rows(y, g[2 * slot + 1:2 * slot + 2]))


def _ffn(h, mod, g, w_in, w_out, slot):
    b, t, d = h.shape
    d_ff = w_out.shape[0]
    tm = _tile(t, FFN_ROW_SPLITS * FFN_GROUP_ROWS)
    return pl.pallas_call(
        functools.partial(_ffn_kernel, slot=slot, d_ff=d_ff),
        grid=(b, t // tm),
        in_specs=[pl.BlockSpec((1, tm, d), lambda bb, i: (bb, i, 0)),
                  _mod_spec(mod),
                  _const_spec(g.shape),
                  _const_spec(w_in.shape),
                  _const_spec(w_out.shape)],
        out_specs=pl.BlockSpec((1, tm, d), lambda bb, i: (bb, i, 0)),
        out_shape=jax.ShapeDtypeStruct(h.shape, F32),
        compiler_params=_cparams(("parallel", "parallel")),
        name="ffn",
    )(h, mod, g, w_in, w_out)


def _even_in_kernel(h_ref, mod_ref, g_ref, w_ref, dft_ref, gb_ref, z_ref, a_ref, *, channel_dft):
    xm = _modulated(h_ref[0], mod_ref[0], g_ref[...], 1).astype(BF16)
    u = _dot(xm, w_ref[...])
    gb_ref[0] = u[:, :D_CONV].astype(BF16)
    z_ref[0] = (u[:, D_CONV:2 * D_CONV] * u[:, 2 * D_CONV:3 * D_CONV]).astype(BF16)
    xf = u[:, 3 * D_CONV:].astype(BF16)
    if not channel_dft:
        a_ref[0] = xf
        return
    for gi in range(FOURIER_GROUPS):
        pq = _dot(xf[:, gi * D_FG:(gi + 1) * D_FG], dft_ref[...])
        a_ref[0, :, gi * D_FG:(gi + 1) * D_FG] = pq[:, :D_FG].astype(BF16)
        a_ref[1, :, gi * D_FG:(gi + 1) * D_FG] = pq[:, D_FG:].astype(BF16)


def _even_in(h, mod, g, w_in, dft_c, channel_dft):
    b, t, d = h.shape
    tm = _tile(t, 512)
    if channel_dft:
        a_spec = pl.BlockSpec((2, tm, D_FOURIER), lambda bb, i: (0, i, bb))
        a_shape = jax.ShapeDtypeStruct((2, t, b * D_FOURIER), BF16)
    else:
        a_spec = pl.BlockSpec((1, tm, D_FOURIER), lambda bb, i: (bb, i, 0))
        a_shape = jax.ShapeDtypeStruct((b, t, D_FOURIER), BF16)
    return pl.pallas_call(
        functools.partial(_even_in_kernel, channel_dft=channel_dft),
        grid=(b, t // tm),
        in_specs=[pl.BlockSpec((1, tm, d), lambda bb, i: (bb, i, 0)),
                  _mod_spec(mod),
                  _const_spec(g.shape),
                  _const_spec(w_in.shape),
                  _const_spec(dft_c.shape)],
        out_specs=[pl.BlockSpec((1, tm, D_CONV), lambda bb, i: (bb, i, 0)),
                   pl.BlockSpec((1, tm, D_CONV), lambda bb, i: (bb, i, 0)),
                   a_spec],
        out_shape=[jax.ShapeDtypeStruct((b, t, D_CONV), BF16),
                   jax.ShapeDtypeStruct((b, t, D_CONV), BF16),
                   a_shape],
        compiler_params=_cparams(("parallel", "parallel")),
        name="even_in",
    )(h, mod, g, w_in, dft_c)


DFT_COLS = 64


def _dft_rows_kernel(x_ref, f_ref, tc_ref, ts_ref, o_ref):
    res = _dot(f_ref[...], x_ref[0])
    r = f_ref.shape[0] // 2
    reps = D_FOURIER // tc_ref.shape[2]
    for c in range(tc_ref.shape[0]):
        a_re = res[:r, c * D_FOURIER:(c + 1) * D_FOURIER]
        a_im = res[r:, c * D_FOURIER:(c + 1) * D_FOURIER]
        tc = jnp.concatenate([tc_ref[c]] * reps, axis=1)
        ts = jnp.concatenate([ts_ref[c]] * reps, axis=1)
        o_ref[0, 0, c] = (a_re * tc + a_im * ts).astype(BF16)
        o_ref[0, 1, c] = (a_im * tc - a_re * ts).astype(BF16)


def _dft_cols_kernel(b_ref, m_ref, cs_ref, o_ref, *, scale):
    z = _dot(m_ref[...], jnp.concatenate([b_ref[0, 0], b_ref[0, 1]], axis=0))
    kb = z.shape[1] // D_FOURIER
    for gi in range(FOURIER_GROUPS):
        lanes = [slice(k * D_FOURIER + gi * D_FG, k * D_FOURIER + (gi + 1) * D_FG) for k in range(kb)]
        z_re = jnp.concatenate([z[:DFT_COLS, ln] for ln in lanes], axis=0)
        z_im = jnp.concatenate([z[DFT_COLS:, ln] for ln in lanes], axis=0)
        y = _dot(jnp.concatenate([z_re, z_im], axis=1).astype(BF16), cs_ref[...]) * scale
        for k in range(kb):
            o_ref[0, :, lanes[k]] = y[k * DFT_COLS:(k + 1) * DFT_COLS].astype(o_ref.dtype)


def _factored_dft(xf, tabs):
    b, t, _ = xf.shape
    r = t // DFT_COLS
    cb, kb = 8, 8
    rows = pl.pallas_call(
        _dft_rows_kernel,
        grid=(b, DFT_COLS // cb),
        in_specs=[pl.BlockSpec((1, r, cb * D_FOURIER), lambda bb, j: (bb, 0, j)),
                  _const_spec(tabs["f_rows"].shape),
                  pl.BlockSpec((cb, r, HEAD_PAD), lambda bb, j: (j, 0, 0)),
                  pl.BlockSpec((cb, r, HEAD_PAD), lambda bb, j: (j, 0, 0))],
        out_specs=pl.BlockSpec((1, 2, cb, r, D_FOURIER), lambda bb, j: (bb, 0, j, 0, 0)),
        out_shape=jax.ShapeDtypeStruct((b, 2, DFT_COLS, r, D_FOURIER), BF16),
        compiler_params=_cparams(("parallel", "parallel")),
        name="dft_rows",
    )(xf.reshape(b, r, DFT_COLS * D_FOURIER), tabs["f_rows"], tabs["tw_cos"], tabs["tw_sin"])
    out = pl.pallas_call(
        functools.partial(_dft_cols_kernel, scale=1.0 / math.sqrt(t * D_FG)),
        grid=(b, r // kb),
        in_specs=[pl.BlockSpec((1, 2, DFT_COLS, kb * D_FOURIER), lambda bb, j: (bb, 0, 0, j)),
                  _const_spec(tabs["m_cols"].shape),
                  _const_spec(tabs["cs_chan"].shape)],
        out_specs=pl.BlockSpec((1, DFT_COLS, kb * D_FOURIER), lambda bb, j: (bb, 0, j)),
        out_shape=jax.ShapeDtypeStruct((b, DFT_COLS, r * D_FOURIER), BF16),
        compiler_params=_cparams(("parallel", "parallel")),
        name="dft_cols",
    )(rows.reshape(b, 2, DFT_COLS, r * D_FOURIER), tabs["m_cols"], tabs["cs_chan"])
    return out.reshape(b, t, D_FOURIER)


def _cos_sin(num, den):
    ang = (num % den).astype(F32) * (2.0 * math.pi / den)
    return jnp.cos(ang), jnp.sin(ang)


def _factored_dft_tables(t):
    r = t // DFT_COLS
    i_r = jnp.arange(r, dtype=jnp.int32)
    i_c = jnp.arange(DFT_COLS, dtype=jnp.int32)
    i_g = jnp.arange(D_FG, dtype=jnp.int32)
    c_r, s_r = _cos_sin(i_r[:, None] * i_r[None, :], r)
    c_t, s_t = _cos_sin(i_c[:, None] * i_r[None, :], t)
    c_c, s_c = _cos_sin(i_c[:, None] * i_c[None, :], DFT_COLS)
    c_g, s_g = _cos_sin(i_g[:, None] * i_g[None, :], D_FG)
    lanes = lambda a: jnp.broadcast_to(a[:, :, None], (DFT_COLS, r, HEAD_PAD))
    return {
        "f_rows": jnp.concatenate([c_r, -s_r], axis=0).astype(BF16),
        "tw_cos": lanes(c_t), "tw_sin": lanes(s_t),
        "m_cols": jnp.block([[c_c, s_c], [-s_c, c_c]]).astype(BF16),
        "cs_chan": jnp.concatenate([c_g, s_g], axis=0).astype(BF16),
    }


def _matmul_kernel(a_ref, b_ref, o_ref, acc_ref, *, scale):
    k = pl.program_id(2)

    @pl.when(k == 0)
    def _():
        acc_ref[...] = jnp.zeros_like(acc_ref)

    acc_ref[...] += _dot(a_ref[...], b_ref[...])

    @pl.when(k == pl.num_programs(2) - 1)
    def _():
        o_ref[...] = (acc_ref[...] * scale).astype(o_ref.dtype)


def _matmul(a, b, scale, out_dtype):
    m, kk = a.shape
    n = b.shape[1]
    bm, bn, bk = _tile(m, 1024), _tile(n, 1024), _tile(kk, 2048)
    return pl.pallas_call(
        functools.partial(_matmul_kernel, scale=scale),
        grid=(m // bm, n // bn, kk // bk),
        in_specs=[pl.BlockSpec((bm, bk), lambda i, j, k: (i, k)),
                  pl.BlockSpec((bk, bn), lambda i, j, k: (k, j))],
        out_specs=pl.BlockSpec((bm, bn), lambda i, j, k: (i, j)),
        out_shape=jax.ShapeDtypeStruct((m, n), out_dtype),
        scratch_shapes=[pltpu.VMEM((bm, bn), F32)],
        compiler_params=_cparams(("parallel", "parallel", "arbitrary")),
        name="dft_matmul",
    )(a, b)


def _even_out_kernel(h_ref, mod_ref, g_ref, gb_ref, z_ref, zp_ref, zn_ref, yf_ref, cw_ref, w_ref, o_ref):
    i = pl.program_id(1)
    x = h_ref[0]
    mod = mod_ref[0]
    g = g_ref[...]
    z = z_ref[0].astype(F32)
    tm = z.shape[0]
    halo = zp_ref.shape[1]
    prev_row = jnp.where(i > 0, zp_ref[0, halo - 1:halo, :].astype(F32), 0.0)
    next_row = jnp.where(i < pl.num_programs(1) - 1, zn_ref[0, 0:1, :].astype(F32), 0.0)
    row = lax.broadcasted_iota(jnp.int32, z.shape, 0)
    z_before = jnp.where(row == 0, prev_row, pltpu.roll(z, 1, 0))
    z_after = jnp.where(row == tm - 1, next_row, pltpu.roll(z, tm - 1, 0))
    cw = cw_ref[...]
    conv = z_before * cw[0:1] + z * cw[1:2] + z_after * cw[2:3]
    y_conv = (gb_ref[0].astype(F32) * conv).astype(BF16)
    y = _dot(y_conv, w_ref[:D_CONV, :]) + _dot(yf_ref[0], w_ref[D_CONV:, :])
    o_ref[0] = x + mod[5:6] * _rms_rows(y, g[3:4])


def _even_out(h, mod, g, gb, z, yf, conv_w, w_out):
    b, t, d = h.shape
    tm = _tile(t, 512)
    halo = 16
    nh = tm // halo
    last_halo = t // halo - 1
    return pl.pallas_call(
        _even_out_kernel,
        grid=(b, t // tm),
        in_specs=[pl.BlockSpec((1, tm, d), lambda bb, i: (bb, i, 0)),
                  _mod_spec(mod),
                  _const_spec(g.shape),
                  pl.BlockSpec((1, tm, D_CONV), lambda bb, i: (bb, i, 0)),
                  pl.BlockSpec((1, tm, D_CONV), lambda bb, i: (bb, i, 0)),
                  pl.BlockSpec((1, halo, D_CONV), lambda bb, i: (bb, jnp.maximum(i * nh - 1, 0), 0)),
                  pl.BlockSpec((1, halo, D_CONV), lambda bb, i: (bb, jnp.minimum((i + 1) * nh, last_halo), 0)),
                  pl.BlockSpec((1, tm, D_FOURIER), lambda bb, i: (bb, i, 0)),
                  _const_spec(conv_w.shape),
                  _const_spec(w_out.shape)],
        out_specs=pl.BlockSpec((1, tm, d), lambda bb, i: (bb, i, 0)),
        out_shape=jax.ShapeDtypeStruct(h.shape, F32),
        compiler_params=_cparams(("parallel", "parallel")),
        name="even_out",
    )(h, mod, g, gb, z, z, z, yf, conv_w, w_out)


def _dft_tables(t):
    n = jnp.arange(t, dtype=jnp.int32)
    ang = ((n[:, None] * n[None, :]) % t).astype(F32) * (2.0 * math.pi / t)
    w_pos = jnp.concatenate([jnp.cos(ang), -jnp.sin(ang)], axis=1).astype(BF16)
    return w_pos


def _channel_dft_table():
    n = jnp.arange(D_FG, dtype=jnp.int32)
    ang = ((n[:, None] * n[None, :]) % D_FG).astype(F32) * (2.0 * math.pi / D_FG)
    return jnp.concatenate([jnp.cos(ang), jnp.sin(ang)], axis=1).astype(BF16)


def _even_mixer(h, mod, g, w_in, conv_w, w_out, dft_c, dft_pos):
    b, t, _ = h.shape
    if isinstance(dft_pos, dict):
        gb, z, xf = _even_in(h, mod, g, w_in, dft_c, False)
        yf = _factored_dft(xf, dft_pos)
    else:
        gb, z, a = _even_in(h, mod, g, w_in, dft_c, True)
        yf = _matmul(dft_pos, a.reshape(2 * t, b * D_FOURIER), 1.0 / math.sqrt(t * D_FG), BF16)
        yf = yf.reshape(t, b, D_FOURIER).transpose(1, 0, 2)
    return _even_out(h, mod, g, gb, z, yf, conv_w, w_out)


def _odd_in_kernel(h_ref, mod_ref, g_ref, wtok_ref, wt_ref, wukp_ref, wuvt_ref, wuqt_ref, place_ref, gsel_ref,
                   gkv_row_ref, gkv_col_ref, gq_col_ref,
                   cosk_ref, sink_ref, tkr_ref, cosq_ref, sinq_ref, cosm_ref, sinm_ref,
                   khd_ref, khm_ref, vtd_ref, vtm_ref, kn_ref, *q_refs, with_q, split8):
    xm = _modulated(h_ref[0], mod_ref[0], g_ref[...], 1).astype(BF16)
    tm = xm.shape[0]

    ut = _dot(xm, wtok_ref[...])
    k = ut[:, :DA_QCOLS]
    ckv = ut[:, DA_QCOLS:DA_QCOLS + MLA_KV_RANK]
    kr2 = ut[:, DA_QCOLS + MLA_KV_RANK:]
    lane = lax.broadcasted_iota(jnp.int32, k.shape, 1)
    first_half = (lane % (DA_DK // 2)) < (DA_DK // 4)
    k_sw = jnp.where(first_half, pltpu.roll(k, DA_QCOLS - DA_DK // 4, 1), pltpu.roll(k, DA_DK // 4, 1))
    reps = DA_QCOLS // HEAD_PAD
    cosk = jnp.concatenate([cosk_ref[...]] * reps, axis=1)
    sink = jnp.concatenate([sink_ref[...]] * reps, axis=1)
    k_rot = k * cosk + k_sw * sink
    ckvn = _rms_rows(ckv, gkv_row_ref[...]).astype(BF16)
    k_nope = _dot(ckvn, wukp_ref[...])
    pr = kr2 * tkr_ref[...]
    pr_hi = pr.astype(BF16)
    pr_lo = (pr - pr_hi.astype(F32)).astype(BF16)
    k_mla = k_nope + _dot(pr_hi, place_ref[...]) + _dot(pr_lo, place_ref[...])
    k_odd = pltpu.roll(k_rot, DA_QCOLS - DA_DK, 1)
    lane_h = lax.broadcasted_iota(jnp.int32, (tm, HEAD_PAD), 1)
    sq = []
    for mp in range(2 * DA_HEADS):
        src = (k_rot if mp % 2 == 0 else k_odd)[:, (mp // 2) * HEAD_PAD:(mp // 2 + 1) * HEAD_PAD]
        kd = jnp.where(lane_h < DA_DK, src, 0.0)
        if split8:
            k_hi, k_lo = _split8(kd * DA_SPLIT_SCALE)
            khd_ref[0, mp] = jnp.where(lane_h < DA_DK, k_hi, pltpu.roll(k_lo, DA_DK, 1)).astype(F8)
            kf = k_hi + k_lo
        else:
            khd_ref[0, mp] = jnp.where(lane_h == DA_DK, 1.0, kd).astype(BF16)
            kf = kd.astype(BF16).astype(F32)
        sq.append((kf * kf).astype(BF16))
    for hh in range(MLA_HEADS):
        kd = k_mla[:, hh * HEAD_PAD:(hh + 1) * HEAD_PAD]
        khm_ref[0, hh] = jnp.where(lane_h == MLA_DQK, 1.0, kd).astype(BF16)
        kf = kd.astype(BF16).astype(F32)
        sq.append((kf * kf).astype(BF16))
    kn_ref[0] = _dot_nt(gsel_ref[...], jnp.concatenate(sq, axis=1))

    r0 = DA_VCOLS
    vckv = _dot_nt(wt_ref[:r0 + MLA_KV_RANK, :], xm)
    for hh in range(DA_HEADS):
        vtd_ref[0, hh] = vckv[hh * DA_DV:(hh + 1) * DA_DV].astype(BF16)
    ckv_t = vckv[r0:]
    ms = jnp.mean(ckv_t * ckv_t, axis=0, keepdims=True)
    ckvn_t = (ckv_t * lax.rsqrt(ms + EPS) * gkv_col_ref[...]).astype(BF16)
    vm_t = _dot(wuvt_ref[...], ckvn_t)
    for hh in range(MLA_HEADS):
        vtm_ref[0, hh] = vm_t[hh * MLA_DV:(hh + 1) * MLA_DV].astype(BF16)

    if with_q:
        qtd_ref, qtm_ref = q_refs
        r1 = r0 + MLA_KV_RANK
        q_t = _dot_nt(wt_ref[r1:, :], xm)
        cosq, sinq = cosq_ref[...], sinq_ref[...]
        qd = DA_DK // 4
        for mp in range(2 * DA_HEADS):
            q = q_t[mp * DA_DK:(mp + 1) * DA_DK]
            q_sw = jnp.concatenate([q[qd:2 * qd], q[:qd], q[3 * qd:], q[2 * qd:3 * qd]], axis=0)
            q_rot = q * cosq + q_sw * sinq
            if split8:
                q_hi, q_lo = _split8(q_rot * DA_SPLIT_SCALE)
                qtd_ref[0, mp, :DA_DK, :] = q_hi.astype(F8)
                qtd_ref[0, mp, DA_DK:, :] = q_lo.astype(F8)
            else:
                qtd_ref[0, mp // 2, (mp % 2) * DA_DK:(mp % 2 + 1) * DA_DK, :] = (
                    q_rot * (DA_SCALE * LOG2E)).astype(BF16)
        cq_t = q_t[DA_QCOLS:]
        ms = jnp.mean(cq_t * cq_t, axis=0, keepdims=True)
        cqn_t = (cq_t * lax.rsqrt(ms + EPS) * gq_col_ref[...]).astype(BF16)
        qm_t = _dot(wuqt_ref[...], cqn_t) * (MLA_SCALE * LOG2E)
        cosm, sinm = cosm_ref[...], sinm_ref[...]
        rd = MLA_ROPE // 4
        for hh in range(MLA_HEADS):
            base = hh * HEAD_PAD
            qtm_ref[0, hh, :MLA_NOPE, :] = qm_t[base:base + MLA_NOPE].astype(BF16)
            r = qm_t[base + MLA_NOPE:base + MLA_DQK]
            r_sw = jnp.concatenate([r[rd:2 * rd], r[:rd], r[3 * rd:], r[2 * rd:3 * rd]], axis=0)
            qtm_ref[0, hh, MLA_NOPE:MLA_DQK, :] = (r * cosm + r_sw * sinm).astype(BF16)
            qtm_ref[0, hh, MLA_DQK:, :] = jnp.zeros((HEAD_PAD - MLA_DQK, tm), BF16)


def _odd_in(h, mod, g, wts, tabs, kv_prev, tok_off, with_q, split8):
    b, t, d = h.shape
    da_dtype = F8 if split8 else BF16
    tk_total = tabs["tk_total"]
    tm = _tile(t, 512)
    assert tok_off % tm == 0
    off = tok_off // tm
    row_tab = lambda w: pl.BlockSpec((tm, w), lambda bb, i: (i, 0))
    col_tab = lambda r: pl.BlockSpec((r, tm), lambda bb, i: (0, i))
    consts = [wts["wtok"], wts["wt"], wts["wukp"], wts["wuvt"], wts["wuqt"], wts["place"], wts["gsel"],
              wts["gkv_row"], wts["gkv_col"], wts["gq_col"]]
    in_specs = ([pl.BlockSpec((1, tm, d), lambda bb, i: (bb, i, 0)), _mod_spec(mod), _const_spec(g.shape)]
                + [_const_spec(c.shape) for c in consts]
                + [row_tab(HEAD_PAD), row_tab(HEAD_PAD), row_tab(2 * MLA_ROPE),
                   col_tab(DA_DK), col_tab(DA_DK), col_tab(MLA_ROPE), col_tab(MLA_ROPE)])
    args = [h, mod, g] + consts + [tabs["cosk"], tabs["sink"], tabs["tkr"],
                                   tabs["cosq"], tabs["sinq"], tabs["cosm"], tabs["sinm"]]
    out_specs = [pl.BlockSpec((1, 2 * DA_HEADS, tm, HEAD_PAD), lambda bb, i: (bb, 0, i + off, 0)),
                 pl.BlockSpec((1, MLA_HEADS, tm, HEAD_PAD), lambda bb, i: (bb, 0, i + off, 0)),
                 pl.BlockSpec((1, DA_HEADS, DA_DV, tm), lambda bb, i: (bb, 0, 0, i + off)),
                 pl.BlockSpec((1, MLA_HEADS, MLA_DV, tm), lambda bb, i: (bb, 0, 0, i + off)),
                 pl.BlockSpec((1, N_MAPS, tm), lambda bb, i: (bb, 0, i + off))]
    out_shape = [jax.ShapeDtypeStruct((b, 2 * DA_HEADS, tk_total, HEAD_PAD), da_dtype),
                 jax.ShapeDtypeStruct((b, MLA_HEADS, tk_total, HEAD_PAD), BF16),
                 jax.ShapeDtypeStruct((b, DA_HEADS, DA_DV, tk_total), BF16),
                 jax.ShapeDtypeStruct((b, MLA_HEADS, MLA_DV, tk_total), BF16),
                 jax.ShapeDtypeStruct((b, N_MAPS, tk_total), F32)]
    if with_q:
        n_qd = 2 * DA_HEADS if split8 else DA_HEADS
        out_specs += [pl.BlockSpec((1, n_qd, HEAD_PAD, tm), lambda bb, i: (bb, 0, 0, i)),
                      pl.BlockSpec((1, MLA_HEADS, HEAD_PAD, tm), lambda bb, i: (bb, 0, 0, i))]
        out_shape += [jax.ShapeDtypeStruct((b, n_qd, HEAD_PAD, t), da_dtype),
                      jax.ShapeDtypeStruct((b, MLA_HEADS, HEAD_PAD, t), BF16)]
    aliases = {}
    kernel_fn = functools.partial(_odd_in_kernel, with_q=with_q, split8=split8)
    if kv_prev is not None:
        n_in = len(args)
        in_specs += [pl.BlockSpec(memory_space=pl.ANY)] * N_KV_ARRAYS
        args += list(kv_prev)
        aliases = {n_in + a: a for a in range(N_KV_ARRAYS)}
        kernel_fn = functools.partial(_odd_in_alias_kernel, n_in=n_in, with_q=with_q, split8=split8)
    return pl.pallas_call(
        kernel_fn,
        grid=(b, t // tm),
        in_specs=in_specs,
        out_specs=out_specs,
        out_shape=out_shape,
        input_output_aliases=aliases,
        compiler_params=_cparams(("parallel", "parallel")),
        name="odd_in",
    )(*args)


def _odd_in_alias_kernel(*refs, n_in, with_q, split8):
    _odd_in_kernel(*refs[:n_in], *refs[n_in + N_KV_ARRAYS:], with_q=with_q, split8=split8)


def _attn_kernel(lam_ref, gsub_ref, kn_ref, qtd_ref, qtm_ref, khd_ref, khm_ref, vtd_ref, vtm_ref,
                 otd_ref, otm_ref, lmin_ref, qad_ref, qam_ref, shift_ref, m_ref, l_ref, accd_ref, accm_ref,
                 *, lam_init, safe):
    j = pl.program_id(2)
    tq = qtd_ref.shape[-1]

    @pl.when(j == 0)
    def _():
        m_ref[...] = jnp.full_like(m_ref, NEG_BIG)
        l_ref[...] = jnp.zeros_like(l_ref)
        accd_ref[...] = jnp.zeros_like(accd_ref)
        accm_ref[...] = jnp.zeros_like(accm_ref)
        kmax = jnp.sqrt(jnp.max(kn_ref[0], axis=1, keepdims=True))
        shift_row = lax.broadcasted_iota(jnp.int32, (16, tq), 0) == 0
        for idx in range(2 * DA_HEADS):
            if safe:
                q = qtd_ref[0, idx // 2, (idx % 2) * DA_DK:(idx % 2 + 1) * DA_DK, :]
                qad_ref[idx] = jnp.concatenate([q, jnp.zeros((HEAD_PAD - DA_DK, tq), BF16)], axis=0)
            else:
                q_hi, q_lo = qtd_ref[0, idx, :DA_DK, :], qtd_ref[0, idx, DA_DK:, :]
                qad_ref[idx] = jnp.concatenate([q_hi, q_hi, q_lo, q_lo], axis=0)
                qf = q_hi.astype(F32) + q_lo.astype(F32)
                qn = jnp.sqrt(jnp.sum(qf * qf, axis=0, keepdims=True))
                ok = (qn < F8_SAFE_MAX) & (kmax[idx:idx + 1] < F8_SAFE_MAX)
                shift_ref[idx] = jnp.where(ok, (BOUND_SLACK * kmax[idx:idx + 1]) * qn, jnp.inf)
        for hh in range(MLA_HEADS):
            idx = 2 * DA_HEADS + hh
            q = qtm_ref[0, hh, :MLA_DQK, :]
            if safe:
                shift_blk = jnp.zeros((16, tq), BF16)
            else:
                qf = q.astype(F32)
                qn = jnp.sqrt(jnp.sum(qf * qf, axis=0, keepdims=True))
                shift = -(BOUND_SLACK * kmax[idx:idx + 1]) * qn
                shift_blk = jnp.where(shift_row, shift, 0.0).astype(BF16)
            pad = jnp.zeros((HEAD_PAD - MLA_DQK - 16, tq), BF16)
            qam_ref[hh] = jnp.concatenate([q, shift_blk, pad], axis=0)

    def softmax_pv(s, ln, idx, v_t, acc_ref, acc_idx):
        if safe:
            m_old = m_ref[idx, :, ln]
            m_new = jnp.maximum(m_old, jnp.max(s, axis=0, keepdims=True))
            alpha = jnp.exp2(m_old - m_new)
            m_ref[idx, :, ln] = m_new
            p = jnp.exp2(s - m_new)
            l_ref[idx, :, ln] = alpha * l_ref[idx, :, ln] + jnp.sum(p.reshape(-1, 8, s.shape[1]), axis=0)
            acc_ref[acc_idx, :, ln] = acc_ref[acc_idx, :, ln] * alpha + _dot(v_t, p.astype(BF16))
        else:
            p = jnp.exp2(s)
            l_ref[idx, :, ln] += jnp.sum(p.reshape(-1, 8, s.shape[1]), axis=0)
            acc_ref[acc_idx, :, ln] += _dot(v_t, p.astype(BF16))

    chunk = ATTN_Q_CHUNK if (not safe and tq % ATTN_Q_CHUNK == 0) else tq
    q_chunks = [slice(c, c + chunk) for c in range(0, tq, chunk)]
    da_per_iter = 1 if safe else DA_HEADS_PER_ITER
    mla_per_iter = 1 if safe else MLA_HEADS_PER_ITER

    def da_score(hh, m, ln):
        keys = khd_ref[0, 2 * hh + m]
        if safe:
            return _dot(keys, qad_ref[2 * hh + m, :, ln])
        return _dot(jnp.concatenate([keys, keys], axis=1), qad_ref[2 * hh + m, :, ln])

    def mla_score(hh, ln):
        return _dot(khm_ref[0, hh], qam_ref[hh, :, ln])

    def run_maps(scores, consume):
        s_next = scores[0]()
        for n in range(len(scores)):
            s_cur = s_next
            if n + 1 < len(scores):
                s_next = scores[n + 1]()
            consume[n](s_cur)

    def da_body(it, carry):
        scores, consume = [], []
        for u in range(da_per_iter):
            hh = it * da_per_iter + u
            for m in range(2):
                for ln in q_chunks:
                    scores.append(functools.partial(da_score, hh, m, ln))
                    consume.append(lambda s, hh=hh, m=m, ln=ln: softmax_pv(
                        s if safe else s - shift_ref[2 * hh + m, :, ln], ln, 2 * hh + m, vtd_ref[0, hh],
                        accd_ref, 2 * hh + m))
        run_maps(scores, consume)
        return carry

    def mla_body(it, carry):
        scores, consume = [], []
        for u in range(mla_per_iter):
            hh = it * mla_per_iter + u
            for ln in [slice(0, tq)]:
                scores.append(functools.partial(mla_score, hh, ln))
                consume.append(lambda s, hh=hh, ln=ln: softmax_pv(s, ln, 2 * DA_HEADS + hh, vtm_ref[0, hh],
                                                                  accm_ref, hh))
        run_maps(scores, consume)
        return carry

    lax.fori_loop(0, DA_HEADS // da_per_iter, da_body, 0)
    lax.fori_loop(0, MLA_HEADS // mla_per_iter, mla_body, 0)

    @pl.when(j == pl.num_programs(2) - 1)
    def _():
        lp = lam_ref[...]
        lam = (jnp.exp(jnp.sum(lp[0:1] * lp[1:2], axis=1, keepdims=True))
               - jnp.exp(jnp.sum(lp[2:3] * lp[3:4], axis=1, keepdims=True)) + lam_init)
        gsub = gsub_ref[...]
        lsum = [jnp.sum(l_ref[idx], axis=0, keepdims=True) for idx in range(N_MAPS)]
        linv = [1.0 / ls for ls in lsum]
        for hh in range(DA_HEADS):
            o = accd_ref[2 * hh] * linv[2 * hh] - accd_ref[2 * hh + 1] * (lam * linv[2 * hh + 1])
            ms = jnp.mean(o * o, axis=0, keepdims=True)
            otd_ref[0, hh] = (o * (lax.rsqrt(ms + EPS) * (1.0 - lam_init)) * gsub).astype(otd_ref.dtype)
        for hh in range(MLA_HEADS):
            otm_ref[0, hh] = (accm_ref[hh] * linv[2 * DA_HEADS + hh]).astype(otm_ref.dtype)
        lmin = lsum[0]
        for idx in range(1, N_MAPS):
            lmin = jnp.minimum(lmin, lsum[idx])
        lmin_ref[0, 0] = jnp.broadcast_to(jnp.min(lmin, axis=1, keepdims=True), lmin_ref.shape[2:])


def _attention_call(lam_p, gsub_col, qtd, qtm, kv, lam_init, tq, tk, kv_off, n_kv, safe):
    khd, khm, vtd, vtm, kn = kv
    b, n_qd, _, t = qtd.shape
    nq = t // tq
    assert kv_off % n_kv == 0
    qad_scratch = (pltpu.VMEM((2 * DA_HEADS, HEAD_PAD, tq), BF16) if safe
                   else pltpu.VMEM((2 * DA_HEADS, 4 * DA_DK, tq), F8))
    return pl.pallas_call(
        functools.partial(_attn_kernel, lam_init=lam_init, safe=safe),
        grid=(b, nq, n_kv),
        in_specs=[_const_spec(lam_p.shape),
                  _const_spec(gsub_col.shape),
                  pl.BlockSpec((1, N_MAPS, n_kv * tk), lambda bb, i, j: (bb, 0, kv_off // n_kv)),
                  pl.BlockSpec((1, n_qd, HEAD_PAD, tq), lambda bb, i, j: (bb, 0, 0, i)),
                  pl.BlockSpec((1, MLA_HEADS, HEAD_PAD, tq), lambda bb, i, j: (bb, 0, 0, i)),
                  pl.BlockSpec((1, 2 * DA_HEADS, tk, HEAD_PAD), lambda bb, i, j: (bb, 0, kv_off + j, 0)),
                  pl.BlockSpec((1, MLA_HEADS, tk, HEAD_PAD), lambda bb, i, j: (bb, 0, kv_off + j, 0)),
                  pl.BlockSpec((1, DA_HEADS, DA_DV, tk), lambda bb, i, j: (bb, 0, 0, kv_off + j)),
                  pl.BlockSpec((1, MLA_HEADS, MLA_DV, tk), lambda bb, i, j: (bb, 0, 0, kv_off + j))],
        out_specs=[pl.BlockSpec((1, DA_HEADS, DA_DV, tq), lambda bb, i, j: (bb, 0, 0, i)),
                   pl.BlockSpec((1, MLA_HEADS, MLA_DV, tq), lambda bb, i, j: (bb, 0, 0, i)),
                   pl.BlockSpec((1, 1, 8, 128), lambda bb, i, j: (bb, i, 0, 0))],
        out_shape=[jax.ShapeDtypeStruct((b, DA_HEADS, DA_DV, t), BF16),
                   jax.ShapeDtypeStruct((b, MLA_HEADS, MLA_DV, t), BF16),
                   jax.ShapeDtypeStruct((b, nq, 8, 128), F32)],
        scratch_shapes=[qad_scratch,
                        pltpu.VMEM((MLA_HEADS, HEAD_PAD, tq), BF16),
                        pltpu.VMEM((2 * DA_HEADS, 1, tq), F32),
                        pltpu.VMEM((N_MAPS, 1, tq), F32),
                        pltpu.VMEM((N_MAPS, 8, tq), F32),
                        pltpu.VMEM((2 * DA_HEADS, DA_DV, tq), F32),
                        pltpu.VMEM((MLA_HEADS, MLA_DV, tq), F32)],
        compiler_params=_cparams(("parallel", "parallel", "arbitrary")),
        name="attention_safe" if safe else "attention",
    )(lam_p, gsub_col, kn, qtd, qtm, khd, khm, vtd, vtm)


def _odd_out_kernel(h_ref, mod_ref, g_ref, otd_ref, otm_ref, wt_ref, o_ref):
    y_t = _dot(wt_ref[:, :DA_VCOLS], otd_ref[0]) + _dot(wt_ref[:, DA_VCOLS:], otm_ref[0])
    y = y_t.T
    o_ref[0] = h_ref[0] + mod_ref[0][5:6] * _rms_rows(y, g_ref[3:4])


def _odd_out(h, mod, g, otd, otm, w_out_t):
    b, t, d = h.shape
    tm = _tile(t, 512)
    otd = otd.reshape(b, DA_VCOLS, t)
    otm = otm.reshape(b, MLA_HEADS * MLA_DV, t)
    return pl.pallas_call(
        _odd_out_kernel,
        grid=(b, t // tm),
        in_specs=[pl.BlockSpec((1, tm, d), lambda bb, i: (bb, i, 0)),
                  _mod_spec(mod),
                  _const_spec(g.shape),
                  pl.BlockSpec((1, DA_VCOLS, tm), lambda bb, i: (bb, 0, i)),
                  pl.BlockSpec((1, MLA_HEADS * MLA_DV, tm), lambda bb, i: (bb, 0, i)),
                  _const_spec(w_out_t.shape)],
        out_specs=pl.BlockSpec((1, tm, d), lambda bb, i: (bb, i, 0)),
        out_shape=jax.ShapeDtypeStruct(h.shape, F32),
        compiler_params=_cparams(("parallel", "parallel")),
        name="odd_out",
    )(h, mod, g, otd, otm, w_out_t)


def _rope_angles(n_tokens, rot_dim):
    rows = n_tokens // GRID_W
    row = jnp.broadcast_to(jnp.arange(rows)[:, None], (rows, GRID_W)).reshape(-1).astype(F32)
    col = jnp.broadcast_to(jnp.arange(GRID_W)[None, :], (rows, GRID_W)).reshape(-1).astype(F32)
    n_freq = rot_dim // 4
    freqs = ROPE_THETA ** (-jnp.arange(n_freq, dtype=F32) / n_freq)
    return row[:, None] * freqs, col[:, None] * freqs


def _rope_cos_sin(n_tokens, rot_dim, identity):
    if identity:
        return jnp.ones((n_tokens, rot_dim), F32), jnp.zeros((n_tokens, rot_dim), F32)
    ar, ac = _rope_angles(n_tokens, rot_dim)
    cos = jnp.concatenate([jnp.cos(ar), jnp.cos(ar), jnp.cos(ac), jnp.cos(ac)], axis=1)
    sin = jnp.concatenate([-jnp.sin(ar), jnp.sin(ar), -jnp.sin(ac), jnp.sin(ac)], axis=1)
    return cos, sin


def _rope_tables(n_tokens, identity, tk_total):
    cd, sd = _rope_cos_sin(n_tokens, DA_DK, identity)
    cm, sm = _rope_cos_sin(n_tokens, MLA_ROPE, identity)
    return {
        "cosk": jnp.concatenate([cd, cd], axis=1), "sink": jnp.concatenate([sd, sd], axis=1),
        "tkr": jnp.concatenate([cm, sm], axis=1),
        "cosq": cd.T, "sinq": sd.T, "cosm": cm.T, "sinm": sm.T,
        "tk_total": tk_total,
    }


def _swap_perm(rot_dim):
    q = rot_dim // 4
    return jnp.concatenate([jnp.arange(q, 2 * q), jnp.arange(0, q), jnp.arange(3 * q, 4 * q), jnp.arange(2 * q, 3 * q)])


def _odd_weights(w_in, g_q, w_uq, g_kv, w_uk, w_uv):
    w_q = w_in[:, :DA_QCOLS]
    w_cq = w_in[:, DA_QCOLS:Q_COLS]
    w_k = w_in[:, Q_COLS:Q_COLS + DA_QCOLS]
    w_v = w_in[:, Q_COLS + DA_QCOLS:Q_COLS + DA_QCOLS + DA_VCOLS]
    w_ckv = w_in[:, Q_COLS + DA_QCOLS + DA_VCOLS:Q_COLS + DA_QCOLS + DA_VCOLS + MLA_KV_RANK]
    w_kr = w_in[:, Q_COLS + DA_QCOLS + DA_VCOLS + MLA_KV_RANK:]
    wtok = jnp.concatenate([w_k, w_ckv, w_kr, w_kr[:, _swap_perm(MLA_ROPE)]], axis=1).astype(BF16)
    wt = jnp.concatenate([w_v, w_ckv, w_q, w_cq], axis=1).T.astype(BF16)
    pad_k = jnp.zeros((MLA_KV_RANK, MLA_HEADS, HEAD_PAD), F32)
    wukp = pad_k.at[:, :, :MLA_NOPE].set(w_uk.reshape(MLA_KV_RANK, MLA_HEADS, MLA_NOPE))
    wukp = wukp.reshape(MLA_KV_RANK, MLA_HEADS * HEAD_PAD).astype(BF16)
    pad_q = jnp.zeros((MLA_Q_RANK, MLA_HEADS, HEAD_PAD), F32)
    wuqp = pad_q.at[:, :, :MLA_DQK].set(w_uq.reshape(MLA_Q_RANK, MLA_HEADS, MLA_DQK))
    wuqt = wuqp.reshape(MLA_Q_RANK, MLA_HEADS * HEAD_PAD).T.astype(BF16)
    eye = jnp.eye(MLA_ROPE, dtype=F32)
    place = jnp.zeros((2, MLA_ROPE, MLA_HEADS, HEAD_PAD), F32)
    place = place.at[:, :, :, MLA_NOPE:MLA_DQK].set(jnp.broadcast_to(eye[None, :, None, :], (2, MLA_ROPE, MLA_HEADS, MLA_ROPE)))
    place = place.reshape(2 * MLA_ROPE, MLA_HEADS * HEAD_PAD).astype(BF16)
    return {
        "wtok": wtok, "wt": wt, "wukp": wukp, "wuvt": w_uv.T.astype(BF16), "wuqt": wuqt, "place": place,
        "gsel": jnp.kron(jnp.eye(N_MAPS, dtype=F32), jnp.ones((1, HEAD_PAD), F32)).astype(BF16),
        "gkv_row": g_kv.reshape(1, -1).astype(F32), "gkv_col": g_kv.reshape(-1, 1).astype(F32),
        "gq_col": g_q.reshape(-1, 1).astype(F32),
    }


def kernel(x, c, ctx, c_ctx, w_mod, b_mod, norm_g, w_ffn_in, w_ffn_out, w_in_even, conv_w, w_out_even,
           w_in_odd, g_q_mla, w_uq, g_kv_mla, w_uk, w_uv, lam_q1, lam_k1, lam_q2, lam_k2, g_subln, w_out_odd):
    b, t, d = x.shape
    tc = ctx.shape[1]
    depth = w_mod.shape[0]
    tk_total = t + tc

    rows = -(-(b + 1) // 8) * 8
    cond = jnp.zeros((rows, d), F32).at[:b].set(c).at[b].set(c_ctx)
    mod = _adaln(cond, w_mod, b_mod).reshape(depth, rows, N_MOD, d)

    dft_c = _channel_dft_table()
    factored = lambda n: n % (DFT_COLS * 16) == 0
    w_pos_x = _factored_dft_tables(t) if factored(t) else _dft_tables(t)
    w_pos_c = _factored_dft_tables(tc) if factored(tc) else _dft_tables(tc)
    tabs_x = _rope_tables(t, False, tk_total)
    tabs_c = _rope_tables(tc, True, tk_total)
    tq = _tile(t, 1024)
    tk = 768 if tk_total % 768 == 0 else _tile(tk_total, 512)
    assert t % tc == 0 and tk_total % tk == 0

    h, hc = x, ctx
    for l in range(depth):
        last = l == depth - 1
        odd = l % 2 == 1
        ctx_live = (not last) or odd
        g = norm_g[l]
        m_x, m_c = mod[l, :b], mod[l, b:b + 1]
        wi0, wo0 = w_ffn_in[l, 0].astype(BF16), w_ffn_out[l, 0].astype(BF16)
        h = _ffn(h, m_x, g, wi0, wo0, 0)
        if ctx_live:
            hc = _ffn(hc, m_c, g, wi0, wo0, 0)

        if not odd:
            e = l // 2
            w_in, w_out = w_in_even[e].astype(BF16), w_out_even[e].astype(BF16)
            h = _even_mixer(h, m_x, g, w_in, conv_w[e], w_out, dft_c, w_pos_x)
            if ctx_live:
                hc = _even_mixer(hc, m_c, g, w_in, conv_w[e], w_out, dft_c, w_pos_c)
        else:
            o = l // 2
            lam_init = 0.8 - 0.6 * math.exp(-0.3 * l)
            wts = _odd_weights(w_in_odd[o], g_q_mla[o], w_uq[o], g_kv_mla[o], w_uk[o], w_uv[o])
            lam_p = jnp.stack([lam_q1[o], lam_k1[o], lam_q2[o], lam_k2[o]]).astype(F32)
            gsub_col = g_subln[o].reshape(-1, 1).astype(F32)
            w_out_t = w_out_odd[o].T.astype(BF16)
            def latent_attention(safe, h=h, hc=hc, m_x=m_x, m_c=m_c, g=g, wts=wts, lam_p=lam_p,
                                 gsub_col=gsub_col, lam_init=lam_init):
                outs_x = _odd_in(h, m_x, g, wts, tabs_x, None, 0, True, not safe)
                kv = _odd_in(hc, m_c, g, wts, tabs_c, outs_x[:N_KV_ARRAYS], t, False, not safe)
                return _attention_call(lam_p, gsub_col, *outs_x[N_KV_ARRAYS:], kv, lam_init, tq, tk, 0,
                                       tk_total // tk, safe)

            otd, otm, lmin = latent_attention(False)
            otd, otm = lax.cond(jnp.min(lmin) >= MIN_DENOMINATOR, lambda: (otd, otm),
                                lambda: tuple(latent_attention(True)[:2]))
            if not last:
                outs_c = _odd_in(hc, m_c, g, wts, dict(tabs_c, tk_total=tc), None, 0, True, False)
                ocd, ocm = _attention_call(lam_p, gsub_col, *outs_c[N_KV_ARRAYS:], outs_c[:N_KV_ARRAYS],
                                           lam_init, tc, tc, 0, 1, True)[:2]
                hc = _odd_out(hc, m_c, g, ocd, ocm, w_out_t)
            h = _odd_out(h, m_x, g, otd, otm, w_out_t)

        wi1, wo1 = w_ffn_in[l, 1].astype(BF16), w_ffn_out[l, 1].astype(BF16)
        h = _ffn(h, m_x, g, wi1, wo1, 2)
        if not last:
            hc = _ffn(hc, m_c, g, wi1, wo1, 2)
    return h
```

```python
import functools
import math

import jax
import jax.numpy as jnp
from jax import lax
from jax.experimental import pallas as pl
from jax.experimental.pallas import tpu as pltpu

F32 = jnp.float32
BF16 = jnp.bfloat16

D_MODEL = 1024
GRID_W = 64
N_MOD = 9
FFN_RES = 0.5
EPS = 1e-6
ROPE_THETA = 10000.0

D_CONV = 512
D_FOURIER = 512
FOURIER_GROUPS = 4
D_FG = D_FOURIER // FOURIER_GROUPS

DA_HEADS = 8
DA_DK = 64
DA_DV = 128
DA_SCALE = DA_DK ** -0.5
MLA_HEADS = 8
MLA_NOPE = 64
MLA_ROPE = 32
MLA_DQK = MLA_NOPE + MLA_ROPE
MLA_DV = 64
MLA_Q_RANK = 384
MLA_KV_RANK = 256
MLA_SCALE = MLA_DQK ** -0.5
DA_QCOLS = DA_HEADS * 2 * DA_DK
DA_VCOLS = DA_HEADS * DA_DV
Q_COLS = DA_QCOLS + MLA_Q_RANK
HEAD_PAD = 128
F32_SUBLANES = 8
BF16_SUBLANES = 16
N_MAPS = 2 * DA_HEADS + MLA_HEADS
LOG2E = 1.4426950408889634
NEG_BIG = -1e30
N_KV_ARRAYS = 5
F8 = jnp.float8_e4m3fn
F8_SAFE_MAX = 400.0
DA_SPLIT_SCALE = math.sqrt(DA_SCALE * LOG2E)
FFN_ROW_SPLITS = 2
FFN_GROUP_ROWS = 256
DA_HEADS_PER_ITER = 4
MLA_HEADS_PER_ITER = 8
ATTN_Q_CHUNK = 256
MLA_Q_CHUNK = 512
BOUND_SLACK = 1.02
MIN_DENOMINATOR = 2.0 ** -40

VMEM_LIMIT_V7X = 56 * 1024 * 1024


def _cparams(sem):
    return pltpu.CompilerParams(dimension_semantics=sem, vmem_limit_bytes=VMEM_LIMIT_V7X)


def _tile(n, pref):
    if n <= pref:
        return n
    t = pref - pref % 128
    while t >= 128:
        if n % t == 0:
            return t
        t -= 128
    return n


def _const_spec(shape):
    nd = len(shape)
    return pl.BlockSpec(shape, lambda *_: (0,) * nd, pipeline_mode=pl.Buffered(1))


def _mod_spec(mod):
    if mod.shape[0] == 1:
        return pl.BlockSpec((1, N_MOD, D_MODEL), lambda b, *_: (0, 0, 0))
    return pl.BlockSpec((1, N_MOD, D_MODEL), lambda b, *_: (b, 0, 0))


def _rms_rows(x, g):
    ms = jnp.mean(x * x, axis=-1, keepdims=True)
    return x * lax.rsqrt(ms + EPS) * g


def _modulated(x, mod, g, slot):
    shift = mod[3 * slot:3 * slot + 1]
    scale = mod[3 * slot + 1:3 * slot + 2]
    return _rms_rows(x, g[2 * slot:2 * slot + 1]) * (1.0 + scale) + shift


def _split8(x):
    hi = x.astype(F8).astype(F32)
    lo = (x - hi).astype(F8).astype(F32)
    return hi, lo


def _dot(a, b):
    return jnp.dot(a, b, preferred_element_type=F32)


def _dot_nt(a, b):
    return lax.dot_general(a, b, (((1,), (1,)), ((), ())), preferred_element_type=F32)


def _adaln_kernel(c_ref, w_ref, b_ref, o_ref):
    c = c_ref[...]
    a = c * jax.nn.sigmoid(c)
    o_ref[0] = jnp.dot(a, w_ref[0], preferred_element_type=F32,
                       precision=lax.Precision.HIGHEST) + b_ref[0]


def _adaln(cond, w_mod, b_mod):
    depth, d, n = w_mod.shape
    rows = cond.shape[0]
    tn = _tile(n, 1152)
    return pl.pallas_call(
        _adaln_kernel,
        grid=(depth, n // tn),
        in_specs=[pl.BlockSpec((rows, d), lambda l, j: (0, 0)),
                  pl.BlockSpec((1, d, tn), lambda l, j: (l, 0, j)),
                  pl.BlockSpec((1, 1, tn), lambda l, j: (l, 0, j))],
        out_specs=pl.BlockSpec((1, rows, tn), lambda l, j: (l, 0, j)),
        out_shape=jax.ShapeDtypeStruct((depth, rows, n), F32),
        compiler_params=_cparams(("parallel", "parallel")),
        name="adaln",
    )(cond, w_mod, b_mod.reshape(depth, 1, n))


def _ffn_kernel(h_ref, mod_ref, g_ref, win_ref, wout_ref, o_ref, *, slot, d_ff):
    mod = mod_ref[0]
    g = g_ref[...]
    res_gate = mod[3 * slot + 2:3 * slot + 3]
    tm = h_ref.shape[1]
    n_groups = max(1, tm // FFN_GROUP_ROWS)
    rows = tm // n_groups
    for part in range(n_groups):
        x = h_ref[0, part * rows:(part + 1) * rows, :]
        xm = _modulated(x, mod, g, slot).astype(BF16)
        gate = _dot(xm, win_ref[:, :d_ff])
        up = _dot(xm, win_ref[:, d_ff:])
        act = (gate * jax.nn.sigmoid(gate) * up).astype(BF16)
        y = _dot(act, wout_ref[...])
        o_ref[0, part * rows:(part + 1) * rows, :] = (
            x + FFN_RES * res_gate * _rms_rows(y, g[2 * slot + 1:2 * slot + 2]))


def _ffn(h, mod, g, w_in, w_out, slot):
    b, t, d = h.shape
    d_ff = w_out.shape[0]
    tm = _tile(t, FFN_ROW_SPLITS * FFN_GROUP_ROWS)
    return pl.pallas_call(
        functools.partial(_ffn_kernel, slot=slot, d_ff=d_ff),
        grid=(b, t // tm),
        in_specs=[pl.BlockSpec((1, tm, d), lambda bb, i: (bb, i, 0)),
                  _mod_spec(mod),
                  _const_spec(g.shape),
                  _const_spec(w_in.shape),
                  _const_spec(w_out.shape)],
        out_specs=pl.BlockSpec((1, tm, d), lambda bb, i: (bb, i, 0)),
        out_shape=jax.ShapeDtypeStruct(h.shape, F32),
        compiler_params=_cparams(("parallel", "parallel")),
        name="ffn",
    )(h, mod, g, w_in, w_out)


def _even_in_kernel(h_ref, mod_ref, g_ref, w_ref, dft_ref, gb_ref, z_ref, a_ref, *, channel_dft):
    xm = _modulated(h_ref[0], mod_ref[0], g_ref[...], 1).astype(BF16)
    u = _dot(xm, w_ref[...])
    gb_ref[0] = u[:, :D_CONV].astype(BF16)
    z_ref[0] = (u[:, D_CONV:2 * D_CONV] * u[:, 2 * D_CONV:3 * D_CONV]).astype(BF16)
    xf = u[:, 3 * D_CONV:].astype(BF16)
    if not channel_dft:
        a_ref[0] = xf
        return
    for gi in range(FOURIER_GROUPS):
        pq = _dot(xf[:, gi * D_FG:(gi + 1) * D_FG], dft_ref[...])
        a_ref[0, :, gi * D_FG:(gi + 1) * D_FG] = pq[:, :D_FG].astype(BF16)
        a_ref[1, :, gi * D_FG:(gi + 1) * D_FG] = pq[:, D_FG:].astype(BF16)


def _even_in(h, mod, g, w_in, dft_c, channel_dft):
    b, t, d = h.shape
    tm = _tile(t, 512)
    if channel_dft:
        a_spec = pl.BlockSpec((2, tm, D_FOURIER), lambda bb, i: (0, i, bb))
        a_shape = jax.ShapeDtypeStruct((2, t, b * D_FOURIER), BF16)
    else:
        a_spec = pl.BlockSpec((1, tm, D_FOURIER), lambda bb, i: (bb, i, 0))
        a_shape = jax.ShapeDtypeStruct((b, t, D_FOURIER), BF16)
    return pl.pallas_call(
        functools.partial(_even_in_kernel, channel_dft=channel_dft),
        grid=(b, t // tm),
        in_specs=[pl.BlockSpec((1, tm, d), lambda bb, i: (bb, i, 0)),
                  _mod_spec(mod),
                  _const_spec(g.shape),
                  _const_spec(w_in.shape),
                  _const_spec(dft_c.shape)],
        out_specs=[pl.BlockSpec((1, tm, D_CONV), lambda bb, i: (bb, i, 0)),
                   pl.BlockSpec((1, tm, D_CONV), lambda bb, i: (bb, i, 0)),
                   a_spec],
        out_shape=[jax.ShapeDtypeStruct((b, t, D_CONV), BF16),
                   jax.ShapeDtypeStruct((b, t, D_CONV), BF16),
                   a_shape],
        compiler_params=_cparams(("parallel", "parallel")),
        name="even_in",
    )(h, mod, g, w_in, dft_c)


DFT_COLS = 64


def _dft_rows_kernel(x_ref, f_ref, tc_ref, ts_ref, o_ref):
    res = _dot(f_ref[...], x_ref[0])
    r = f_ref.shape[0] // 2
    reps = D_FOURIER // tc_ref.shape[2]
    for c in range(tc_ref.shape[0]):
        a_re = res[:r, c * D_FOURIER:(c + 1) * D_FOURIER]
        a_im = res[r:, c * D_FOURIER:(c + 1) * D_FOURIER]
        tc = jnp.concatenate([tc_ref[c]] * reps, axis=1)
        ts = jnp.concatenate([ts_ref[c]] * reps, axis=1)
        o_ref[0, 0, c] = (a_re * tc + a_im * ts).astype(BF16)
        o_ref[0, 1, c] = (a_im * tc - a_re * ts).astype(BF16)


def _dft_cols_kernel(b_ref, m_ref, cs_ref, o_ref, *, scale):
    z = _dot(m_ref[...], jnp.concatenate([b_ref[0, 0], b_ref[0, 1]], axis=0))
    kb = z.shape[1] // D_FOURIER
    for gi in range(FOURIER_GROUPS):
        lanes = [slice(k * D_FOURIER + gi * D_FG, k * D_FOURIER + (gi + 1) * D_FG) for k in range(kb)]
        z_re = jnp.concatenate([z[:DFT_COLS, ln] for ln in lanes], axis=0)
        z_im = jnp.concatenate([z[DFT_COLS:, ln] for ln in lanes], axis=0)
        y = _dot(jnp.concatenate([z_re, z_im], axis=1).astype(BF16), cs_ref[...]) * scale
        for k in range(kb):
            o_ref[0, :, lanes[k]] = y[k * DFT_COLS:(k + 1) * DFT_COLS].astype(o_ref.dtype)


def _factored_dft(xf, tabs):
    b, t, _ = xf.shape
    r = t // DFT_COLS
    cb = kb = F32_SUBLANES
    rows = pl.pallas_call(
        _dft_rows_kernel,
        grid=(b, DFT_COLS // cb),
        in_specs=[pl.BlockSpec((1, r, cb * D_FOURIER), lambda bb, j: (bb, 0, j)),
                  _const_spec(tabs["f_rows"].shape),
                  pl.BlockSpec((cb, r, HEAD_PAD), lambda bb, j: (j, 0, 0)),
                  pl.BlockSpec((cb, r, HEAD_PAD), lambda bb, j: (j, 0, 0))],
        out_specs=pl.BlockSpec((1, 2, cb, r, D_FOURIER), lambda bb, j: (bb, 0, j, 0, 0)),
        out_shape=jax.ShapeDtypeStruct((b, 2, DFT_COLS, r, D_FOURIER), BF16),
        compiler_params=_cparams(("parallel", "parallel")),
        name="dft_rows",
    )(xf.reshape(b, r, DFT_COLS * D_FOURIER), tabs["f_rows"], tabs["tw_cos"], tabs["tw_sin"])
    out = pl.pallas_call(
        functools.partial(_dft_cols_kernel, scale=1.0 / math.sqrt(t * D_FG)),
        grid=(b, r // kb),
        in_specs=[pl.BlockSpec((1, 2, DFT_COLS, kb * D_FOURIER), lambda bb, j: (bb, 0, 0, j)),
                  _const_spec(tabs["m_cols"].shape),
                  _const_spec(tabs["cs_chan"].shape)],
        out_specs=pl.BlockSpec((1, DFT_COLS, kb * D_FOURIER), lambda bb, j: (bb, 0, j)),
        out_shape=jax.ShapeDtypeStruct((b, DFT_COLS, r * D_FOURIER), BF16),
        compiler_params=_cparams(("parallel", "parallel")),
        name="dft_cols",
    )(rows.reshape(b, 2, DFT_COLS, r * D_FOURIER), tabs["m_cols"], tabs["cs_chan"])
    return out.reshape(b, t, D_FOURIER)


def _cos_sin(num, den):
    ang = (num % den).astype(F32) * (2.0 * math.pi / den)
    return jnp.cos(ang), jnp.sin(ang)


def _factored_dft_tables(t):
    r = t // DFT_COLS
    i_r = jnp.arange(r, dtype=jnp.int32)
    i_c = jnp.arange(DFT_COLS, dtype=jnp.int32)
    i_g = jnp.arange(D_FG, dtype=jnp.int32)
    c_r, s_r = _cos_sin(i_r[:, None] * i_r[None, :], r)
    c_t, s_t = _cos_sin(i_c[:, None] * i_r[None, :], t)
    c_c, s_c = _cos_sin(i_c[:, None] * i_c[None, :], DFT_COLS)
    c_g, s_g = _cos_sin(i_g[:, None] * i_g[None, :], D_FG)
    lanes = lambda a: jnp.broadcast_to(a[:, :, None], (DFT_COLS, r, HEAD_PAD))
    return {
        "f_rows": jnp.concatenate([c_r, -s_r], axis=0).astype(BF16),
        "tw_cos": lanes(c_t), "tw_sin": lanes(s_t),
        "m_cols": jnp.block([[c_c, s_c], [-s_c, c_c]]).astype(BF16),
        "cs_chan": jnp.concatenate([c_g, s_g], axis=0).astype(BF16),
    }


def _matmul_kernel(a_ref, b_ref, o_ref, acc_ref, *, scale):
    k = pl.program_id(2)

    @pl.when(k == 0)
    def _():
        acc_ref[...] = jnp.zeros_like(acc_ref)

    acc_ref[...] += _dot(a_ref[...], b_ref[...])

    @pl.when(k == pl.num_programs(2) - 1)
    def _():
        o_ref[...] = (acc_ref[...] * scale).astype(o_ref.dtype)


def _matmul(a, b, scale, out_dtype):
    m, kk = a.shape
    n = b.shape[1]
    bm, bn, bk = _tile(m, 1024), _tile(n, 1024), _tile(kk, 2048)
    return pl.pallas_call(
        functools.partial(_matmul_kernel, scale=scale),
        grid=(m // bm, n // bn, kk // bk),
        in_specs=[pl.BlockSpec((bm, bk), lambda i, j, k: (i, k)),
                  pl.BlockSpec((bk, bn), lambda i, j, k: (k, j))],
        out_specs=pl.BlockSpec((bm, bn), lambda i, j, k: (i, j)),
        out_shape=jax.ShapeDtypeStruct((m, n), out_dtype),
        scratch_shapes=[pltpu.VMEM((bm, bn), F32)],
        compiler_params=_cparams(("parallel", "parallel", "arbitrary")),
        name="dft_matmul",
    )(a, b)


def _even_out_kernel(h_ref, mod_ref, g_ref, gb_ref, z_ref, zp_ref, zn_ref, yf_ref, cw_ref, w_ref, o_ref):
    i = pl.program_id(1)
    x = h_ref[0]
    mod = mod_ref[0]
    g = g_ref[...]
    z = z_ref[0].astype(F32)
    tm = z.shape[0]
    halo = zp_ref.shape[1]
    prev_row = jnp.where(i > 0, zp_ref[0, halo - 1:halo, :].astype(F32), 0.0)
    next_row = jnp.where(i < pl.num_programs(1) - 1, zn_ref[0, 0:1, :].astype(F32), 0.0)
    row = lax.broadcasted_iota(jnp.int32, z.shape, 0)
    z_before = jnp.where(row == 0, prev_row, pltpu.roll(z, 1, 0))
    z_after = jnp.where(row == tm - 1, next_row, pltpu.roll(z, tm - 1, 0))
    cw = cw_ref[...]
    conv = z_before * cw[0:1] + z * cw[1:2] + z_after * cw[2:3]
    y_conv = (gb_ref[0].astype(F32) * conv).astype(BF16)
    y = _dot(y_conv, w_ref[:D_CONV, :]) + _dot(yf_ref[0], w_ref[D_CONV:, :])
    o_ref[0] = x + mod[5:6] * _rms_rows(y, g[3:4])


def _even_out(h, mod, g, gb, z, yf, conv_w, w_out):
    b, t, d = h.shape
    tm = _tile(t, 512)
    halo = BF16_SUBLANES
    nh = tm // halo
    last_halo = t // halo - 1
    return pl.pallas_call(
        _even_out_kernel,
        grid=(b, t // tm),
        in_specs=[pl.BlockSpec((1, tm, d), lambda bb, i: (bb, i, 0)),
                  _mod_spec(mod),
                  _const_spec(g.shape),
                  pl.BlockSpec((1, tm, D_CONV), lambda bb, i: (bb, i, 0)),
                  pl.BlockSpec((1, tm, D_CONV), lambda bb, i: (bb, i, 0)),
                  pl.BlockSpec((1, halo, D_CONV), lambda bb, i: (bb, jnp.maximum(i * nh - 1, 0), 0)),
                  pl.BlockSpec((1, halo, D_CONV), lambda bb, i: (bb, jnp.minimum((i + 1) * nh, last_halo), 0)),
                  pl.BlockSpec((1, tm, D_FOURIER), lambda bb, i: (bb, i, 0)),
                  _const_spec(conv_w.shape),
                  _const_spec(w_out.shape)],
        out_specs=pl.BlockSpec((1, tm, d), lambda bb, i: (bb, i, 0)),
        out_shape=jax.ShapeDtypeStruct(h.shape, F32),
        compiler_params=_cparams(("parallel", "parallel")),
        name="even_out",
    )(h, mod, g, gb, z, z, z, yf, conv_w, w_out)


def _dft_tables(t):
    n = jnp.arange(t, dtype=jnp.int32)
    ang = ((n[:, None] * n[None, :]) % t).astype(F32) * (2.0 * math.pi / t)
    w_pos = jnp.concatenate([jnp.cos(ang), -jnp.sin(ang)], axis=1).astype(BF16)
    return w_pos


def _channel_dft_table():
    n = jnp.arange(D_FG, dtype=jnp.int32)
    ang = ((n[:, None] * n[None, :]) % D_FG).astype(F32) * (2.0 * math.pi / D_FG)
    return jnp.concatenate([jnp.cos(ang), jnp.sin(ang)], axis=1).astype(BF16)


def _even_mixer(h, mod, g, w_in, conv_w, w_out, dft_c, dft_pos):
    b, t, _ = h.shape
    if isinstance(dft_pos, dict):
        gb, z, xf = _even_in(h, mod, g, w_in, dft_c, False)
        yf = _factored_dft(xf, dft_pos)
    else:
        gb, z, a = _even_in(h, mod, g, w_in, dft_c, True)
        yf = _matmul(dft_pos, a.reshape(2 * t, b * D_FOURIER), 1.0 / math.sqrt(t * D_FG), BF16)
        yf = yf.reshape(t, b, D_FOURIER).transpose(1, 0, 2)
    return _even_out(h, mod, g, gb, z, yf, conv_w, w_out)


def _odd_in_kernel(h_ref, mod_ref, g_ref, wtok_ref, wt_ref, wukp_ref, wuvt_ref, wuqt_ref, place_ref, gsel_ref,
                   gkv_row_ref, gkv_col_ref, gq_col_ref,
                   cosk_ref, sink_ref, tkr_ref, cosq_ref, sinq_ref, cosm_ref, sinm_ref,
                   khd_ref, khm_ref, vtd_ref, vtm_ref, kn_ref, *q_refs, with_q, split8):
    xm = _modulated(h_ref[0], mod_ref[0], g_ref[...], 1).astype(BF16)
    tm = xm.shape[0]

    ut = _dot(xm, wtok_ref[...])
    k = ut[:, :DA_QCOLS]
    ckv = ut[:, DA_QCOLS:DA_QCOLS + MLA_KV_RANK]
    kr2 = ut[:, DA_QCOLS + MLA_KV_RANK:]
    lane = lax.broadcasted_iota(jnp.int32, k.shape, 1)
    first_half = (lane % (DA_DK // 2)) < (DA_DK // 4)
    k_sw = jnp.where(first_half, pltpu.roll(k, DA_QCOLS - DA_DK // 4, 1), pltpu.roll(k, DA_DK // 4, 1))
    reps = DA_QCOLS // HEAD_PAD
    cosk = jnp.concatenate([cosk_ref[...]] * reps, axis=1)
    sink = jnp.concatenate([sink_ref[...]] * reps, axis=1)
    k_rot = k * cosk + k_sw * sink
    ckvn = _rms_rows(ckv, gkv_row_ref[...]).astype(BF16)
    k_nope = _dot(ckvn, wukp_ref[...])
    pr = kr2 * tkr_ref[...]
    pr_hi = pr.astype(BF16)
    pr_lo = (pr - pr_hi.astype(F32)).astype(BF16)
    k_mla = k_nope + _dot(pr_hi, place_ref[...]) + _dot(pr_lo, place_ref[...])
    k_odd = pltpu.roll(k_rot, DA_QCOLS - DA_DK, 1)
    lane_h = lax.broadcasted_iota(jnp.int32, (tm, HEAD_PAD), 1)
    sq = []
    for mp in range(2 * DA_HEADS):
        src = (k_rot if mp % 2 == 0 else k_odd)[:, (mp // 2) * HEAD_PAD:(mp // 2 + 1) * HEAD_PAD]
        kd = jnp.where(lane_h < DA_DK, src, 0.0)
        if split8:
            k_hi, k_lo = _split8(kd * DA_SPLIT_SCALE)
            khd_ref[0, mp] = jnp.where(lane_h < DA_DK, k_hi, pltpu.roll(k_lo, DA_DK, 1)).astype(F8)
            kf = k_hi + k_lo
        else:
            khd_ref[0, mp] = jnp.where(lane_h == DA_DK, 1.0, kd).astype(BF16)
            kf = kd.astype(BF16).astype(F32)
        sq.append((kf * kf).astype(BF16))
    for hh in range(MLA_HEADS):
        kd = k_mla[:, hh * HEAD_PAD:(hh + 1) * HEAD_PAD]
        khm_ref[0, hh] = jnp.where(lane_h == MLA_DQK, 1.0, kd).astype(BF16)
        kf = kd.astype(BF16).astype(F32)
        sq.append((kf * kf).astype(BF16))
    kn_ref[0] = _dot_nt(gsel_ref[...], jnp.concatenate(sq, axis=1))

    r0 = DA_VCOLS
    vckv = _dot_nt(wt_ref[:r0 + MLA_KV_RANK, :], xm)
    for hh in range(DA_HEADS):
        vtd_ref[0, hh] = vckv[hh * DA_DV:(hh + 1) * DA_DV].astype(BF16)
    ckv_t = vckv[r0:]
    ms = jnp.mean(ckv_t * ckv_t, axis=0, keepdims=True)
    ckvn_t = (ckv_t * lax.rsqrt(ms + EPS) * gkv_col_ref[...]).astype(BF16)
    vm_t = _dot(wuvt_ref[...], ckvn_t)
    for hh in range(MLA_HEADS):
        vtm_ref[0, hh] = vm_t[hh * MLA_DV:(hh + 1) * MLA_DV].astype(BF16)

    if with_q:
        qtd_ref, qtm_ref = q_refs
        r1 = r0 + MLA_KV_RANK
        q_t = _dot_nt(wt_ref[r1:, :], xm)
        cosq, sinq = cosq_ref[...], sinq_ref[...]
        qd = DA_DK // 4
        for mp in range(2 * DA_HEADS):
            q = q_t[mp * DA_DK:(mp + 1) * DA_DK]
            q_sw = jnp.concatenate([q[qd:2 * qd], q[:qd], q[3 * qd:], q[2 * qd:3 * qd]], axis=0)
            q_rot = q * cosq + q_sw * sinq
            if split8:
                q_hi, q_lo = _split8(q_rot * DA_SPLIT_SCALE)
                qtd_ref[0, mp, :DA_DK, :] = q_hi.astype(F8)
                qtd_ref[0, mp, DA_DK:, :] = q_lo.astype(F8)
            else:
                qtd_ref[0, mp // 2, (mp % 2) * DA_DK:(mp % 2 + 1) * DA_DK, :] = (
                    q_rot * (DA_SCALE * LOG2E)).astype(BF16)
        cq_t = q_t[DA_QCOLS:]
        ms = jnp.mean(cq_t * cq_t, axis=0, keepdims=True)
        cqn_t = (cq_t * lax.rsqrt(ms + EPS) * gq_col_ref[...]).astype(BF16)
        qm_t = _dot(wuqt_ref[...], cqn_t) * (MLA_SCALE * LOG2E)
        cosm, sinm = cosm_ref[...], sinm_ref[...]
        rd = MLA_ROPE // 4
        for hh in range(MLA_HEADS):
            base = hh * HEAD_PAD
            qtm_ref[0, hh, :MLA_NOPE, :] = qm_t[base:base + MLA_NOPE].astype(BF16)
            r = qm_t[base + MLA_NOPE:base + MLA_DQK]
            r_sw = jnp.concatenate([r[rd:2 * rd], r[:rd], r[3 * rd:], r[2 * rd:3 * rd]], axis=0)
            qtm_ref[0, hh, MLA_NOPE:MLA_DQK, :] = (r * cosm + r_sw * sinm).astype(BF16)
            qtm_ref[0, hh, MLA_DQK:, :] = jnp.zeros((HEAD_PAD - MLA_DQK, tm), BF16)


def _odd_in(h, mod, g, wts, tabs, kv_prev, tok_off, with_q, split8):
    b, t, d = h.shape
    da_dtype = F8 if split8 else BF16
    tk_total = tabs["tk_total"]
    tm = _tile(t, 512)
    assert tok_off % tm == 0
    off = tok_off // tm
    row_tab = lambda w: pl.BlockSpec((tm, w), lambda bb, i: (i, 0))
    col_tab = lambda r: pl.BlockSpec((r, tm), lambda bb, i: (0, i))
    consts = [wts["wtok"], wts["wt"], wts["wukp"], wts["wuvt"], wts["wuqt"], wts["place"], wts["gsel"],
              wts["gkv_row"], wts["gkv_col"], wts["gq_col"]]
    in_specs = ([pl.BlockSpec((1, tm, d), lambda bb, i: (bb, i, 0)), _mod_spec(mod), _const_spec(g.shape)]
                + [_const_spec(c.shape) for c in consts]
                + [row_tab(HEAD_PAD), row_tab(HEAD_PAD), row_tab(2 * MLA_ROPE),
                   col_tab(DA_DK), col_tab(DA_DK), col_tab(MLA_ROPE), col_tab(MLA_ROPE)])
    args = [h, mod, g] + consts + [tabs["cosk"], tabs["sink"], tabs["tkr"],
                                   tabs["cosq"], tabs["sinq"], tabs["cosm"], tabs["sinm"]]
    out_specs = [pl.BlockSpec((1, 2 * DA_HEADS, tm, HEAD_PAD), lambda bb, i: (bb, 0, i + off, 0)),
                 pl.BlockSpec((1, MLA_HEADS, tm, HEAD_PAD), lambda bb, i: (bb, 0, i + off, 0)),
                 pl.BlockSpec((1, DA_HEADS, DA_DV, tm), lambda bb, i: (bb, 0, 0, i + off)),
                 pl.BlockSpec((1, MLA_HEADS, MLA_DV, tm), lambda bb, i: (bb, 0, 0, i + off)),
                 pl.BlockSpec((1, N_MAPS, tm), lambda bb, i: (bb, 0, i + off))]
    out_shape = [jax.ShapeDtypeStruct((b, 2 * DA_HEADS, tk_total, HEAD_PAD), da_dtype),
                 jax.ShapeDtypeStruct((b, MLA_HEADS, tk_total, HEAD_PAD), BF16),
                 jax.ShapeDtypeStruct((b, DA_HEADS, DA_DV, tk_total), BF16),
                 jax.ShapeDtypeStruct((b, MLA_HEADS, MLA_DV, tk_total), BF16),
                 jax.ShapeDtypeStruct((b, N_MAPS, tk_total), F32)]
    if with_q:
        n_qd = 2 * DA_HEADS if split8 else DA_HEADS
        out_specs += [pl.BlockSpec((1, n_qd, HEAD_PAD, tm), lambda bb, i: (bb, 0, 0, i)),
                      pl.BlockSpec((1, MLA_HEADS, HEAD_PAD, tm), lambda bb, i: (bb, 0, 0, i))]
        out_shape += [jax.ShapeDtypeStruct((b, n_qd, HEAD_PAD, t), da_dtype),
                      jax.ShapeDtypeStruct((b, MLA_HEADS, HEAD_PAD, t), BF16)]
    aliases = {}
    kernel_fn = functools.partial(_odd_in_kernel, with_q=with_q, split8=split8)
    if kv_prev is not None:
        n_in = len(args)
        in_specs += [pl.BlockSpec(memory_space=pl.ANY)] * N_KV_ARRAYS
        args += list(kv_prev)
        aliases = {n_in + a: a for a in range(N_KV_ARRAYS)}
        kernel_fn = functools.partial(_odd_in_alias_kernel, n_in=n_in, with_q=with_q, split8=split8)
    return pl.pallas_call(
        kernel_fn,
        grid=(b, t // tm),
        in_specs=in_specs,
        out_specs=out_specs,
        out_shape=out_shape,
        input_output_aliases=aliases,
        compiler_params=_cparams(("parallel", "parallel")),
        name="odd_in",
    )(*args)


def _odd_in_alias_kernel(*refs, n_in, with_q, split8):
    _odd_in_kernel(*refs[:n_in], *refs[n_in + N_KV_ARRAYS:], with_q=with_q, split8=split8)


def _attn_kernel(lam_ref, gsub_ref, kn_ref, qtd_ref, qtm_ref, khd_ref, khm_ref, vtd_ref, vtm_ref,
                 otd_ref, otm_ref, lmin_ref, qad_ref, qam_ref, shift_ref, m_ref, l_ref, accd_ref, accm_ref,
                 *, lam_init, safe):
    j = pl.program_id(2)
    tq = qtd_ref.shape[-1]

    @pl.when(j == 0)
    def _():
        m_ref[...] = jnp.full_like(m_ref, NEG_BIG)
        l_ref[...] = jnp.zeros_like(l_ref)
        accd_ref[...] = jnp.zeros_like(accd_ref)
        accm_ref[...] = jnp.zeros_like(accm_ref)
        kmax = jnp.sqrt(jnp.max(kn_ref[0], axis=1, keepdims=True))
        shift_row = lax.broadcasted_iota(jnp.int32, (BF16_SUBLANES, tq), 0) == 0
        for idx in range(2 * DA_HEADS):
            if safe:
                q = qtd_ref[0, idx // 2, (idx % 2) * DA_DK:(idx % 2 + 1) * DA_DK, :]
                qad_ref[idx] = jnp.concatenate([q, jnp.zeros((HEAD_PAD - DA_DK, tq), BF16)], axis=0)
            else:
                q_hi, q_lo = qtd_ref[0, idx, :DA_DK, :], qtd_ref[0, idx, DA_DK:, :]
                qad_ref[idx] = jnp.concatenate([q_hi, q_hi, q_lo, q_lo], axis=0)
                qf = q_hi.astype(F32) + q_lo.astype(F32)
                qn = jnp.sqrt(jnp.sum(qf * qf, axis=0, keepdims=True))
                ok = (qn < F8_SAFE_MAX) & (kmax[idx:idx + 1] < F8_SAFE_MAX)
                shift_ref[idx] = jnp.where(ok, (BOUND_SLACK * kmax[idx:idx + 1]) * qn, jnp.inf)
        for hh in range(MLA_HEADS):
            idx = 2 * DA_HEADS + hh
            q = qtm_ref[0, hh, :MLA_DQK, :]
            if safe:
                shift_blk = jnp.zeros((BF16_SUBLANES, tq), BF16)
            else:
                qf = q.astype(F32)
                qn = jnp.sqrt(jnp.sum(qf * qf, axis=0, keepdims=True))
                shift = -(BOUND_SLACK * kmax[idx:idx + 1]) * qn
                shift_blk = jnp.where(shift_row, shift, 0.0).astype(BF16)
            pad = jnp.zeros((HEAD_PAD - MLA_DQK - BF16_SUBLANES, tq), BF16)
            qam_ref[hh] = jnp.concatenate([q, shift_blk, pad], axis=0)

    def softmax_pv(s, ln, idx, v_t, acc_ref, acc_idx):
        if safe:
            m_old = m_ref[idx, :, ln]
            m_new = jnp.maximum(m_old, jnp.max(s, axis=0, keepdims=True))
            alpha = jnp.exp2(m_old - m_new)
            m_ref[idx, :, ln] = m_new
            p = jnp.exp2(s - m_new)
            l_ref[idx, :, ln] = alpha * l_ref[idx, :, ln] + jnp.sum(p.reshape(-1, F32_SUBLANES, s.shape[1]), axis=0)
            acc_ref[acc_idx, :, ln] = acc_ref[acc_idx, :, ln] * alpha + _dot(v_t, p.astype(BF16))
        else:
            p = jnp.exp2(s)
            l_ref[idx, :, ln] += jnp.sum(p.reshape(-1, F32_SUBLANES, s.shape[1]), axis=0)
            acc_ref[acc_idx, :, ln] += _dot(v_t, p.astype(BF16))

    chunk = ATTN_Q_CHUNK if (not safe and tq % ATTN_Q_CHUNK == 0) else tq
    q_chunks = [slice(c, c + chunk) for c in range(0, tq, chunk)]
    da_per_iter = 1 if safe else DA_HEADS_PER_ITER
    mla_per_iter = 1 if safe else MLA_HEADS_PER_ITER

    def da_score(hh, m, ln):
        keys = khd_ref[0, 2 * hh + m]
        if safe:
            return _dot(keys, qad_ref[2 * hh + m, :, ln])
        return _dot(jnp.concatenate([keys, keys], axis=1), qad_ref[2 * hh + m, :, ln])

    def mla_score(hh, ln):
        return _dot(khm_ref[0, hh], qam_ref[hh, :, ln])

    def run_maps(scores, consume):
        s_next = scores[0]()
        for n in range(len(scores)):
            s_cur = s_next
            if n + 1 < len(scores):
                s_next = scores[n + 1]()
            consume[n](s_cur)

    def da_units(hh, scores, consume):
        for m in range(2):
            for ln in q_chunks:
                scores.append(functools.partial(da_score, hh, m, ln))
                consume.append(lambda s, m=m, ln=ln: softmax_pv(
                    s if safe else s - shift_ref[2 * hh + m, :, ln], ln, 2 * hh + m, vtd_ref[0, hh],
                    accd_ref, 2 * hh + m))

    def mla_units(hh, scores, consume):
        mchunk = MLA_Q_CHUNK if (not safe and tq % MLA_Q_CHUNK == 0) else tq
        for c in range(0, tq, mchunk):
            ln = slice(c, c + mchunk)
            scores.append(functools.partial(mla_score, hh, ln))
            consume.append(lambda s, ln=ln: softmax_pv(s, ln, 2 * DA_HEADS + hh, vtm_ref[0, hh], accm_ref, hh))

    def da_body(it, carry):
        scores, consume = [], []
        for u in range(da_per_iter):
            da_units(it * da_per_iter + u, scores, consume)
        run_maps(scores, consume)
        return carry

    def mla_body(it, carry):
        scores, consume = [], []
        for u in range(mla_per_iter):
            mla_units(it * mla_per_iter + u, scores, consume)
        run_maps(scores, consume)
        return carry

    lax.fori_loop(0, DA_HEADS // da_per_iter, da_body, 0)
    lax.fori_loop(0, MLA_HEADS // mla_per_iter, mla_body, 0)

    @pl.when(j == pl.num_programs(2) - 1)
    def _():
        lp = lam_ref[...]
        lam = (jnp.exp(jnp.sum(lp[0:1] * lp[1:2], axis=1, keepdims=True))
               - jnp.exp(jnp.sum(lp[2:3] * lp[3:4], axis=1, keepdims=True)) + lam_init)
        gsub = gsub_ref[...]
        lsum = [jnp.sum(l_ref[idx], axis=0, keepdims=True) for idx in range(N_MAPS)]
        linv = [1.0 / ls for ls in lsum]
        for hh in range(DA_HEADS):
            o = accd_ref[2 * hh] * linv[2 * hh] - accd_ref[2 * hh + 1] * (lam * linv[2 * hh + 1])
            ms = jnp.mean(o * o, axis=0, keepdims=True)
            otd_ref[0, hh] = (o * (lax.rsqrt(ms + EPS) * (1.0 - lam_init)) * gsub).astype(otd_ref.dtype)
        for hh in range(MLA_HEADS):
            otm_ref[0, hh] = (accm_ref[hh] * linv[2 * DA_HEADS + hh]).astype(otm_ref.dtype)
        lmin = lsum[0]
        for idx in range(1, N_MAPS):
            lmin = jnp.minimum(lmin, lsum[idx])
        lmin_ref[0, 0] = jnp.broadcast_to(jnp.min(lmin, axis=1, keepdims=True), lmin_ref.shape[2:])


def _attention_call(lam_p, gsub_col, qtd, qtm, kv, lam_init, tq, tk, kv_off, n_kv, safe):
    khd, khm, vtd, vtm, kn = kv
    b, n_qd, _, t = qtd.shape
    nq = t // tq
    assert kv_off % n_kv == 0
    qad_scratch = (pltpu.VMEM((2 * DA_HEADS, HEAD_PAD, tq), BF16) if safe
                   else pltpu.VMEM((2 * DA_HEADS, 4 * DA_DK, tq), F8))
    return pl.pallas_call(
        functools.partial(_attn_kernel, lam_init=lam_init, safe=safe),
        grid=(b, nq, n_kv),
        in_specs=[_const_spec(lam_p.shape),
                  _const_spec(gsub_col.shape),
                  pl.BlockSpec((1, N_MAPS, n_kv * tk), lambda bb, i, j: (bb, 0, kv_off // n_kv)),
                  pl.BlockSpec((1, n_qd, HEAD_PAD, tq), lambda bb, i, j: (bb, 0, 0, i)),
                  pl.BlockSpec((1, MLA_HEADS, HEAD_PAD, tq), lambda bb, i, j: (bb, 0, 0, i)),
                  pl.BlockSpec((1, 2 * DA_HEADS, tk, HEAD_PAD), lambda bb, i, j: (bb, 0, kv_off + j, 0)),
                  pl.BlockSpec((1, MLA_HEADS, tk, HEAD_PAD), lambda bb, i, j: (bb, 0, kv_off + j, 0)),
                  pl.BlockSpec((1, DA_HEADS, DA_DV, tk), lambda bb, i, j: (bb, 0, 0, kv_off + j)),
                  pl.BlockSpec((1, MLA_HEADS, MLA_DV, tk), lambda bb, i, j: (bb, 0, 0, kv_off + j))],
        out_specs=[pl.BlockSpec((1, DA_HEADS, DA_DV, tq), lambda bb, i, j: (bb, 0, 0, i)),
                   pl.BlockSpec((1, MLA_HEADS, MLA_DV, tq), lambda bb, i, j: (bb, 0, 0, i)),
                   pl.BlockSpec((1, 1, F32_SUBLANES, HEAD_PAD), lambda bb, i, j: (bb, i, 0, 0))],
        out_shape=[jax.ShapeDtypeStruct((b, DA_HEADS, DA_DV, t), BF16),
                   jax.ShapeDtypeStruct((b, MLA_HEADS, MLA_DV, t), BF16),
                   jax.ShapeDtypeStruct((b, nq, F32_SUBLANES, HEAD_PAD), F32)],
        scratch_shapes=[qad_scratch,
                        pltpu.VMEM((MLA_HEADS, HEAD_PAD, tq), BF16),
                        pltpu.VMEM((2 * DA_HEADS, 1, tq), F32),
                        pltpu.VMEM((N_MAPS, 1, tq), F32),
                        pltpu.VMEM((N_MAPS, F32_SUBLANES, tq), F32),
                        pltpu.VMEM((2 * DA_HEADS, DA_DV, tq), F32),
                        pltpu.VMEM((MLA_HEADS, MLA_DV, tq), F32)],
        compiler_params=_cparams(("parallel", "parallel", "arbitrary")),
        name="attention_safe" if safe else "attention",
    )(lam_p, gsub_col, kn, qtd, qtm, khd, khm, vtd, vtm)


def _odd_out_kernel(h_ref, mod_ref, g_ref, otd_ref, otm_ref, wt_ref, o_ref):
    y_t = _dot(wt_ref[:, :DA_VCOLS], otd_ref[0]) + _dot(wt_ref[:, DA_VCOLS:], otm_ref[0])
    y = y_t.T
    o_ref[0] = h_ref[0] + mod_ref[0][5:6] * _rms_rows(y, g_ref[3:4])


def _odd_out(h, mod, g, otd, otm, w_out_t):
    b, t, d = h.shape
    tm = _tile(t, 512)
    otd = otd.reshape(b, DA_VCOLS, t)
    otm = otm.reshape(b, MLA_HEADS * MLA_DV, t)
    return pl.pallas_call(
        _odd_out_kernel,
        grid=(b, t // tm),
        in_specs=[pl.BlockSpec((1, tm, d), lambda bb, i: (bb, i, 0)),
                  _mod_spec(mod),
                  _const_spec(g.shape),
                  pl.BlockSpec((1, DA_VCOLS, tm), lambda bb, i: (bb, 0, i)),
                  pl.BlockSpec((1, MLA_HEADS * MLA_DV, tm), lambda bb, i: (bb, 0, i)),
                  _const_spec(w_out_t.shape)],
        out_specs=pl.BlockSpec((1, tm, d), lambda bb, i: (bb, i, 0)),
        out_shape=jax.ShapeDtypeStruct(h.shape, F32),
        compiler_params=_cparams(("parallel", "parallel")),
        name="odd_out",
    )(h, mod, g, otd, otm, w_out_t)


def _rope_angles(n_tokens, rot_dim):
    rows = n_tokens // GRID_W
    row = jnp.broadcast_to(jnp.arange(rows)[:, None], (rows, GRID_W)).reshape(-1).astype(F32)
    col = jnp.broadcast_to(jnp.arange(GRID_W)[None, :], (rows, GRID_W)).reshape(-1).astype(F32)
    n_freq = rot_dim // 4
    freqs = ROPE_THETA ** (-jnp.arange(n_freq, dtype=F32) / n_freq)
    return row[:, None] * freqs, col[:, None] * freqs


def _rope_cos_sin(n_tokens, rot_dim, identity):
    if identity:
        return jnp.ones((n_tokens, rot_dim), F32), jnp.zeros((n_tokens, rot_dim), F32)
    ar, ac = _rope_angles(n_tokens, rot_dim)
    cos = jnp.concatenate([jnp.cos(ar), jnp.cos(ar), jnp.cos(ac), jnp.cos(ac)], axis=1)
    sin = jnp.concatenate([-jnp.sin(ar), jnp.sin(ar), -jnp.sin(ac), jnp.sin(ac)], axis=1)
    return cos, sin


def _rope_tables(n_tokens, identity, tk_total):
    cd, sd = _rope_cos_sin(n_tokens, DA_DK, identity)
    cm, sm = _rope_cos_sin(n_tokens, MLA_ROPE, identity)
    return {
        "cosk": jnp.concatenate([cd, cd], axis=1), "sink": jnp.concatenate([sd, sd], axis=1),
        "tkr": jnp.concatenate([cm, sm], axis=1),
        "cosq": cd.T, "sinq": sd.T, "cosm": cm.T, "sinm": sm.T,
        "tk_total": tk_total,
    }


def _swap_perm(rot_dim):
    q = rot_dim // 4
    return jnp.concatenate([jnp.arange(q, 2 * q), jnp.arange(0, q), jnp.arange(3 * q, 4 * q), jnp.arange(2 * q, 3 * q)])


def _odd_weights(w_in, g_q, w_uq, g_kv, w_uk, w_uv):
    w_q = w_in[:, :DA_QCOLS]
    w_cq = w_in[:, DA_QCOLS:Q_COLS]
    w_k = w_in[:, Q_COLS:Q_COLS + DA_QCOLS]
    w_v = w_in[:, Q_COLS + DA_QCOLS:Q_COLS + DA_QCOLS + DA_VCOLS]
    w_ckv = w_in[:, Q_COLS + DA_QCOLS + DA_VCOLS:Q_COLS + DA_QCOLS + DA_VCOLS + MLA_KV_RANK]
    w_kr = w_in[:, Q_COLS + DA_QCOLS + DA_VCOLS + MLA_KV_RANK:]
    wtok = jnp.concatenate([w_k, w_ckv, w_kr, w_kr[:, _swap_perm(MLA_ROPE)]], axis=1).astype(BF16)
    wt = jnp.concatenate([w_v, w_ckv, w_q, w_cq], axis=1).T.astype(BF16)
    pad_k = jnp.zeros((MLA_KV_RANK, MLA_HEADS, HEAD_PAD), F32)
    wukp = pad_k.at[:, :, :MLA_NOPE].set(w_uk.reshape(MLA_KV_RANK, MLA_HEADS, MLA_NOPE))
    wukp = wukp.reshape(MLA_KV_RANK, MLA_HEADS * HEAD_PAD).astype(BF16)
    pad_q = jnp.zeros((MLA_Q_RANK, MLA_HEADS, HEAD_PAD), F32)
    wuqp = pad_q.at[:, :, :MLA_DQK].set(w_uq.reshape(MLA_Q_RANK, MLA_HEADS, MLA_DQK))
    wuqt = wuqp.reshape(MLA_Q_RANK, MLA_HEADS * HEAD_PAD).T.astype(BF16)
    eye = jnp.eye(MLA_ROPE, dtype=F32)
    place = jnp.zeros((2, MLA_ROPE, MLA_HEADS, HEAD_PAD), F32)
    place = place.at[:, :, :, MLA_NOPE:MLA_DQK].set(jnp.broadcast_to(eye[None, :, None, :], (2, MLA_ROPE, MLA_HEADS, MLA_ROPE)))
    place = place.reshape(2 * MLA_ROPE, MLA_HEADS * HEAD_PAD).astype(BF16)
    return {
        "wtok": wtok, "wt": wt, "wukp": wukp, "wuvt": w_uv.T.astype(BF16), "wuqt": wuqt, "place": place,
        "gsel": jnp.kron(jnp.eye(N_MAPS, dtype=F32), jnp.ones((1, HEAD_PAD), F32)).astype(BF16),
        "gkv_row": g_kv.reshape(1, -1).astype(F32), "gkv_col": g_kv.reshape(-1, 1).astype(F32),
        "gq_col": g_q.reshape(-1, 1).astype(F32),
    }


def kernel(x, c, ctx, c_ctx, w_mod, b_mod, norm_g, w_ffn_in, w_ffn_out, w_in_even, conv_w, w_out_even,
           w_in_odd, g_q_mla, w_uq, g_kv_mla, w_uk, w_uv, lam_q1, lam_k1, lam_q2, lam_k2, g_subln, w_out_odd):
    b, t, d = x.shape
    tc = ctx.shape[1]
    depth = w_mod.shape[0]
    tk_total = t + tc

    rows = -(-(b + 1) // F32_SUBLANES) * F32_SUBLANES
    cond = jnp.zeros((rows, d), F32).at[:b].set(c).at[b].set(c_ctx)
    mod = _adaln(cond, w_mod, b_mod).reshape(depth, rows, N_MOD, d)

    dft_c = _channel_dft_table()
    factored = lambda n: n % (DFT_COLS * BF16_SUBLANES) == 0
    w_pos_x = _factored_dft_tables(t) if factored(t) else _dft_tables(t)
    w_pos_c = _factored_dft_tables(tc) if factored(tc) else _dft_tables(tc)
    tabs_x = _rope_tables(t, False, tk_total)
    tabs_c = _rope_tables(tc, True, tk_total)
    tq = _tile(t, 1024)
    tk = 768 if tk_total % 768 == 0 else _tile(tk_total, 512)
    assert t % tc == 0 and tk_total % tk == 0

    h, hc = x, ctx
    for l in range(depth):
        last = l == depth - 1
        odd = l % 2 == 1
        ctx_live = (not last) or odd
        g = norm_g[l]
        m_x, m_c = mod[l, :b], mod[l, b:b + 1]
        wi0, wo0 = w_ffn_in[l, 0].astype(BF16), w_ffn_out[l, 0].astype(BF16)
        h = _ffn(h, m_x, g, wi0, wo0, 0)
        if ctx_live:
            hc = _ffn(hc, m_c, g, wi0, wo0, 0)

        if not odd:
            e = l // 2
            w_in, w_out = w_in_even[e].astype(BF16), w_out_even[e].astype(BF16)
            h = _even_mixer(h, m_x, g, w_in, conv_w[e], w_out, dft_c, w_pos_x)
            if ctx_live:
                hc = _even_mixer(hc, m_c, g, w_in, conv_w[e], w_out, dft_c, w_pos_c)
        else:
            o = l // 2
            lam_init = 0.8 - 0.6 * math.exp(-0.3 * l)
            wts = _odd_weights(w_in_odd[o], g_q_mla[o], w_uq[o], g_kv_mla[o], w_uk[o], w_uv[o])
            lam_p = jnp.stack([lam_q1[o], lam_k1[o], lam_q2[o], lam_k2[o]]).astype(F32)
            gsub_col = g_subln[o].reshape(-1, 1).astype(F32)
            w_out_t = w_out_odd[o].T.astype(BF16)
            def latent_attention(safe, h=h, hc=hc, m_x=m_x, m_c=m_c, g=g, wts=wts, lam_p=lam_p,
                                 gsub_col=gsub_col, lam_init=lam_init):
                outs_x = _odd_in(h, m_x, g, wts, tabs_x, None, 0, True, not safe)
                kv = _odd_in(hc, m_c, g, wts, tabs_c, outs_x[:N_KV_ARRAYS], t, False, not safe)
                return _attention_call(lam_p, gsub_col, *outs_x[N_KV_ARRAYS:], kv, lam_init, tq, tk, 0,
                                       tk_total // tk, safe)

            otd, otm, lmin = latent_attention(False)
            otd, otm = lax.cond(jnp.min(lmin) >= MIN_DENOMINATOR, lambda: (otd, otm),
                                lambda: tuple(latent_attention(True)[:2]))
            if not last:
                outs_c = _odd_in(hc, m_c, g, wts, dict(tabs_c, tk_total=tc), None, 0, True, False)
                ocd, ocm = _attention_call(lam_p, gsub_col, *outs_c[N_KV_ARRAYS:], outs_c[:N_KV_ARRAYS],
                                           lam_init, tc, tc, 0, 1, True)[:2]
                hc = _odd_out(hc, m_c, g, ocd, ocm, w_out_t)
            h = _odd_out(h, m_x, g, otd, otm, w_out_t)

        wi1, wo1 = w_ffn_in[l, 1].astype(BF16), w_ffn_out[l, 1].astype(BF16)
        h = _ffn(h, m_x, g, wi1, wo1, 2)
        if not last:
            hc = _ffn(hc, m_c, g, wi1, wo1, 2)
    return h
```

```python
import functools
import math

import jax
import jax.numpy as jnp
from jax import lax
from jax.experimental import pallas as pl
from jax.experimental.pallas import tpu as pltpu

F32 = jnp.float32
BF16 = jnp.bfloat16

D_MODEL = 1024
GRID_W = 64
N_MOD = 9
FFN_RES = 0.5
EPS = 1e-6
ROPE_THETA = 10000.0

D_CONV = 512
D_FOURIER = 512
FOURIER_GROUPS = 4
D_FG = D_FOURIER // FOURIER_GROUPS

DA_HEADS = 8
DA_DK = 64
DA_DV = 128
DA_SCALE = DA_DK ** -0.5
MLA_HEADS = 8
MLA_NOPE = 64
MLA_ROPE = 32
MLA_DQK = MLA_NOPE + MLA_ROPE
MLA_DV = 64
MLA_Q_RANK = 384
MLA_KV_RANK = 256
MLA_SCALE = MLA_DQK ** -0.5
DA_QCOLS = DA_HEADS * 2 * DA_DK
DA_VCOLS = DA_HEADS * DA_DV
Q_COLS = DA_QCOLS + MLA_Q_RANK
HEAD_PAD = 128
F32_SUBLANES = 8
BF16_SUBLANES = 16
N_MAPS = 2 * DA_HEADS + MLA_HEADS
LOG2E = 1.4426950408889634
NEG_BIG = -1e30
N_KV_ARRAYS = 5
F8 = jnp.float8_e4m3fn
F8_SAFE_MAX = 400.0
DA_SPLIT_SCALE = math.sqrt(DA_SCALE * LOG2E)
FFN_ROW_SPLITS = 2
FFN_GROUP_ROWS = 256
DA_HEADS_PER_ITER = 4
MLA_HEADS_PER_ITER = 8
ATTN_Q_CHUNK = 256
MLA_Q_CHUNK = 512
BOUND_SLACK = 1.02
MIN_DENOMINATOR = 2.0 ** -40

VMEM_LIMIT_V7X = 56 * 1024 * 1024


def _cparams(sem):
    return pltpu.CompilerParams(dimension_semantics=sem, vmem_limit_bytes=VMEM_LIMIT_V7X)


def _tile(n, pref):
    if n <= pref:
        return n
    t = pref - pref % 128
    while t >= 128:
        if n % t == 0:
            return t
        t -= 128
    return n


def _const_spec(shape):
    nd = len(shape)
    return pl.BlockSpec(shape, lambda *_: (0,) * nd, pipeline_mode=pl.Buffered(1))


def _mod_spec(mod):
    if mod.shape[0] == 1:
        return pl.BlockSpec((1, N_MOD, D_MODEL), lambda b, *_: (0, 0, 0))
    return pl.BlockSpec((1, N_MOD, D_MODEL), lambda b, *_: (b, 0, 0))


def _rms_rows(x, g):
    ms = jnp.mean(x * x, axis=-1, keepdims=True)
    return x * lax.rsqrt(ms + EPS) * g


def _modulated(x, mod, g, slot):
    shift = mod[3 * slot:3 * slot + 1]
    scale = mod[3 * slot + 1:3 * slot + 2]
    return _rms_rows(x, g[2 * slot:2 * slot + 1]) * (1.0 + scale) + shift


def _split8(x):
    hi = x.astype(F8).astype(F32)
    lo = (x - hi).astype(F8).astype(F32)
    return hi, lo


def _dot(a, b):
    return jnp.dot(a, b, preferred_element_type=F32)


def _dot_nt(a, b):
    return lax.dot_general(a, b, (((1,), (1,)), ((), ())), preferred_element_type=F32)


def _adaln_kernel(c_ref, w_ref, b_ref, o_ref):
    c = c_ref[...]
    a = c * jax.nn.sigmoid(c)
    o_ref[0] = jnp.dot(a, w_ref[0], preferred_element_type=F32,
                       precision=lax.Precision.HIGHEST) + b_ref[0]


def _adaln(cond, w_mod, b_mod):
    depth, d, n = w_mod.shape
    rows = cond.shape[0]
    tn = _tile(n, 1152)
    return pl.pallas_call(
        _adaln_kernel,
        grid=(depth, n // tn),
        in_specs=[pl.BlockSpec((rows, d), lambda l, j: (0, 0)),
                  pl.BlockSpec((1, d, tn), lambda l, j: (l, 0, j)),
                  pl.BlockSpec((1, 1, tn), lambda l, j: (l, 0, j))],
        out_specs=pl.BlockSpec((1, rows, tn), lambda l, j: (l, 0, j)),
        out_shape=jax.ShapeDtypeStruct((depth, rows, n), F32),
        compiler_params=_cparams(("parallel", "parallel")),
        name="adaln",
    )(cond, w_mod, b_mod.reshape(depth, 1, n))


def _ffn_kernel(h_ref, mod_ref, g_ref, win_ref, wout_ref, o_ref, *, slot, d_ff):
    mod = mod_ref[0]
    g = g_ref[...]
    res_gate = mod[3 * slot + 2:3 * slot + 3]
    tm = h_ref.shape[1]
    n_groups = max(1, tm // FFN_GROUP_ROWS)
    rows = tm // n_groups
    for part in range(n_groups):
        x = h_ref[0, part * rows:(part + 1) * rows, :]
        xm = _modulated(x, mod, g, slot).astype(BF16)
        gate = _dot(xm, win_ref[:, :d_ff])
        up = _dot(xm, win_ref[:, d_ff:])
        act = (gate * jax.nn.sigmoid(gate) * up).astype(BF16)
        y = _dot(act, wout_ref[...])
        o_ref[0, part * rows:(part + 1) * rows, :] = (
            x + FFN_RES * res_gate * _rms_rows(y, g[2 * slot + 1:2 * slot + 2]))


def _ffn(h, mod, g, w_in, w_out, slot):
    b, t, d = h.shape
    d_ff = w_out.shape[0]
    tm = _tile(t, FFN_ROW_SPLITS * FFN_GROUP_ROWS)
    return pl.pallas_call(
        functools.partial(_ffn_kernel, slot=slot, d_ff=d_ff),
        grid=(b, t // tm),
        in_specs=[pl.BlockSpec((1, tm, d), lambda bb, i: (bb, i, 0)),
                  _mod_spec(mod),
                  _const_spec(g.shape),
                  _const_spec(w_in.shape),
                  _const_spec(w_out.shape)],
        out_specs=pl.BlockSpec((1, tm, d), lambda bb, i: (bb, i, 0)),
        out_shape=jax.ShapeDtypeStruct(h.shape, F32),
        compiler_params=_cparams(("parallel", "parallel")),
        name="ffn",
    )(h, mod, g, w_in, w_out)


def _even_in_kernel(h_ref, mod_ref, g_ref, w_ref, dft_ref, gb_ref, z_ref, a_ref, *, channel_dft):
    xm = _modulated(h_ref[0], mod_ref[0], g_ref[...], 1).astype(BF16)
    u = _dot(xm, w_ref[...])
    gb_ref[0] = u[:, :D_CONV].astype(BF16)
    z_ref[0] = (u[:, D_CONV:2 * D_CONV] * u[:, 2 * D_CONV:3 * D_CONV]).astype(BF16)
    xf = u[:, 3 * D_CONV:].astype(BF16)
    if not channel_dft:
        a_ref[0] = xf
        return
    for gi in range(FOURIER_GROUPS):
        pq = _dot(xf[:, gi * D_FG:(gi + 1) * D_FG], dft_ref[...])
        a_ref[0, :, gi * D_FG:(gi + 1) * D_FG] = pq[:, :D_FG].astype(BF16)
        a_ref[1, :, gi * D_FG:(gi + 1) * D_FG] = pq[:, D_FG:].astype(BF16)


def _even_in(h, mod, g, w_in, dft_c, channel_dft):
    b, t, d = h.shape
    tm = _tile(t, 512)
    if channel_dft:
        a_spec = pl.BlockSpec((2, tm, D_FOURIER), lambda bb, i: (0, i, bb))
        a_shape = jax.ShapeDtypeStruct((2, t, b * D_FOURIER), BF16)
    else:
        a_spec = pl.BlockSpec((1, tm, D_FOURIER), lambda bb, i: (bb, i, 0))
        a_shape = jax.ShapeDtypeStruct((b, t, D_FOURIER), BF16)
    return pl.pallas_call(
        functools.partial(_even_in_kernel, channel_dft=channel_dft),
        grid=(b, t // tm),
        in_specs=[pl.BlockSpec((1, tm, d), lambda bb, i: (bb, i, 0)),
                  _mod_spec(mod),
                  _const_spec(g.shape),
                  _const_spec(w_in.shape),
                  _const_spec(dft_c.shape)],
        out_specs=[pl.BlockSpec((1, tm, D_CONV), lambda bb, i: (bb, i, 0)),
                   pl.BlockSpec((1, tm, D_CONV), lambda bb, i: (bb, i, 0)),
                   a_spec],
        out_shape=[jax.ShapeDtypeStruct((b, t, D_CONV), BF16),
                   jax.ShapeDtypeStruct((b, t, D_CONV), BF16),
                   a_shape],
        compiler_params=_cparams(("parallel", "parallel")),
        name="even_in",
    )(h, mod, g, w_in, dft_c)


DFT_COLS = 64


def _dft_rows_kernel(x_ref, f_ref, tc_ref, ts_ref, o_ref):
    res = _dot(f_ref[...], x_ref[0])
    r = f_ref.shape[0] // 2
    reps = D_FOURIER // tc_ref.shape[2]
    for c in range(tc_ref.shape[0]):
        a_re = res[:r, c * D_FOURIER:(c + 1) * D_FOURIER]
        a_im = res[r:, c * D_FOURIER:(c + 1) * D_FOURIER]
        tc = jnp.concatenate([tc_ref[c]] * reps, axis=1)
        ts = jnp.concatenate([ts_ref[c]] * reps, axis=1)
        o_ref[0, 0, c] = (a_re * tc + a_im * ts).astype(BF16)
        o_ref[0, 1, c] = (a_im * tc - a_re * ts).astype(BF16)


def _dft_cols_kernel(b_ref, m_ref, cs_ref, o_ref, *, scale):
    z = _dot(m_ref[...], jnp.concatenate([b_ref[0, 0], b_ref[0, 1]], axis=0))
    kb = z.shape[1] // D_FOURIER
    for gi in range(FOURIER_GROUPS):
        lanes = [slice(k * D_FOURIER + gi * D_FG, k * D_FOURIER + (gi + 1) * D_FG) for k in range(kb)]
        z_re = jnp.concatenate([z[:DFT_COLS, ln] for ln in lanes], axis=0)
        z_im = jnp.concatenate([z[DFT_COLS:, ln] for ln in lanes], axis=0)
        y = _dot(jnp.concatenate([z_re, z_im], axis=1).astype(BF16), cs_ref[...]) * scale
        for k in range(kb):
            o_ref[0, :, lanes[k]] = y[k * DFT_COLS:(k + 1) * DFT_COLS].astype(o_ref.dtype)


def _factored_dft(xf, tabs):
    b, t, _ = xf.shape
    r = t // DFT_COLS
    cb = kb = F32_SUBLANES
    rows = pl.pallas_call(
        _dft_rows_kernel,
        grid=(b, DFT_COLS // cb),
        in_specs=[pl.BlockSpec((1, r, cb * D_FOURIER), lambda bb, j: (bb, 0, j)),
                  _const_spec(tabs["f_rows"].shape),
                  pl.BlockSpec((cb, r, HEAD_PAD), lambda bb, j: (j, 0, 0)),
                  pl.BlockSpec((cb, r, HEAD_PAD), lambda bb, j: (j, 0, 0))],
        out_specs=pl.BlockSpec((1, 2, cb, r, D_FOURIER), lambda bb, j: (bb, 0, j, 0, 0)),
        out_shape=jax.ShapeDtypeStruct((b, 2, DFT_COLS, r, D_FOURIER), BF16),
        compiler_params=_cparams(("parallel", "parallel")),
        name="dft_rows",
    )(xf.reshape(b, r, DFT_COLS * D_FOURIER), tabs["f_rows"], tabs["tw_cos"], tabs["tw_sin"])
    out = pl.pallas_call(
        functools.partial(_dft_cols_kernel, scale=1.0 / math.sqrt(t * D_FG)),
        grid=(b, r // kb),
        in_specs=[pl.BlockSpec((1, 2, DFT_COLS, kb * D_FOURIER), lambda bb, j: (bb, 0, 0, j)),
                  _const_spec(tabs["m_cols"].shape),
                  _const_spec(tabs["cs_chan"].shape)],
        out_specs=pl.BlockSpec((1, DFT_COLS, kb * D_FOURIER), lambda bb, j: (bb, 0, j)),
        out_shape=jax.ShapeDtypeStruct((b, DFT_COLS, r * D_FOURIER), BF16),
        compiler_params=_cparams(("parallel", "parallel")),
        name="dft_cols",
    )(rows.reshape(b, 2, DFT_COLS, r * D_FOURIER), tabs["m_cols"], tabs["cs_chan"])
    return out.reshape(b, t, D_FOURIER)


def _cos_sin(num, den):
    ang = (num % den).astype(F32) * (2.0 * math.pi / den)
    return jnp.cos(ang), jnp.sin(ang)


def _factored_dft_tables(t):
    r = t // DFT_COLS
    i_r = jnp.arange(r, dtype=jnp.int32)
    i_c = jnp.arange(DFT_COLS, dtype=jnp.int32)
    i_g = jnp.arange(D_FG, dtype=jnp.int32)
    c_r, s_r = _cos_sin(i_r[:, None] * i_r[None, :], r)
    c_t, s_t = _cos_sin(i_c[:, None] * i_r[None, :], t)
    c_c, s_c = _cos_sin(i_c[:, None] * i_c[None, :], DFT_COLS)
    c_g, s_g = _cos_sin(i_g[:, None] * i_g[None, :], D_FG)
    lanes = lambda a: jnp.broadcast_to(a[:, :, None], (DFT_COLS, r, HEAD_PAD))
    return {
        "f_rows": jnp.concatenate([c_r, -s_r], axis=0).astype(BF16),
        "tw_cos": lanes(c_t), "tw_sin": lanes(s_t),
        "m_cols": jnp.block([[c_c, s_c], [-s_c, c_c]]).astype(BF16),
        "cs_chan": jnp.concatenate([c_g, s_g], axis=0).astype(BF16),
    }


def _matmul_kernel(a_ref, b_ref, o_ref, acc_ref, *, scale):
    k = pl.program_id(2)

    @pl.when(k == 0)
    def _():
        acc_ref[...] = jnp.zeros_like(acc_ref)

    acc_ref[...] += _dot(a_ref[...], b_ref[...])

    @pl.when(k == pl.num_programs(2) - 1)
    def _():
        o_ref[...] = (acc_ref[...] * scale).astype(o_ref.dtype)


def _matmul(a, b, scale, out_dtype):
    m, kk = a.shape
    n = b.shape[1]
    bm, bn, bk = _tile(m, 1024), _tile(n, 1024), _tile(kk, 2048)
    return pl.pallas_call(
        functools.partial(_matmul_kernel, scale=scale),
        grid=(m // bm, n // bn, kk // bk),
        in_specs=[pl.BlockSpec((bm, bk), lambda i, j, k: (i, k)),
                  pl.BlockSpec((bk, bn), lambda i, j, k: (k, j))],
        out_specs=pl.BlockSpec((bm, bn), lambda i, j, k: (i, j)),
        out_shape=jax.ShapeDtypeStruct((m, n), out_dtype),
        scratch_shapes=[pltpu.VMEM((bm, bn), F32)],
        compiler_params=_cparams(("parallel", "parallel", "arbitrary")),
        name="dft_matmul",
    )(a, b)


def _even_out_kernel(h_ref, mod_ref, g_ref, gb_ref, z_ref, zp_ref, zn_ref, yf_ref, cw_ref, w_ref, o_ref):
    i = pl.program_id(1)
    x = h_ref[0]
    mod = mod_ref[0]
    g = g_ref[...]
    z = z_ref[0].astype(F32)
    tm = z.shape[0]
    halo = zp_ref.shape[1]
    prev_row = jnp.where(i > 0, zp_ref[0, halo - 1:halo, :].astype(F32), 0.0)
    next_row = jnp.where(i < pl.num_programs(1) - 1, zn_ref[0, 0:1, :].astype(F32), 0.0)
    row = lax.broadcasted_iota(jnp.int32, z.shape, 0)
    z_before = jnp.where(row == 0, prev_row, pltpu.roll(z, 1, 0))
    z_after = jnp.where(row == tm - 1, next_row, pltpu.roll(z, tm - 1, 0))
    cw = cw_ref[...]
    conv = z_before * cw[0:1] + z * cw[1:2] + z_after * cw[2:3]
    y_conv = (gb_ref[0].astype(F32) * conv).astype(BF16)
    y = _dot(y_conv, w_ref[:D_CONV, :]) + _dot(yf_ref[0], w_ref[D_CONV:, :])
    o_ref[0] = x + mod[5:6] * _rms_rows(y, g[3:4])


def _even_out(h, mod, g, gb, z, yf, conv_w, w_out):
    b, t, d = h.shape
    tm = _tile(t, 1024)
    halo = BF16_SUBLANES
    nh = tm // halo
    last_halo = t // halo - 1
    return pl.pallas_call(
        _even_out_kernel,
        grid=(b, t // tm),
        in_specs=[pl.BlockSpec((1, tm, d), lambda bb, i: (bb, i, 0)),
                  _mod_spec(mod),
                  _const_spec(g.shape),
                  pl.BlockSpec((1, tm, D_CONV), lambda bb, i: (bb, i, 0)),
                  pl.BlockSpec((1, tm, D_CONV), lambda bb, i: (bb, i, 0)),
                  pl.BlockSpec((1, halo, D_CONV), lambda bb, i: (bb, jnp.maximum(i * nh - 1, 0), 0)),
                  pl.BlockSpec((1, halo, D_CONV), lambda bb, i: (bb, jnp.minimum((i + 1) * nh, last_halo), 0)),
                  pl.BlockSpec((1, tm, D_FOURIER), lambda bb, i: (bb, i, 0)),
                  _const_spec(conv_w.shape),
                  _const_spec(w_out.shape)],
        out_specs=pl.BlockSpec((1, tm, d), lambda bb, i: (bb, i, 0)),
        out_shape=jax.ShapeDtypeStruct(h.shape, F32),
        compiler_params=_cparams(("parallel", "parallel")),
        name="even_out",
    )(h, mod, g, gb, z, z, z, yf, conv_w, w_out)


def _dft_tables(t):
    n = jnp.arange(t, dtype=jnp.int32)
    ang = ((n[:, None] * n[None, :]) % t).astype(F32) * (2.0 * math.pi / t)
    w_pos = jnp.concatenate([jnp.cos(ang), -jnp.sin(ang)], axis=1).astype(BF16)
    return w_pos


def _channel_dft_table():
    n = jnp.arange(D_FG, dtype=jnp.int32)
    ang = ((n[:, None] * n[None, :]) % D_FG).astype(F32) * (2.0 * math.pi / D_FG)
    return jnp.concatenate([jnp.cos(ang), jnp.sin(ang)], axis=1).astype(BF16)


def _even_mixer(h, mod, g, w_in, conv_w, w_out, dft_c, dft_pos):
    b, t, _ = h.shape
    if isinstance(dft_pos, dict):
        gb, z, xf = _even_in(h, mod, g, w_in, dft_c, False)
        yf = _factored_dft(xf, dft_pos)
    else:
        gb, z, a = _even_in(h, mod, g, w_in, dft_c, True)
        yf = _matmul(dft_pos, a.reshape(2 * t, b * D_FOURIER), 1.0 / math.sqrt(t * D_FG), BF16)
        yf = yf.reshape(t, b, D_FOURIER).transpose(1, 0, 2)
    return _even_out(h, mod, g, gb, z, yf, conv_w, w_out)


def _odd_in_kernel(h_ref, mod_ref, g_ref, wtok_ref, wt_ref, wukp_ref, wuvt_ref, wuqt_ref, place_ref, gsel_ref,
                   gkv_row_ref, gkv_col_ref, gq_col_ref,
                   cosk_ref, sink_ref, tkr_ref, cosq_ref, sinq_ref, cosm_ref, sinm_ref,
                   khd_ref, khm_ref, vtd_ref, vtm_ref, kn_ref, *q_refs, with_q, split8):
    xm = _modulated(h_ref[0], mod_ref[0], g_ref[...], 1).astype(BF16)
    tm = xm.shape[0]

    ut = _dot(xm, wtok_ref[...])
    k = ut[:, :DA_QCOLS]
    ckv = ut[:, DA_QCOLS:DA_QCOLS + MLA_KV_RANK]
    kr2 = ut[:, DA_QCOLS + MLA_KV_RANK:]
    lane = lax.broadcasted_iota(jnp.int32, k.shape, 1)
    first_half = (lane % (DA_DK // 2)) < (DA_DK // 4)
    k_sw = jnp.where(first_half, pltpu.roll(k, DA_QCOLS - DA_DK // 4, 1), pltpu.roll(k, DA_DK // 4, 1))
    reps = DA_QCOLS // HEAD_PAD
    cosk = jnp.concatenate([cosk_ref[...]] * reps, axis=1)
    sink = jnp.concatenate([sink_ref[...]] * reps, axis=1)
    k_rot = k * cosk + k_sw * sink
    ckvn = _rms_rows(ckv, gkv_row_ref[...]).astype(BF16)
    k_nope = _dot(ckvn, wukp_ref[...])
    pr = kr2 * tkr_ref[...]
    pr_hi = pr.astype(BF16)
    pr_lo = (pr - pr_hi.astype(F32)).astype(BF16)
    k_mla = k_nope + _dot(pr_hi, place_ref[...]) + _dot(pr_lo, place_ref[...])
    lane_h = lax.broadcasted_iota(jnp.int32, (tm, HEAD_PAD), 1)
    low_half = lane_h < DA_DK
    to_low = functools.partial(pltpu.roll, shift=DA_QCOLS - DA_DK, axis=1)
    if split8:
        k_hi, k_lo = _split8(k_rot * DA_SPLIT_SCALE)
        k_eff = k_hi + k_lo
        lo_up, hi_dn = pltpu.roll(k_lo, DA_DK, 1), to_low(k_hi)
        for hh in range(DA_HEADS):
            tl = slice(hh * HEAD_PAD, (hh + 1) * HEAD_PAD)
            khd_ref[0, 2 * hh] = jnp.where(low_half, k_hi[:, tl], lo_up[:, tl]).astype(F8)
            khd_ref[0, 2 * hh + 1] = jnp.where(low_half, hi_dn[:, tl], k_lo[:, tl]).astype(F8)
    else:
        k_eff = k_rot.astype(BF16).astype(F32)
        k_dn = to_low(k_rot)
        for hh in range(DA_HEADS):
            tl = slice(hh * HEAD_PAD, (hh + 1) * HEAD_PAD)
            for m, src in enumerate((k_rot, k_dn)):
                khd_ref[0, 2 * hh + m] = jnp.where(lane_h == DA_DK, 1.0,
                                                   jnp.where(low_half, src[:, tl], 0.0)).astype(BF16)
    sq = [(k_eff * k_eff).astype(BF16)]
    for hh in range(MLA_HEADS):
        kd = k_mla[:, hh * HEAD_PAD:(hh + 1) * HEAD_PAD]
        khm_ref[0, hh] = jnp.where(lane_h == MLA_DQK, 1.0, kd).astype(BF16)
        kf = kd.astype(BF16).astype(F32)
        sq.append((kf * kf).astype(BF16))
    kn_ref[0] = _dot_nt(gsel_ref[...], jnp.concatenate(sq, axis=1))

    r0 = DA_VCOLS
    vckv = _dot_nt(wt_ref[:r0 + MLA_KV_RANK, :], xm)
    for hh in range(DA_HEADS):
        vtd_ref[0, hh] = vckv[hh * DA_DV:(hh + 1) * DA_DV].astype(BF16)
    ckv_t = vckv[r0:]
    ms = jnp.mean(ckv_t * ckv_t, axis=0, keepdims=True)
    ckvn_t = (ckv_t * lax.rsqrt(ms + EPS) * gkv_col_ref[...]).astype(BF16)
    vm_t = _dot(wuvt_ref[...], ckvn_t)
    for hh in range(MLA_HEADS):
        vtm_ref[0, hh] = vm_t[hh * MLA_DV:(hh + 1) * MLA_DV].astype(BF16)

    if with_q:
        qtd_ref, qtm_ref = q_refs
        r1 = r0 + MLA_KV_RANK
        q_t = _dot_nt(wt_ref[r1:, :], xm)
        cosq, sinq = cosq_ref[...], sinq_ref[...]
        qd = DA_DK // 4
        for mp in range(2 * DA_HEADS):
            q = q_t[mp * DA_DK:(mp + 1) * DA_DK]
            q_sw = jnp.concatenate([q[qd:2 * qd], q[:qd], q[3 * qd:], q[2 * qd:3 * qd]], axis=0)
            q_rot = q * cosq + q_sw * sinq
            if split8:
                q_hi, q_lo = _split8(q_rot * DA_SPLIT_SCALE)
                qtd_ref[0, mp, :DA_DK, :] = q_hi.astype(F8)
                qtd_ref[0, mp, DA_DK:, :] = q_lo.astype(F8)
            else:
                qtd_ref[0, mp // 2, (mp % 2) * DA_DK:(mp % 2 + 1) * DA_DK, :] = (
                    q_rot * (DA_SCALE * LOG2E)).astype(BF16)
        cq_t = q_t[DA_QCOLS:]
        ms = jnp.mean(cq_t * cq_t, axis=0, keepdims=True)
        cqn_t = (cq_t * lax.rsqrt(ms + EPS) * gq_col_ref[...]).astype(BF16)
        qm_t = _dot(wuqt_ref[...], cqn_t) * (MLA_SCALE * LOG2E)
        cosm, sinm = cosm_ref[...], sinm_ref[...]
        rd = MLA_ROPE // 4
        for hh in range(MLA_HEADS):
            base = hh * HEAD_PAD
            qtm_ref[0, hh, :MLA_NOPE, :] = qm_t[base:base + MLA_NOPE].astype(BF16)
            r = qm_t[base + MLA_NOPE:base + MLA_DQK]
            r_sw = jnp.concatenate([r[rd:2 * rd], r[:rd], r[3 * rd:], r[2 * rd:3 * rd]], axis=0)
            qtm_ref[0, hh, MLA_NOPE:MLA_DQK, :] = (r * cosm + r_sw * sinm).astype(BF16)
            qtm_ref[0, hh, MLA_DQK:, :] = jnp.zeros((HEAD_PAD - MLA_DQK, tm), BF16)


def _odd_in(h, mod, g, wts, tabs, kv_prev, tok_off, with_q, split8):
    b, t, d = h.shape
    da_dtype = F8 if split8 else BF16
    tk_total = tabs["tk_total"]
    tm = _tile(t, 512)
    assert tok_off % tm == 0
    off = tok_off // tm
    row_tab = lambda w: pl.BlockSpec((tm, w), lambda bb, i: (i, 0))
    col_tab = lambda r: pl.BlockSpec((r, tm), lambda bb, i: (0, i))
    consts = [wts["wtok"], wts["wt"], wts["wukp"], wts["wuvt"], wts["wuqt"], wts["place"], wts["gsel"],
              wts["gkv_row"], wts["gkv_col"], wts["gq_col"]]
    in_specs = ([pl.BlockSpec((1, tm, d), lambda bb, i: (bb, i, 0)), _mod_spec(mod), _const_spec(g.shape)]
                + [_const_spec(c.shape) for c in consts]
                + [row_tab(HEAD_PAD), row_tab(HEAD_PAD), row_tab(2 * MLA_ROPE),
                   col_tab(DA_DK), col_tab(DA_DK), col_tab(MLA_ROPE), col_tab(MLA_ROPE)])
    args = [h, mod, g] + consts + [tabs["cosk"], tabs["sink"], tabs["tkr"],
                                   tabs["cosq"], tabs["sinq"], tabs["cosm"], tabs["sinm"]]
    out_specs = [pl.BlockSpec((1, 2 * DA_HEADS, tm, HEAD_PAD), lambda bb, i: (bb, 0, i + off, 0)),
                 pl.BlockSpec((1, MLA_HEADS, tm, HEAD_PAD), lambda bb, i: (bb, 0, i + off, 0)),
                 pl.BlockSpec((1, DA_HEADS, DA_DV, tm), lambda bb, i: (bb, 0, 0, i + off)),
                 pl.BlockSpec((1, MLA_HEADS, MLA_DV, tm), lambda bb, i: (bb, 0, 0, i + off)),
                 pl.BlockSpec((1, N_MAPS, tm), lambda bb, i: (bb, 0, i + off))]
    out_shape = [jax.ShapeDtypeStruct((b, 2 * DA_HEADS, tk_total, HEAD_PAD), da_dtype),
                 jax.ShapeDtypeStruct((b, MLA_HEADS, tk_total, HEAD_PAD), BF16),
                 jax.ShapeDtypeStruct((b, DA_HEADS, DA_DV, tk_total), BF16),
                 jax.ShapeDtypeStruct((b, MLA_HEADS, MLA_DV, tk_total), BF16),
                 jax.ShapeDtypeStruct((b, N_MAPS, tk_total), F32)]
    if with_q:
        n_qd = 2 * DA_HEADS if split8 else DA_HEADS
        out_specs += [pl.BlockSpec((1, n_qd, HEAD_PAD, tm), lambda bb, i: (bb, 0, 0, i)),
                      pl.BlockSpec((1, MLA_HEADS, HEAD_PAD, tm), lambda bb, i: (bb, 0, 0, i))]
        out_shape += [jax.ShapeDtypeStruct((b, n_qd, HEAD_PAD, t), da_dtype),
                      jax.ShapeDtypeStruct((b, MLA_HEADS, HEAD_PAD, t), BF16)]
    aliases = {}
    kernel_fn = functools.partial(_odd_in_kernel, with_q=with_q, split8=split8)
    if kv_prev is not None:
        n_in = len(args)
        in_specs += [pl.BlockSpec(memory_space=pl.ANY)] * N_KV_ARRAYS
        args += list(kv_prev)
        aliases = {n_in + a: a for a in range(N_KV_ARRAYS)}
        kernel_fn = functools.partial(_odd_in_alias_kernel, n_in=n_in, with_q=with_q, split8=split8)
    return pl.pallas_call(
        kernel_fn,
        grid=(b, t // tm),
        in_specs=in_specs,
        out_specs=out_specs,
        out_shape=out_shape,
        input_output_aliases=aliases,
        compiler_params=_cparams(("parallel", "parallel")),
        name="odd_in",
    )(*args)


def _odd_in_alias_kernel(*refs, n_in, with_q, split8):
    _odd_in_kernel(*refs[:n_in], *refs[n_in + N_KV_ARRAYS:], with_q=with_q, split8=split8)


def _attn_kernel(lam_ref, gsub_ref, kn_ref, qtd_ref, qtm_ref, khd_ref, khm_ref, vtd_ref, vtm_ref,
                 otd_ref, otm_ref, lmin_ref, qad_ref, qam_ref, shift_ref, m_ref, l_ref, accd_ref, accm_ref,
                 *, lam_init, safe):
    j = pl.program_id(2)
    tq = qtd_ref.shape[-1]

    @pl.when(j == 0)
    def _():
        m_ref[...] = jnp.full_like(m_ref, NEG_BIG)
        l_ref[...] = jnp.zeros_like(l_ref)
        accd_ref[...] = jnp.zeros_like(accd_ref)
        accm_ref[...] = jnp.zeros_like(accm_ref)
        kmax = jnp.sqrt(jnp.max(kn_ref[0], axis=1, keepdims=True))
        shift_row = lax.broadcasted_iota(jnp.int32, (BF16_SUBLANES, tq), 0) == 0
        for idx in range(2 * DA_HEADS):
            if safe:
                q = qtd_ref[0, idx // 2, (idx % 2) * DA_DK:(idx % 2 + 1) * DA_DK, :]
                qad_ref[idx] = jnp.concatenate([q, jnp.zeros((HEAD_PAD - DA_DK, tq), BF16)], axis=0)
            else:
                q_hi, q_lo = qtd_ref[0, idx, :DA_DK, :], qtd_ref[0, idx, DA_DK:, :]
                qad_ref[idx] = jnp.concatenate([q_hi, q_hi, q_lo, q_lo], axis=0)
                qf = q_hi.astype(F32) + q_lo.astype(F32)
                qn = jnp.sqrt(jnp.sum(qf * qf, axis=0, keepdims=True))
                ok = (qn < F8_SAFE_MAX) & (kmax[idx:idx + 1] < F8_SAFE_MAX)
                shift_ref[idx] = jnp.where(ok, (BOUND_SLACK * kmax[idx:idx + 1]) * qn, jnp.inf)
        for hh in range(MLA_HEADS):
            idx = 2 * DA_HEADS + hh
            q = qtm_ref[0, hh, :MLA_DQK, :]
            if safe:
                shift_blk = jnp.zeros((BF16_SUBLANES, tq), BF16)
            else:
                qf = q.astype(F32)
                qn = jnp.sqrt(jnp.sum(qf * qf, axis=0, keepdims=True))
                shift = -(BOUND_SLACK * kmax[idx:idx + 1]) * qn
                shift_blk = jnp.where(shift_row, shift, 0.0).astype(BF16)
            pad = jnp.zeros((HEAD_PAD - MLA_DQK - BF16_SUBLANES, tq), BF16)
            qam_ref[hh] = jnp.concatenate([q, shift_blk, pad], axis=0)

    def softmax_pv(s, ln, idx, v_t, acc_ref, acc_idx):
        if safe:
            m_old = m_ref[idx, :, ln]
            m_new = jnp.maximum(m_old, jnp.max(s, axis=0, keepdims=True))
            alpha = jnp.exp2(m_old - m_new)
            m_ref[idx, :, ln] = m_new
            p = jnp.exp2(s - m_new)
            l_ref[idx, :, ln] = alpha * l_ref[idx, :, ln] + jnp.sum(p.reshape(-1, F32_SUBLANES, s.shape[1]), axis=0)
            acc_ref[acc_idx, :, ln] = acc_ref[acc_idx, :, ln] * alpha + _dot(v_t, p.astype(BF16))
        else:
            p = jnp.exp2(s)
            l_ref[idx, :, ln] += jnp.sum(p.reshape(-1, F32_SUBLANES, s.shape[1]), axis=0)
            acc_ref[acc_idx, :, ln] += _dot(v_t, p.astype(BF16))

    chunk = ATTN_Q_CHUNK if (not safe and tq % ATTN_Q_CHUNK == 0) else tq
    q_chunks = [slice(c, c + chunk) for c in range(0, tq, chunk)]
    da_per_iter = 1 if safe else DA_HEADS_PER_ITER
    mla_per_iter = 1 if safe else MLA_HEADS_PER_ITER

    def da_score(hh, m, ln):
        keys = khd_ref[0, 2 * hh + m]
        if safe:
            return _dot(keys, qad_ref[2 * hh + m, :, ln])
        return _dot(jnp.concatenate([keys, keys], axis=1), qad_ref[2 * hh + m, :, ln])

    def mla_score(hh, ln):
        return _dot(khm_ref[0, hh], qam_ref[hh, :, ln])

    def run_maps(scores, consume):
        s_next = scores[0]()
        for n in range(len(scores)):
            s_cur = s_next
            if n + 1 < len(scores):
                s_next = scores[n + 1]()
            consume[n](s_cur)

    def da_units(hh, scores, consume):
        for m in range(2):
            for ln in q_chunks:
                scores.append(functools.partial(da_score, hh, m, ln))
                consume.append(lambda s, m=m, ln=ln: softmax_pv(
                    s if safe else s - shift_ref[2 * hh + m, :, ln], ln, 2 * hh + m, vtd_ref[0, hh],
                    accd_ref, 2 * hh + m))

    def mla_units(hh, scores, consume):
        mchunk = MLA_Q_CHUNK if (not safe and tq % MLA_Q_CHUNK == 0) else tq
        for c in range(0, tq, mchunk):
            ln = slice(c, c + mchunk)
            scores.append(functools.partial(mla_score, hh, ln))
            consume.append(lambda s, ln=ln: softmax_pv(s, ln, 2 * DA_HEADS + hh, vtm_ref[0, hh], accm_ref, hh))

    def da_body(it, carry):
        scores, consume = [], []
        for u in range(da_per_iter):
            da_units(it * da_per_iter + u, scores, consume)
        run_maps(scores, consume)
        return carry

    def mla_body(it, carry):
        scores, consume = [], []
        for u in range(mla_per_iter):
            mla_units(it * mla_per_iter + u, scores, consume)
        run_maps(scores, consume)
        return carry

    lax.fori_loop(0, DA_HEADS // da_per_iter, da_body, 0)
    lax.fori_loop(0, MLA_HEADS // mla_per_iter, mla_body, 0)

    @pl.when(j == pl.num_programs(2) - 1)
    def _():
        lp = lam_ref[...]
        lam = (jnp.exp(jnp.sum(lp[0:1] * lp[1:2], axis=1, keepdims=True))
               - jnp.exp(jnp.sum(lp[2:3] * lp[3:4], axis=1, keepdims=True)) + lam_init)
        gsub = gsub_ref[...]
        lsum = [jnp.sum(l_ref[idx], axis=0, keepdims=True) for idx in range(N_MAPS)]
        linv = [1.0 / ls for ls in lsum]
        for hh in range(DA_HEADS):
            o = accd_ref[2 * hh] * linv[2 * hh] - accd_ref[2 * hh + 1] * (lam * linv[2 * hh + 1])
            ms = jnp.mean(o * o, axis=0, keepdims=True)
            otd_ref[0, hh] = (o * (lax.rsqrt(ms + EPS) * (1.0 - lam_init)) * gsub).astype(otd_ref.dtype)
        for hh in range(MLA_HEADS):
            otm_ref[0, hh] = (accm_ref[hh] * linv[2 * DA_HEADS + hh]).astype(otm_ref.dtype)
        lmin = lsum[0]
        for idx in range(1, N_MAPS):
            lmin = jnp.minimum(lmin, lsum[idx])
        lmin_ref[0, 0] = jnp.broadcast_to(jnp.min(lmin, axis=1, keepdims=True), lmin_ref.shape[2:])


def _attention_call(lam_p, gsub_col, qtd, qtm, kv, lam_init, tq, tk, kv_off, n_kv, safe):
    khd, khm, vtd, vtm, kn = kv
    b, n_qd, _, t = qtd.shape
    nq = t // tq
    assert kv_off % n_kv == 0
    qad_scratch = (pltpu.VMEM((2 * DA_HEADS, HEAD_PAD, tq), BF16) if safe
                   else pltpu.VMEM((2 * DA_HEADS, 4 * DA_DK, tq), F8))
    return pl.pallas_call(
        functools.partial(_attn_kernel, lam_init=lam_init, safe=safe),
        grid=(b, nq, n_kv),
        in_specs=[_const_spec(lam_p.shape),
                  _const_spec(gsub_col.shape),
                  pl.BlockSpec((1, N_MAPS, n_kv * tk), lambda bb, i, j: (bb, 0, kv_off // n_kv)),
                  pl.BlockSpec((1, n_qd, HEAD_PAD, tq), lambda bb, i, j: (bb, 0, 0, i)),
                  pl.BlockSpec((1, MLA_HEADS, HEAD_PAD, tq), lambda bb, i, j: (bb, 0, 0, i)),
                  pl.BlockSpec((1, 2 * DA_HEADS, tk, HEAD_PAD), lambda bb, i, j: (bb, 0, kv_off + j, 0)),
                  pl.BlockSpec((1, MLA_HEADS, tk, HEAD_PAD), lambda bb, i, j: (bb, 0, kv_off + j, 0)),
                  pl.BlockSpec((1, DA_HEADS, DA_DV, tk), lambda bb, i, j: (bb, 0, 0, kv_off + j)),
                  pl.BlockSpec((1, MLA_HEADS, MLA_DV, tk), lambda bb, i, j: (bb, 0, 0, kv_off + j))],
        out_specs=[pl.BlockSpec((1, DA_HEADS, DA_DV, tq), lambda bb, i, j: (bb, 0, 0, i)),
                   pl.BlockSpec((1, MLA_HEADS, MLA_DV, tq), lambda bb, i, j: (bb, 0, 0, i)),
                   pl.BlockSpec((1, 1, F32_SUBLANES, HEAD_PAD), lambda bb, i, j: (bb, i, 0, 0))],
        out_shape=[jax.ShapeDtypeStruct((b, DA_HEADS, DA_DV, t), BF16),
                   jax.ShapeDtypeStruct((b, MLA_HEADS, MLA_DV, t), BF16),
                   jax.ShapeDtypeStruct((b, nq, F32_SUBLANES, HEAD_PAD), F32)],
        scratch_shapes=[qad_scratch,
                        pltpu.VMEM((MLA_HEADS, HEAD_PAD, tq), BF16),
                        pltpu.VMEM((2 * DA_HEADS, 1, tq), F32),
                        pltpu.VMEM((N_MAPS, 1, tq), F32),
                        pltpu.VMEM((N_MAPS, F32_SUBLANES, tq), F32),
                        pltpu.VMEM((2 * DA_HEADS, DA_DV, tq), F32),
                        pltpu.VMEM((MLA_HEADS, MLA_DV, tq), F32)],
        compiler_params=_cparams(("parallel", "parallel", "arbitrary")),
        name="attention_safe" if safe else "attention",
    )(lam_p, gsub_col, kn, qtd, qtm, khd, khm, vtd, vtm)


def _odd_out_kernel(h_ref, mod_ref, g_ref, otd_ref, otm_ref, wt_ref, o_ref):
    y_t = _dot(wt_ref[:, :DA_VCOLS], otd_ref[0]) + _dot(wt_ref[:, DA_VCOLS:], otm_ref[0])
    y = y_t.T
    o_ref[0] = h_ref[0] + mod_ref[0][5:6] * _rms_rows(y, g_ref[3:4])


def _odd_out(h, mod, g, otd, otm, w_out_t):
    b, t, d = h.shape
    tm = _tile(t, 1024)
    otd = otd.reshape(b, DA_VCOLS, t)
    otm = otm.reshape(b, MLA_HEADS * MLA_DV, t)
    return pl.pallas_call(
        _odd_out_kernel,
        grid=(b, t // tm),
        in_specs=[pl.BlockSpec((1, tm, d), lambda bb, i: (bb, i, 0)),
                  _mod_spec(mod),
                  _const_spec(g.shape),
                  pl.BlockSpec((1, DA_VCOLS, tm), lambda bb, i: (bb, 0, i)),
                  pl.BlockSpec((1, MLA_HEADS * MLA_DV, tm), lambda bb, i: (bb, 0, i)),
                  _const_spec(w_out_t.shape)],
        out_specs=pl.BlockSpec((1, tm, d), lambda bb, i: (bb, i, 0)),
        out_shape=jax.ShapeDtypeStruct(h.shape, F32),
        compiler_params=_cparams(("parallel", "parallel")),
        name="odd_out",
    )(h, mod, g, otd, otm, w_out_t)


def _rope_angles(n_tokens, rot_dim):
    rows = n_tokens // GRID_W
    row = jnp.broadcast_to(jnp.arange(rows)[:, None], (rows, GRID_W)).reshape(-1).astype(F32)
    col = jnp.broadcast_to(jnp.arange(GRID_W)[None, :], (rows, GRID_W)).reshape(-1).astype(F32)
    n_freq = rot_dim // 4
    freqs = ROPE_THETA ** (-jnp.arange(n_freq, dtype=F32) / n_freq)
    return row[:, None] * freqs, col[:, None] * freqs


def _rope_cos_sin(n_tokens, rot_dim, identity):
    if identity:
        return jnp.ones((n_tokens, rot_dim), F32), jnp.zeros((n_tokens, rot_dim), F32)
    ar, ac = _rope_angles(n_tokens, rot_dim)
    cos = jnp.concatenate([jnp.cos(ar), jnp.cos(ar), jnp.cos(ac), jnp.cos(ac)], axis=1)
    sin = jnp.concatenate([-jnp.sin(ar), jnp.sin(ar), -jnp.sin(ac), jnp.sin(ac)], axis=1)
    return cos, sin


def _rope_tables(n_tokens, identity, tk_total):
    cd, sd = _rope_cos_sin(n_tokens, DA_DK, identity)
    cm, sm = _rope_cos_sin(n_tokens, MLA_ROPE, identity)
    return {
        "cosk": jnp.concatenate([cd, cd], axis=1), "sink": jnp.concatenate([sd, sd], axis=1),
        "tkr": jnp.concatenate([cm, sm], axis=1),
        "cosq": cd.T, "sinq": sd.T, "cosm": cm.T, "sinm": sm.T,
        "tk_total": tk_total,
    }


def _swap_perm(rot_dim):
    q = rot_dim // 4
    return jnp.concatenate([jnp.arange(q, 2 * q), jnp.arange(0, q), jnp.arange(3 * q, 4 * q), jnp.arange(2 * q, 3 * q)])


def _map_lane_selector():
    da = jnp.kron(jnp.eye(2 * DA_HEADS, dtype=F32), jnp.ones((1, DA_DK), F32))
    mla = jnp.kron(jnp.eye(MLA_HEADS, dtype=F32), jnp.ones((1, HEAD_PAD), F32))
    top = jnp.concatenate([da, jnp.zeros((2 * DA_HEADS, MLA_HEADS * HEAD_PAD), F32)], axis=1)
    bottom = jnp.concatenate([jnp.zeros((MLA_HEADS, DA_QCOLS), F32), mla], axis=1)
    return jnp.concatenate([top, bottom], axis=0).astype(BF16)


def _odd_weights(w_in, g_q, w_uq, g_kv, w_uk, w_uv):
    w_q = w_in[:, :DA_QCOLS]
    w_cq = w_in[:, DA_QCOLS:Q_COLS]
    w_k = w_in[:, Q_COLS:Q_COLS + DA_QCOLS]
    w_v = w_in[:, Q_COLS + DA_QCOLS:Q_COLS + DA_QCOLS + DA_VCOLS]
    w_ckv = w_in[:, Q_COLS + DA_QCOLS + DA_VCOLS:Q_COLS + DA_QCOLS + DA_VCOLS + MLA_KV_RANK]
    w_kr = w_in[:, Q_COLS + DA_QCOLS + DA_VCOLS + MLA_KV_RANK:]
    wtok = jnp.concatenate([w_k, w_ckv, w_kr, w_kr[:, _swap_perm(MLA_ROPE)]], axis=1).astype(BF16)
    wt = jnp.concatenate([w_v, w_ckv, w_q, w_cq], axis=1).T.astype(BF16)
    pad_k = jnp.zeros((MLA_KV_RANK, MLA_HEADS, HEAD_PAD), F32)
    wukp = pad_k.at[:, :, :MLA_NOPE].set(w_uk.reshape(MLA_KV_RANK, MLA_HEADS, MLA_NOPE))
    wukp = wukp.reshape(MLA_KV_RANK, MLA_HEADS * HEAD_PAD).astype(BF16)
    pad_q = jnp.zeros((MLA_Q_RANK, MLA_HEADS, HEAD_PAD), F32)
    wuqp = pad_q.at[:, :, :MLA_DQK].set(w_uq.reshape(MLA_Q_RANK, MLA_HEADS, MLA_DQK))
    wuqt = wuqp.reshape(MLA_Q_RANK, MLA_HEADS * HEAD_PAD).T.astype(BF16)
    eye = jnp.eye(MLA_ROPE, dtype=F32)
    place = jnp.zeros((2, MLA_ROPE, MLA_HEADS, HEAD_PAD), F32)
    place = place.at[:, :, :, MLA_NOPE:MLA_DQK].set(jnp.broadcast_to(eye[None, :, None, :], (2, MLA_ROPE, MLA_HEADS, MLA_ROPE)))
    place = place.reshape(2 * MLA_ROPE, MLA_HEADS * HEAD_PAD).astype(BF16)
    return {
        "wtok": wtok, "wt": wt, "wukp": wukp, "wuvt": w_uv.T.astype(BF16), "wuqt": wuqt, "place": place,
        "gsel": _map_lane_selector(),
        "gkv_row": g_kv.reshape(1, -1).astype(F32), "gkv_col": g_kv.reshape(-1, 1).astype(F32),
        "gq_col": g_q.reshape(-1, 1).astype(F32),
    }


def kernel(x, c, ctx, c_ctx, w_mod, b_mod, norm_g, w_ffn_in, w_ffn_out, w_in_even, conv_w, w_out_even,
           w_in_odd, g_q_mla, w_uq, g_kv_mla, w_uk, w_uv, lam_q1, lam_k1, lam_q2, lam_k2, g_subln, w_out_odd):
    b, t, d = x.shape
    tc = ctx.shape[1]
    depth = w_mod.shape[0]
    tk_total = t + tc

    rows = -(-(b + 1) // F32_SUBLANES) * F32_SUBLANES
    cond = jnp.zeros((rows, d), F32).at[:b].set(c).at[b].set(c_ctx)
    mod = _adaln(cond, w_mod, b_mod).reshape(depth, rows, N_MOD, d)

    dft_c = _channel_dft_table()
    factored = lambda n: n % (DFT_COLS * BF16_SUBLANES) == 0
    w_pos_x = _factored_dft_tables(t) if factored(t) else _dft_tables(t)
    w_pos_c = _factored_dft_tables(tc) if factored(tc) else _dft_tables(tc)
    tabs_x = _rope_tables(t, False, tk_total)
    tabs_c = _rope_tables(tc, True, tk_total)
    tq = _tile(t, 1024)
    tk = 768 if tk_total % 768 == 0 else _tile(tk_total, 512)
    assert t % tc == 0 and tk_total % tk == 0

    h, hc = x, ctx
    for l in range(depth):
        last = l == depth - 1
        odd = l % 2 == 1
        ctx_live = (not last) or odd
        g = norm_g[l]
        m_x, m_c = mod[l, :b], mod[l, b:b + 1]
        wi0, wo0 = w_ffn_in[l, 0].astype(BF16), w_ffn_out[l, 0].astype(BF16)
        h = _ffn(h, m_x, g, wi0, wo0, 0)
        if ctx_live:
            hc = _ffn(hc, m_c, g, wi0, wo0, 0)

        if not odd:
            e = l // 2
            w_in, w_out = w_in_even[e].astype(BF16), w_out_even[e].astype(BF16)
            h = _even_mixer(h, m_x, g, w_in, conv_w[e], w_out, dft_c, w_pos_x)
            if ctx_live:
                hc = _even_mixer(hc, m_c, g, w_in, conv_w[e], w_out, dft_c, w_pos_c)
        else:
            o = l // 2
            lam_init = 0.8 - 0.6 * math.exp(-0.3 * l)
            wts = _odd_weights(w_in_odd[o], g_q_mla[o], w_uq[o], g_kv_mla[o], w_uk[o], w_uv[o])
            lam_p = jnp.stack([lam_q1[o], lam_k1[o], lam_q2[o], lam_k2[o]]).astype(F32)
            gsub_col = g_subln[o].reshape(-1, 1).astype(F32)
            w_out_t = w_out_odd[o].T.astype(BF16)
            def latent_attention(safe, h=h, hc=hc, m_x=m_x, m_c=m_c, g=g, wts=wts, lam_p=lam_p,
                                 gsub_col=gsub_col, lam_init=lam_init):
                outs_x = _odd_in(h, m_x, g, wts, tabs_x, None, 0, True, not safe)
                kv = _odd_in(hc, m_c, g, wts, tabs_c, outs_x[:N_KV_ARRAYS], t, False, not safe)
                return _attention_call(lam_p, gsub_col, *outs_x[N_KV_ARRAYS:], kv, lam_init, tq, tk, 0,
                                       tk_total // tk, safe)

            otd, otm, lmin = latent_attention(False)
            otd, otm = lax.cond(jnp.min(lmin) >= MIN_DENOMINATOR, lambda: (otd, otm),
                                lambda: tuple(latent_attention(True)[:2]))
            if not last:
                outs_c = _odd_in(hc, m_c, g, wts, dict(tabs_c, tk_total=tc), None, 0, True, False)
                ocd, ocm = _attention_call(lam_p, gsub_col, *outs_c[N_KV_ARRAYS:], outs_c[:N_KV_ARRAYS],
                                           lam_init, tc, tc, 0, 1, True)[:2]
                hc = _odd_out(hc, m_c, g, ocd, ocm, w_out_t)
            h = _odd_out(h, m_x, g, otd, otm, w_out_t)

        wi1, wo1 = w_ffn_in[l, 1].astype(BF16), w_ffn_out[l, 1].astype(BF16)
        h = _ffn(h, m_x, g, wi1, wo1, 2)
        if not last:
            hc = _ffn(hc, m_c, g, wi1, wo1, 2)
    return h
```

```python
import functools
import math

import jax
import jax.numpy as jnp
from jax import lax
from jax.experimental import pallas as pl
from jax.experimental.pallas import tpu as pltpu

F32 = jnp.float32
BF16 = jnp.bfloat16

D_MODEL = 1024
GRID_W = 64
N_MOD = 9
FFN_RES = 0.5
EPS = 1e-6
ROPE_THETA = 10000.0

D_CONV = 512
D_FOURIER = 512
FOURIER_GROUPS = 4
D_FG = D_FOURIER // FOURIER_GROUPS

DA_HEADS = 8
DA_DK = 64
DA_DV = 128
DA_SCALE = DA_DK ** -0.5
MLA_HEADS = 8
MLA_NOPE = 64
MLA_ROPE = 32
MLA_DQK = MLA_NOPE + MLA_ROPE
MLA_DV = 64
MLA_Q_RANK = 384
MLA_KV_RANK = 256
MLA_SCALE = MLA_DQK ** -0.5
DA_QCOLS = DA_HEADS * 2 * DA_DK
DA_VCOLS = DA_HEADS * DA_DV
Q_COLS = DA_QCOLS + MLA_Q_RANK
HEAD_PAD = 128
F32_SUBLANES = 8
BF16_SUBLANES = 16
N_MAPS = 2 * DA_HEADS + MLA_HEADS
LOG2E = 1.4426950408889634
NEG_BIG = -1e30
N_KV_ARRAYS = 5
F8 = jnp.float8_e4m3fn
F8_SAFE_MAX = 400.0
DA_SPLIT_SCALE = math.sqrt(DA_SCALE * LOG2E)
FFN_ROW_SPLITS = 4
FFN_GROUP_ROWS = 256
DA_HEADS_PER_ITER = 4
MLA_HEADS_PER_ITER = 8
ATTN_Q_CHUNK = 256
MLA_Q_CHUNK = 512
BOUND_SLACK = 1.02
MIN_DENOMINATOR = 2.0 ** -40

VMEM_LIMIT_V7X = 56 * 1024 * 1024


def _cparams(sem):
    return pltpu.CompilerParams(dimension_semantics=sem, vmem_limit_bytes=VMEM_LIMIT_V7X)


def _tile(n, pref):
    if n <= pref:
        return n
    t = pref - pref % 128
    while t >= 128:
        if n % t == 0:
            return t
        t -= 128
    return n


def _const_spec(shape):
    nd = len(shape)
    return pl.BlockSpec(shape, lambda *_: (0,) * nd, pipeline_mode=pl.Buffered(1))


def _mod_spec(mod):
    if mod.shape[0] == 1:
        return pl.BlockSpec((1, N_MOD, D_MODEL), lambda b, *_: (0, 0, 0))
    return pl.BlockSpec((1, N_MOD, D_MODEL), lambda b, *_: (b, 0, 0))


def _rms_rows(x, g):
    ms = jnp.mean(x * x, axis=-1, keepdims=True)
    return x * lax.rsqrt(ms + EPS) * g


def _modulated(x, mod, g, slot):
    shift = mod[3 * slot:3 * slot + 1]
    scale = mod[3 * slot + 1:3 * slot + 2]
    return _rms_rows(x, g[2 * slot:2 * slot + 1]) * (1.0 + scale) + shift


def _split8(x):
    hi = x.astype(F8).astype(F32)
    lo = (x - hi).astype(F8).astype(F32)
    return hi, lo


def _dot(a, b):
    return jnp.dot(a, b, preferred_element_type=F32)


def _dot_nt(a, b):
    return lax.dot_general(a, b, (((1,), (1,)), ((), ())), preferred_element_type=F32)


def _adaln_kernel(c_ref, w_ref, b_ref, o_ref):
    c = c_ref[...]
    a = c * jax.nn.sigmoid(c)
    o_ref[0] = jnp.dot(a, w_ref[0], preferred_element_type=F32,
                       precision=lax.Precision.HIGHEST) + b_ref[0]


def _adaln(cond, w_mod, b_mod):
    depth, d, n = w_mod.shape
    rows = cond.shape[0]
    tn = _tile(n, 1152)
    return pl.pallas_call(
        _adaln_kernel,
        grid=(depth, n // tn),
        in_specs=[pl.BlockSpec((rows, d), lambda l, j: (0, 0)),
                  pl.BlockSpec((1, d, tn), lambda l, j: (l, 0, j)),
                  pl.BlockSpec((1, 1, tn), lambda l, j: (l, 0, j))],
        out_specs=pl.BlockSpec((1, rows, tn), lambda l, j: (l, 0, j)),
        out_shape=jax.ShapeDtypeStruct((depth, rows, n), F32),
        compiler_params=_cparams(("parallel", "parallel")),
        name="adaln",
    )(cond, w_mod, b_mod.reshape(depth, 1, n))


def _ffn_kernel(h_ref, mod_ref, g_ref, win_ref, wout_ref, o_ref, *, slot, d_ff):
    mod = mod_ref[0]
    g = g_ref[...]
    res_gate = mod[3 * slot + 2:3 * slot + 3]
    tm = h_ref.shape[1]
    n_groups = max(1, tm // FFN_GROUP_ROWS)
    rows = tm // n_groups
    for part in range(n_groups):
        x = h_ref[0, part * rows:(part + 1) * rows, :]
        xm = _modulated(x, mod, g, slot).astype(BF16)
        gate = _dot(xm, win_ref[:, :d_ff])
        up = _dot(xm, win_ref[:, d_ff:])
        act = (gate * jax.nn.sigmoid(gate) * up).astype(BF16)
        y = _dot(act, wout_ref[...])
        o_ref[0, part * rows:(part + 1) * rows, :] = (
            x + FFN_RES * res_gate * _rms_rows(y, g[2 * slot + 1:2 * slot + 2]))


def _ffn(h, mod, g, w_in, w_out, slot):
    b, t, d = h.shape
    d_ff = w_out.shape[0]
    tm = _tile(t, FFN_ROW_SPLITS * FFN_GROUP_ROWS)
    return pl.pallas_call(
        functools.partial(_ffn_kernel, slot=slot, d_ff=d_ff),
        grid=(b, t // tm),
        in_specs=[pl.BlockSpec((1, tm, d), lambda bb, i: (bb, i, 0)),
                  _mod_spec(mod),
                  _const_spec(g.shape),
                  _const_spec(w_in.shape),
                  _const_spec(w_out.shape)],
        out_specs=pl.BlockSpec((1, tm, d), lambda bb, i: (bb, i, 0)),
        out_shape=jax.ShapeDtypeStruct(h.shape, F32),
        compiler_params=_cparams(("parallel", "parallel")),
        name="ffn",
    )(h, mod, g, w_in, w_out)


def _even_in_kernel(h_ref, mod_ref, g_ref, w_ref, dft_ref, gb_ref, z_ref, a_ref, *, channel_dft):
    xm = _modulated(h_ref[0], mod_ref[0], g_ref[...], 1).astype(BF16)
    u = _dot(xm, w_ref[...])
    gb_ref[0] = u[:, :D_CONV].astype(BF16)
    z_ref[0] = (u[:, D_CONV:2 * D_CONV] * u[:, 2 * D_CONV:3 * D_CONV]).astype(BF16)
    xf = u[:, 3 * D_CONV:].astype(BF16)
    if not channel_dft:
        a_ref[0] = xf
        return
    for gi in range(FOURIER_GROUPS):
        pq = _dot(xf[:, gi * D_FG:(gi + 1) * D_FG], dft_ref[...])
        a_ref[0, :, gi * D_FG:(gi + 1) * D_FG] = pq[:, :D_FG].astype(BF16)
        a_ref[1, :, gi * D_FG:(gi + 1) * D_FG] = pq[:, D_FG:].astype(BF16)


def _even_in(h, mod, g, w_in, dft_c, channel_dft):
    b, t, d = h.shape
    tm = _tile(t, 1024)
    if channel_dft:
        a_spec = pl.BlockSpec((2, tm, D_FOURIER), lambda bb, i: (0, i, bb))
        a_shape = jax.ShapeDtypeStruct((2, t, b * D_FOURIER), BF16)
    else:
        a_spec = pl.BlockSpec((1, tm, D_FOURIER), lambda bb, i: (bb, i, 0))
        a_shape = jax.ShapeDtypeStruct((b, t, D_FOURIER), BF16)
    return pl.pallas_call(
        functools.partial(_even_in_kernel, channel_dft=channel_dft),
        grid=(b, t // tm),
        in_specs=[pl.BlockSpec((1, tm, d), lambda bb, i: (bb, i, 0)),
                  _mod_spec(mod),
                  _const_spec(g.shape),
                  _const_spec(w_in.shape),
                  _const_spec(dft_c.shape)],
        out_specs=[pl.BlockSpec((1, tm, D_CONV), lambda bb, i: (bb, i, 0)),
                   pl.BlockSpec((1, tm, D_CONV), lambda bb, i: (bb, i, 0)),
                   a_spec],
        out_shape=[jax.ShapeDtypeStruct((b, t, D_CONV), BF16),
                   jax.ShapeDtypeStruct((b, t, D_CONV), BF16),
                   a_shape],
        compiler_params=_cparams(("parallel", "parallel")),
        name="even_in",
    )(h, mod, g, w_in, dft_c)


DFT_COLS = 64


def _dft_rows_kernel(x_ref, f_ref, tc_ref, ts_ref, o_ref):
    res = _dot(f_ref[...], x_ref[0])
    r = f_ref.shape[0] // 2
    reps = D_FOURIER // tc_ref.shape[2]
    for c in range(tc_ref.shape[0]):
        a_re = res[:r, c * D_FOURIER:(c + 1) * D_FOURIER]
        a_im = res[r:, c * D_FOURIER:(c + 1) * D_FOURIER]
        tc = jnp.concatenate([tc_ref[c]] * reps, axis=1)
        ts = jnp.concatenate([ts_ref[c]] * reps, axis=1)
        o_ref[0, 0, c] = (a_re * tc + a_im * ts).astype(BF16)
        o_ref[0, 1, c] = (a_im * tc - a_re * ts).astype(BF16)


def _dft_cols_kernel(b_ref, m_ref, cs_ref, o_ref, *, scale):
    z = _dot(m_ref[...], jnp.concatenate([b_ref[0, 0], b_ref[0, 1]], axis=0))
    kb = z.shape[1] // D_FOURIER
    for gi in range(FOURIER_GROUPS):
        lanes = [slice(k * D_FOURIER + gi * D_FG, k * D_FOURIER + (gi + 1) * D_FG) for k in range(kb)]
        z_re = jnp.concatenate([z[:DFT_COLS, ln] for ln in lanes], axis=0)
        z_im = jnp.concatenate([z[DFT_COLS:, ln] for ln in lanes], axis=0)
        y = _dot(jnp.concatenate([z_re, z_im], axis=1).astype(BF16), cs_ref[...]) * scale
        for k in range(kb):
            o_ref[0, :, lanes[k]] = y[k * DFT_COLS:(k + 1) * DFT_COLS].astype(o_ref.dtype)


def _factored_dft(xf, tabs):
    b, t, _ = xf.shape
    r = t // DFT_COLS
    cb = kb = F32_SUBLANES
    rows = pl.pallas_call(
        _dft_rows_kernel,
        grid=(b, DFT_COLS // cb),
        in_specs=[pl.BlockSpec((1, r, cb * D_FOURIER), lambda bb, j: (bb, 0, j)),
                  _const_spec(tabs["f_rows"].shape),
                  pl.BlockSpec((cb, r, HEAD_PAD), lambda bb, j: (j, 0, 0)),
                  pl.BlockSpec((cb, r, HEAD_PAD), lambda bb, j: (j, 0, 0))],
        out_specs=pl.BlockSpec((1, 2, cb, r, D_FOURIER), lambda bb, j: (bb, 0, j, 0, 0)),
        out_shape=jax.ShapeDtypeStruct((b, 2, DFT_COLS, r, D_FOURIER), BF16),
        compiler_params=_cparams(("parallel", "parallel")),
        name="dft_rows",
    )(xf.reshape(b, r, DFT_COLS * D_FOURIER), tabs["f_rows"], tabs["tw_cos"], tabs["tw_sin"])
    out = pl.pallas_call(
        functools.partial(_dft_cols_kernel, scale=1.0 / math.sqrt(t * D_FG)),
        grid=(b, r // kb),
        in_specs=[pl.BlockSpec((1, 2, DFT_COLS, kb * D_FOURIER), lambda bb, j: (bb, 0, 0, j)),
                  _const_spec(tabs["m_cols"].shape),
                  _const_spec(tabs["cs_chan"].shape)],
        out_specs=pl.BlockSpec((1, DFT_COLS, kb * D_FOURIER), lambda bb, j: (bb, 0, j)),
        out_shape=jax.ShapeDtypeStruct((b, DFT_COLS, r * D_FOURIER), BF16),
        compiler_params=_cparams(("parallel", "parallel")),
        name="dft_cols",
    )(rows.reshape(b, 2, DFT_COLS, r * D_FOURIER), tabs["m_cols"], tabs["cs_chan"])
    return out.reshape(b, t, D_FOURIER)


def _cos_sin(num, den):
    ang = (num % den).astype(F32) * (2.0 * math.pi / den)
    return jnp.cos(ang), jnp.sin(ang)


def _factored_dft_tables(t):
    r = t // DFT_COLS
    i_r = jnp.arange(r, dtype=jnp.int32)
    i_c = jnp.arange(DFT_COLS, dtype=jnp.int32)
    i_g = jnp.arange(D_FG, dtype=jnp.int32)
    c_r, s_r = _cos_sin(i_r[:, None] * i_r[None, :], r)
    c_t, s_t = _cos_sin(i_c[:, None] * i_r[None, :], t)
    c_c, s_c = _cos_sin(i_c[:, None] * i_c[None, :], DFT_COLS)
    c_g, s_g = _cos_sin(i_g[:, None] * i_g[None, :], D_FG)
    lanes = lambda a: jnp.broadcast_to(a[:, :, None], (DFT_COLS, r, HEAD_PAD))
    return {
        "f_rows": jnp.concatenate([c_r, -s_r], axis=0).astype(BF16),
        "tw_cos": lanes(c_t), "tw_sin": lanes(s_t),
        "m_cols": jnp.block([[c_c, s_c], [-s_c, c_c]]).astype(BF16),
        "cs_chan": jnp.concatenate([c_g, s_g], axis=0).astype(BF16),
    }


def _matmul_kernel(a_ref, b_ref, o_ref, acc_ref, *, scale):
    k = pl.program_id(2)

    @pl.when(k == 0)
    def _():
        acc_ref[...] = jnp.zeros_like(acc_ref)

    acc_ref[...] += _dot(a_ref[...], b_ref[...])

    @pl.when(k == pl.num_programs(2) - 1)
    def _():
        o_ref[...] = (acc_ref[...] * scale).astype(o_ref.dtype)


def _matmul(a, b, scale, out_dtype):
    m, kk = a.shape
    n = b.shape[1]
    bm, bn, bk = _tile(m, 1024), _tile(n, 1024), _tile(kk, 2048)
    return pl.pallas_call(
        functools.partial(_matmul_kernel, scale=scale),
        grid=(m // bm, n // bn, kk // bk),
        in_specs=[pl.BlockSpec((bm, bk), lambda i, j, k: (i, k)),
                  pl.BlockSpec((bk, bn), lambda i, j, k: (k, j))],
        out_specs=pl.BlockSpec((bm, bn), lambda i, j, k: (i, j)),
        out_shape=jax.ShapeDtypeStruct((m, n), out_dtype),
        scratch_shapes=[pltpu.VMEM((bm, bn), F32)],
        compiler_params=_cparams(("parallel", "parallel", "arbitrary")),
        name="dft_matmul",
    )(a, b)


def _even_out_kernel(h_ref, mod_ref, g_ref, gb_ref, z_ref, zp_ref, zn_ref, yf_ref, cw_ref, w_ref, o_ref):
    i = pl.program_id(1)
    x = h_ref[0]
    mod = mod_ref[0]
    g = g_ref[...]
    z = z_ref[0].astype(F32)
    tm = z.shape[0]
    halo = zp_ref.shape[1]
    prev_row = jnp.where(i > 0, zp_ref[0, halo - 1:halo, :].astype(F32), 0.0)
    next_row = jnp.where(i < pl.num_programs(1) - 1, zn_ref[0, 0:1, :].astype(F32), 0.0)
    row = lax.broadcasted_iota(jnp.int32, z.shape, 0)
    z_before = jnp.where(row == 0, prev_row, pltpu.roll(z, 1, 0))
    z_after = jnp.where(row == tm - 1, next_row, pltpu.roll(z, tm - 1, 0))
    cw = cw_ref[...]
    conv = z_before * cw[0:1] + z * cw[1:2] + z_after * cw[2:3]
    y_conv = (gb_ref[0].astype(F32) * conv).astype(BF16)
    y = _dot(y_conv, w_ref[:D_CONV, :]) + _dot(yf_ref[0], w_ref[D_CONV:, :])
    o_ref[0] = x + mod[5:6] * _rms_rows(y, g[3:4])


def _even_out(h, mod, g, gb, z, yf, conv_w, w_out):
    b, t, d = h.shape
    tm = _tile(t, 1024)
    halo = BF16_SUBLANES
    nh = tm // halo
    last_halo = t // halo - 1
    return pl.pallas_call(
        _even_out_kernel,
        grid=(b, t // tm),
        in_specs=[pl.BlockSpec((1, tm, d), lambda bb, i: (bb, i, 0)),
                  _mod_spec(mod),
                  _const_spec(g.shape),
                  pl.BlockSpec((1, tm, D_CONV), lambda bb, i: (bb, i, 0)),
                  pl.BlockSpec((1, tm, D_CONV), lambda bb, i: (bb, i, 0)),
                  pl.BlockSpec((1, halo, D_CONV), lambda bb, i: (bb, jnp.maximum(i * nh - 1, 0), 0)),
                  pl.BlockSpec((1, halo, D_CONV), lambda bb, i: (bb, jnp.minimum((i + 1) * nh, last_halo), 0)),
                  pl.BlockSpec((1, tm, D_FOURIER), lambda bb, i: (bb, i, 0)),
                  _const_spec(conv_w.shape),
                  _const_spec(w_out.shape)],
        out_specs=pl.BlockSpec((1, tm, d), lambda bb, i: (bb, i, 0)),
        out_shape=jax.ShapeDtypeStruct(h.shape, F32),
        compiler_params=_cparams(("parallel", "parallel")),
        name="even_out",
    )(h, mod, g, gb, z, z, z, yf, conv_w, w_out)


def _dft_tables(t):
    n = jnp.arange(t, dtype=jnp.int32)
    ang = ((n[:, None] * n[None, :]) % t).astype(F32) * (2.0 * math.pi / t)
    w_pos = jnp.concatenate([jnp.cos(ang), -jnp.sin(ang)], axis=1).astype(BF16)
    return w_pos


def _channel_dft_table():
    n = jnp.arange(D_FG, dtype=jnp.int32)
    ang = ((n[:, None] * n[None, :]) % D_FG).astype(F32) * (2.0 * math.pi / D_FG)
    return jnp.concatenate([jnp.cos(ang), jnp.sin(ang)], axis=1).astype(BF16)


def _even_mixer(h, mod, g, w_in, conv_w, w_out, dft_c, dft_pos):
    b, t, _ = h.shape
    if isinstance(dft_pos, dict):
        gb, z, xf = _even_in(h, mod, g, w_in, dft_c, False)
        yf = _factored_dft(xf, dft_pos)
    else:
        gb, z, a = _even_in(h, mod, g, w_in, dft_c, True)
        yf = _matmul(dft_pos, a.reshape(2 * t, b * D_FOURIER), 1.0 / math.sqrt(t * D_FG), BF16)
        yf = yf.reshape(t, b, D_FOURIER).transpose(1, 0, 2)
    return _even_out(h, mod, g, gb, z, yf, conv_w, w_out)


def _odd_in_kernel(h_ref, mod_ref, g_ref, wtok_ref, wt_ref, wukp_ref, wuvt_ref, wuqt_ref, place_ref, gsel_ref,
                   gkv_row_ref, gkv_col_ref, gq_col_ref,
                   cosk_ref, sink_ref, tkr_ref, cosq_ref, sinq_ref, cosm_ref, sinm_ref,
                   khd_ref, khm_ref, vtd_ref, vtm_ref, kn_ref, *q_refs, with_q, split8):
    xm = _modulated(h_ref[0], mod_ref[0], g_ref[...], 1).astype(BF16)
    tm = xm.shape[0]

    ut = _dot(xm, wtok_ref[...])
    k = ut[:, :DA_QCOLS]
    ckv = ut[:, DA_QCOLS:DA_QCOLS + MLA_KV_RANK]
    kr2 = ut[:, DA_QCOLS + MLA_KV_RANK:]
    lane = lax.broadcasted_iota(jnp.int32, k.shape, 1)
    first_half = (lane % (DA_DK // 2)) < (DA_DK // 4)
    k_sw = jnp.where(first_half, pltpu.roll(k, DA_QCOLS - DA_DK // 4, 1), pltpu.roll(k, DA_DK // 4, 1))
    reps = DA_QCOLS // HEAD_PAD
    cosk = jnp.concatenate([cosk_ref[...]] * reps, axis=1)
    sink = jnp.concatenate([sink_ref[...]] * reps, axis=1)
    k_rot = k * cosk + k_sw * sink
    ckvn = _rms_rows(ckv, gkv_row_ref[...]).astype(BF16)
    k_nope = _dot(ckvn, wukp_ref[...])
    pr = kr2 * tkr_ref[...]
    pr_hi = pr.astype(BF16)
    pr_lo = (pr - pr_hi.astype(F32)).astype(BF16)
    k_mla = k_nope + _dot(pr_hi, place_ref[...]) + _dot(pr_lo, place_ref[...])
    lane_h = lax.broadcasted_iota(jnp.int32, (tm, HEAD_PAD), 1)
    low_half = lane_h < DA_DK
    to_low = functools.partial(pltpu.roll, shift=DA_QCOLS - DA_DK, axis=1)
    if split8:
        k_hi, k_lo = _split8(k_rot * DA_SPLIT_SCALE)
        k_eff = k_hi + k_lo
        lo_up, hi_dn = pltpu.roll(k_lo, DA_DK, 1), to_low(k_hi)
        for hh in range(DA_HEADS):
            tl = slice(hh * HEAD_PAD, (hh + 1) * HEAD_PAD)
            khd_ref[0, 2 * hh] = jnp.where(low_half, k_hi[:, tl], lo_up[:, tl]).astype(F8)
            khd_ref[0, 2 * hh + 1] = jnp.where(low_half, hi_dn[:, tl], k_lo[:, tl]).astype(F8)
    else:
        k_eff = k_rot.astype(BF16).astype(F32)
        k_dn = to_low(k_rot)
        for hh in range(DA_HEADS):
            tl = slice(hh * HEAD_PAD, (hh + 1) * HEAD_PAD)
            for m, src in enumerate((k_rot, k_dn)):
                khd_ref[0, 2 * hh + m] = jnp.where(lane_h == DA_DK, 1.0,
                                                   jnp.where(low_half, src[:, tl], 0.0)).astype(BF16)
    sq = [(k_eff * k_eff).astype(BF16)]
    for hh in range(MLA_HEADS):
        kd = k_mla[:, hh * HEAD_PAD:(hh + 1) * HEAD_PAD]
        khm_ref[0, hh] = jnp.where(lane_h == MLA_DQK, 1.0, kd).astype(BF16)
        kf = kd.astype(BF16).astype(F32)
        sq.append((kf * kf).astype(BF16))
    kn_ref[0] = _dot_nt(gsel_ref[...], jnp.concatenate(sq, axis=1))

    r0 = DA_VCOLS
    vckv = _dot_nt(wt_ref[:r0 + MLA_KV_RANK, :], xm)
    for hh in range(DA_HEADS):
        vtd_ref[0, hh] = vckv[hh * DA_DV:(hh + 1) * DA_DV].astype(BF16)
    ckv_t = vckv[r0:]
    ms = jnp.mean(ckv_t * ckv_t, axis=0, keepdims=True)
    ckvn_t = (ckv_t * lax.rsqrt(ms + EPS) * gkv_col_ref[...]).astype(BF16)
    vm_t = _dot(wuvt_ref[...], ckvn_t)
    for hh in range(MLA_HEADS):
        vtm_ref[0, hh] = vm_t[hh * MLA_DV:(hh + 1) * MLA_DV].astype(BF16)

    if with_q:
        qtd_ref, qtm_ref = q_refs
        r1 = r0 + MLA_KV_RANK
        q_t = _dot_nt(wt_ref[r1:, :], xm)
        cosq, sinq = cosq_ref[...], sinq_ref[...]
        qd = DA_DK // 4
        for mp in range(2 * DA_HEADS):
            q = q_t[mp * DA_DK:(mp + 1) * DA_DK]
            q_sw = jnp.concatenate([q[qd:2 * qd], q[:qd], q[3 * qd:], q[2 * qd:3 * qd]], axis=0)
            q_rot = q * cosq + q_sw * sinq
            if split8:
                q_hi, q_lo = _split8(q_rot * DA_SPLIT_SCALE)
                qtd_ref[0, mp, :DA_DK, :] = q_hi.astype(F8)
                qtd_ref[0, mp, DA_DK:, :] = q_lo.astype(F8)
            else:
                qtd_ref[0, mp // 2, (mp % 2) * DA_DK:(mp % 2 + 1) * DA_DK, :] = (
                    q_rot * (DA_SCALE * LOG2E)).astype(BF16)
        cq_t = q_t[DA_QCOLS:]
        ms = jnp.mean(cq_t * cq_t, axis=0, keepdims=True)
        cqn_t = (cq_t * lax.rsqrt(ms + EPS) * gq_col_ref[...]).astype(BF16)
        qm_t = _dot(wuqt_ref[...], cqn_t) * (MLA_SCALE * LOG2E)
        cosm, sinm = cosm_ref[...], sinm_ref[...]
        rd = MLA_ROPE // 4
        for hh in range(MLA_HEADS):
            base = hh * HEAD_PAD
            qtm_ref[0, hh, :MLA_NOPE, :] = qm_t[base:base + MLA_NOPE].astype(BF16)
            r = qm_t[base + MLA_NOPE:base + MLA_DQK]
            r_sw = jnp.concatenate([r[rd:2 * rd], r[:rd], r[3 * rd:], r[2 * rd:3 * rd]], axis=0)
            qtm_ref[0, hh, MLA_NOPE:MLA_DQK, :] = (r * cosm + r_sw * sinm).astype(BF16)
            qtm_ref[0, hh, MLA_DQK:, :] = jnp.zeros((HEAD_PAD - MLA_DQK, tm), BF16)


def _odd_in(h, mod, g, wts, tabs, kv_prev, tok_off, with_q, split8):
    b, t, d = h.shape
    da_dtype = F8 if split8 else BF16
    tk_total = tabs["tk_total"]
    tm = _tile(t, 512)
    assert tok_off % tm == 0
    off = tok_off // tm
    row_tab = lambda w: pl.BlockSpec((tm, w), lambda bb, i: (i, 0))
    col_tab = lambda r: pl.BlockSpec((r, tm), lambda bb, i: (0, i))
    consts = [wts["wtok"], wts["wt"], wts["wukp"], wts["wuvt"], wts["wuqt"], wts["place"], wts["gsel"],
              wts["gkv_row"], wts["gkv_col"], wts["gq_col"]]
    in_specs = ([pl.BlockSpec((1, tm, d), lambda bb, i: (bb, i, 0)), _mod_spec(mod), _const_spec(g.shape)]
                + [_const_spec(c.shape) for c in consts]
                + [row_tab(HEAD_PAD), row_tab(HEAD_PAD), row_tab(2 * MLA_ROPE),
                   col_tab(DA_DK), col_tab(DA_DK), col_tab(MLA_ROPE), col_tab(MLA_ROPE)])
    args = [h, mod, g] + consts + [tabs["cosk"], tabs["sink"], tabs["tkr"],
                                   tabs["cosq"], tabs["sinq"], tabs["cosm"], tabs["sinm"]]
    out_specs = [pl.BlockSpec((1, 2 * DA_HEADS, tm, HEAD_PAD), lambda bb, i: (bb, 0, i + off, 0)),
                 pl.BlockSpec((1, MLA_HEADS, tm, HEAD_PAD), lambda bb, i: (bb, 0, i + off, 0)),
                 pl.BlockSpec((1, DA_HEADS, DA_DV, tm), lambda bb, i: (bb, 0, 0, i + off)),
                 pl.BlockSpec((1, MLA_HEADS, MLA_DV, tm), lambda bb, i: (bb, 0, 0, i + off)),
                 pl.BlockSpec((1, N_MAPS, tm), lambda bb, i: (bb, 0, i + off))]
    out_shape = [jax.ShapeDtypeStruct((b, 2 * DA_HEADS, tk_total, HEAD_PAD), da_dtype),
                 jax.ShapeDtypeStruct((b, MLA_HEADS, tk_total, HEAD_PAD), BF16),
                 jax.ShapeDtypeStruct((b, DA_HEADS, DA_DV, tk_total), BF16),
                 jax.ShapeDtypeStruct((b, MLA_HEADS, MLA_DV, tk_total), BF16),
                 jax.ShapeDtypeStruct((b, N_MAPS, tk_total), F32)]
    if with_q:
        n_qd = 2 * DA_HEADS if split8 else DA_HEADS
        out_specs += [pl.BlockSpec((1, n_qd, HEAD_PAD, tm), lambda bb, i: (bb, 0, 0, i)),
                      pl.BlockSpec((1, MLA_HEADS, HEAD_PAD, tm), lambda bb, i: (bb, 0, 0, i))]
        out_shape += [jax.ShapeDtypeStruct((b, n_qd, HEAD_PAD, t), da_dtype),
                      jax.ShapeDtypeStruct((b, MLA_HEADS, HEAD_PAD, t), BF16)]
    aliases = {}
    kernel_fn = functools.partial(_odd_in_kernel, with_q=with_q, split8=split8)
    if kv_prev is not None:
        n_in = len(args)
        in_specs += [pl.BlockSpec(memory_space=pl.ANY)] * N_KV_ARRAYS
        args += list(kv_prev)
        aliases = {n_in + a: a for a in range(N_KV_ARRAYS)}
        kernel_fn = functools.partial(_odd_in_alias_kernel, n_in=n_in, with_q=with_q, split8=split8)
    return pl.pallas_call(
        kernel_fn,
        grid=(b, t // tm),
        in_specs=in_specs,
        out_specs=out_specs,
        out_shape=out_shape,
        input_output_aliases=aliases,
        compiler_params=_cparams(("parallel", "parallel")),
        name="odd_in",
    )(*args)


def _odd_in_alias_kernel(*refs, n_in, with_q, split8):
    _odd_in_kernel(*refs[:n_in], *refs[n_in + N_KV_ARRAYS:], with_q=with_q, split8=split8)


def _attn_kernel(lam_ref, gsub_ref, kn_ref, qtd_ref, qtm_ref, khd_ref, khm_ref, vtd_ref, vtm_ref,
                 otd_ref, otm_ref, lmin_ref, qad_ref, qam_ref, shift_ref, m_ref, l_ref, accd_ref, accm_ref,
                 *, lam_init, safe):
    j = pl.program_id(2)
    tq = qtd_ref.shape[-1]

    @pl.when(j == 0)
    def _():
        m_ref[...] = jnp.full_like(m_ref, NEG_BIG)
        l_ref[...] = jnp.zeros_like(l_ref)
        accd_ref[...] = jnp.zeros_like(accd_ref)
        accm_ref[...] = jnp.zeros_like(accm_ref)
        kmax = jnp.sqrt(jnp.max(kn_ref[0], axis=1, keepdims=True))
        shift_row = lax.broadcasted_iota(jnp.int32, (BF16_SUBLANES, tq), 0) == 0
        for idx in range(2 * DA_HEADS):
            if safe:
                q = qtd_ref[0, idx // 2, (idx % 2) * DA_DK:(idx % 2 + 1) * DA_DK, :]
                qad_ref[idx] = jnp.concatenate([q, jnp.zeros((HEAD_PAD - DA_DK, tq), BF16)], axis=0)
            else:
                q_hi, q_lo = qtd_ref[0, idx, :DA_DK, :], qtd_ref[0, idx, DA_DK:, :]
                qad_ref[idx] = jnp.concatenate([q_hi, q_hi, q_lo, q_lo], axis=0)
                qf = q_hi.astype(F32) + q_lo.astype(F32)
                qn = jnp.sqrt(jnp.sum(qf * qf, axis=0, keepdims=True))
                ok = (qn < F8_SAFE_MAX) & (kmax[idx:idx + 1] < F8_SAFE_MAX)
                shift_ref[idx] = jnp.where(ok, (BOUND_SLACK * kmax[idx:idx + 1]) * qn, jnp.inf)
        for hh in range(MLA_HEADS):
            idx = 2 * DA_HEADS + hh
            q = qtm_ref[0, hh, :MLA_DQK, :]
            if safe:
                shift_blk = jnp.zeros((BF16_SUBLANES, tq), BF16)
            else:
                qf = q.astype(F32)
                qn = jnp.sqrt(jnp.sum(qf * qf, axis=0, keepdims=True))
                shift = -(BOUND_SLACK * kmax[idx:idx + 1]) * qn
                shift_blk = jnp.where(shift_row, shift, 0.0).astype(BF16)
            pad = jnp.zeros((HEAD_PAD - MLA_DQK - BF16_SUBLANES, tq), BF16)
            qam_ref[hh] = jnp.concatenate([q, shift_blk, pad], axis=0)

    def softmax_pv(s, ln, idx, v_t, acc_ref, acc_idx):
        if safe:
            m_old = m_ref[idx, :, ln]
            m_new = jnp.maximum(m_old, jnp.max(s, axis=0, keepdims=True))
            alpha = jnp.exp2(m_old - m_new)
            m_ref[idx, :, ln] = m_new
            p = jnp.exp2(s - m_new)
            l_ref[idx, :, ln] = alpha * l_ref[idx, :, ln] + jnp.sum(p.reshape(-1, F32_SUBLANES, s.shape[1]), axis=0)
            acc_ref[acc_idx, :, ln] = acc_ref[acc_idx, :, ln] * alpha + _dot(v_t, p.astype(BF16))
        else:
            p = jnp.exp2(s)
            l_ref[idx, :, ln] += jnp.sum(p.reshape(-1, F32_SUBLANES, s.shape[1]), axis=0)
            acc_ref[acc_idx, :, ln] += _dot(v_t, p.astype(BF16))

    chunk = ATTN_Q_CHUNK if (not safe and tq % ATTN_Q_CHUNK == 0) else tq
    q_chunks = [slice(c, c + chunk) for c in range(0, tq, chunk)]
    da_per_iter = 1 if safe else DA_HEADS_PER_ITER
    mla_per_iter = 1 if safe else MLA_HEADS_PER_ITER

    def da_score(hh, m, ln):
        keys = khd_ref[0, 2 * hh + m]
        if safe:
            return _dot(keys, qad_ref[2 * hh + m, :, ln])
        return _dot(jnp.concatenate([keys, keys], axis=1), qad_ref[2 * hh + m, :, ln])

    def mla_score(hh, ln):
        return _dot(khm_ref[0, hh], qam_ref[hh, :, ln])

    def run_maps(scores, consume):
        s_next = scores[0]()
        for n in range(len(scores)):
            s_cur = s_next
            if n + 1 < len(scores):
                s_next = scores[n + 1]()
            consume[n](s_cur)

    def da_units(hh, scores, consume):
        for m in range(2):
            for ln in q_chunks:
                scores.append(functools.partial(da_score, hh, m, ln))
                consume.append(lambda s, m=m, ln=ln: softmax_pv(
                    s if safe else s - shift_ref[2 * hh + m, :, ln], ln, 2 * hh + m, vtd_ref[0, hh],
                    accd_ref, 2 * hh + m))

    def mla_units(hh, scores, consume):
        mchunk = MLA_Q_CHUNK if (not safe and tq % MLA_Q_CHUNK == 0) else tq
        for c in range(0, tq, mchunk):
            ln = slice(c, c + mchunk)
            scores.append(functools.partial(mla_score, hh, ln))
            consume.append(lambda s, ln=ln: softmax_pv(s, ln, 2 * DA_HEADS + hh, vtm_ref[0, hh], accm_ref, hh))

    def da_body(it, carry):
        scores, consume = [], []
        for u in range(da_per_iter):
            da_units(it * da_per_iter + u, scores, consume)
        run_maps(scores, consume)
        return carry

    def mla_body(it, carry):
        scores, consume = [], []
        for u in range(mla_per_iter):
            mla_units(it * mla_per_iter + u, scores, consume)
        run_maps(scores, consume)
        return carry

    lax.fori_loop(0, DA_HEADS // da_per_iter, da_body, 0)
    lax.fori_loop(0, MLA_HEADS // mla_per_iter, mla_body, 0)

    @pl.when(j == pl.num_programs(2) - 1)
    def _():
        lp = lam_ref[...]
        lam = (jnp.exp(jnp.sum(lp[0:1] * lp[1:2], axis=1, keepdims=True))
               - jnp.exp(jnp.sum(lp[2:3] * lp[3:4], axis=1, keepdims=True)) + lam_init)
        gsub = gsub_ref[...]
        lsum = [jnp.sum(l_ref[idx], axis=0, keepdims=True) for idx in range(N_MAPS)]
        linv = [1.0 / ls for ls in lsum]
        for hh in range(DA_HEADS):
            o = accd_ref[2 * hh] * linv[2 * hh] - accd_ref[2 * hh + 1] * (lam * linv[2 * hh + 1])
            ms = jnp.mean(o * o, axis=0, keepdims=True)
            otd_ref[0, hh] = (o * (lax.rsqrt(ms + EPS) * (1.0 - lam_init)) * gsub).astype(otd_ref.dtype)
        for hh in range(MLA_HEADS):
            otm_ref[0, hh] = (accm_ref[hh] * linv[2 * DA_HEADS + hh]).astype(otm_ref.dtype)
        lmin = lsum[0]
        for idx in range(1, N_MAPS):
            lmin = jnp.minimum(lmin, lsum[idx])
        lmin_ref[0, 0] = jnp.broadcast_to(jnp.min(lmin, axis=1, keepdims=True), lmin_ref.shape[2:])


def _attention_call(lam_p, gsub_col, qtd, qtm, kv, lam_init, tq, tk, kv_off, n_kv, safe):
    khd, khm, vtd, vtm, kn = kv
    b, n_qd, _, t = qtd.shape
    nq = t // tq
    assert kv_off % n_kv == 0
    qad_scratch = (pltpu.VMEM((2 * DA_HEADS, HEAD_PAD, tq), BF16) if safe
                   else pltpu.VMEM((2 * DA_HEADS, 4 * DA_DK, tq), F8))
    return pl.pallas_call(
        functools.partial(_attn_kernel, lam_init=lam_init, safe=safe),
        grid=(b, nq, n_kv),
        in_specs=[_const_spec(lam_p.shape),
                  _const_spec(gsub_col.shape),
                  pl.BlockSpec((1, N_MAPS, n_kv * tk), lambda bb, i, j: (bb, 0, kv_off // n_kv)),
                  pl.BlockSpec((1, n_qd, HEAD_PAD, tq), lambda bb, i, j: (bb, 0, 0, i)),
                  pl.BlockSpec((1, MLA_HEADS, HEAD_PAD, tq), lambda bb, i, j: (bb, 0, 0, i)),
                  pl.BlockSpec((1, 2 * DA_HEADS, tk, HEAD_PAD), lambda bb, i, j: (bb, 0, kv_off + j, 0)),
                  pl.BlockSpec((1, MLA_HEADS, tk, HEAD_PAD), lambda bb, i, j: (bb, 0, kv_off + j, 0)),
                  pl.BlockSpec((1, DA_HEADS, DA_DV, tk), lambda bb, i, j: (bb, 0, 0, kv_off + j)),
                  pl.BlockSpec((1, MLA_HEADS, MLA_DV, tk), lambda bb, i, j: (bb, 0, 0, kv_off + j))],
        out_specs=[pl.BlockSpec((1, DA_HEADS, DA_DV, tq), lambda bb, i, j: (bb, 0, 0, i)),
                   pl.BlockSpec((1, MLA_HEADS, MLA_DV, tq), lambda bb, i, j: (bb, 0, 0, i)),
                   pl.BlockSpec((1, 1, F32_SUBLANES, HEAD_PAD), lambda bb, i, j: (bb, i, 0, 0))],
        out_shape=[jax.ShapeDtypeStruct((b, DA_HEADS, DA_DV, t), BF16),
                   jax.ShapeDtypeStruct((b, MLA_HEADS, MLA_DV, t), BF16),
                   jax.ShapeDtypeStruct((b, nq, F32_SUBLANES, HEAD_PAD), F32)],
        scratch_shapes=[qad_scratch,
                        pltpu.VMEM((MLA_HEADS, HEAD_PAD, tq), BF16),
                        pltpu.VMEM((2 * DA_HEADS, 1, tq), F32),
                        pltpu.VMEM((N_MAPS, 1, tq), F32),
                        pltpu.VMEM((N_MAPS, F32_SUBLANES, tq), F32),
                        pltpu.VMEM((2 * DA_HEADS, DA_DV, tq), F32),
                        pltpu.VMEM((MLA_HEADS, MLA_DV, tq), F32)],
        compiler_params=_cparams(("parallel", "parallel", "arbitrary")),
        name="attention_safe" if safe else "attention",
    )(lam_p, gsub_col, kn, qtd, qtm, khd, khm, vtd, vtm)


def _odd_out_kernel(h_ref, mod_ref, g_ref, otd_ref, otm_ref, wt_ref, o_ref):
    y_t = _dot(wt_ref[:, :DA_VCOLS], otd_ref[0]) + _dot(wt_ref[:, DA_VCOLS:], otm_ref[0])
    y = y_t.T
    o_ref[0] = h_ref[0] + mod_ref[0][5:6] * _rms_rows(y, g_ref[3:4])


def _odd_out(h, mod, g, otd, otm, w_out_t):
    b, t, d = h.shape
    tm = _tile(t, 1024)
    otd = otd.reshape(b, DA_VCOLS, t)
    otm = otm.reshape(b, MLA_HEADS * MLA_DV, t)
    return pl.pallas_call(
        _odd_out_kernel,
        grid=(b, t // tm),
        in_specs=[pl.BlockSpec((1, tm, d), lambda bb, i: (bb, i, 0)),
                  _mod_spec(mod),
                  _const_spec(g.shape),
                  pl.BlockSpec((1, DA_VCOLS, tm), lambda bb, i: (bb, 0, i)),
                  pl.BlockSpec((1, MLA_HEADS * MLA_DV, tm), lambda bb, i: (bb, 0, i)),
                  _const_spec(w_out_t.shape)],
        out_specs=pl.BlockSpec((1, tm, d), lambda bb, i: (bb, i, 0)),
        out_shape=jax.ShapeDtypeStruct(h.shape, F32),
        compiler_params=_cparams(("parallel", "parallel")),
        name="odd_out",
    )(h, mod, g, otd, otm, w_out_t)


def _rope_angles(n_tokens, rot_dim):
    rows = n_tokens // GRID_W
    row = jnp.broadcast_to(jnp.arange(rows)[:, None], (rows, GRID_W)).reshape(-1).astype(F32)
    col = jnp.broadcast_to(jnp.arange(GRID_W)[None, :], (rows, GRID_W)).reshape(-1).astype(F32)
    n_freq = rot_dim // 4
    freqs = ROPE_THETA ** (-jnp.arange(n_freq, dtype=F32) / n_freq)
    return row[:, None] * freqs, col[:, None] * freqs


def _rope_cos_sin(n_tokens, rot_dim, identity):
    if identity:
        return jnp.ones((n_tokens, rot_dim), F32), jnp.zeros((n_tokens, rot_dim), F32)
    ar, ac = _rope_angles(n_tokens, rot_dim)
    cos = jnp.concatenate([jnp.cos(ar), jnp.cos(ar), jnp.cos(ac), jnp.cos(ac)], axis=1)
    sin = jnp.concatenate([-jnp.sin(ar), jnp.sin(ar), -jnp.sin(ac), jnp.sin(ac)], axis=1)
    return cos, sin


def _rope_tables(n_tokens, identity, tk_total):
    cd, sd = _rope_cos_sin(n_tokens, DA_DK, identity)
    cm, sm = _rope_cos_sin(n_tokens, MLA_ROPE, identity)
    return {
        "cosk": jnp.concatenate([cd, cd], axis=1), "sink": jnp.concatenate([sd, sd], axis=1),
        "tkr": jnp.concatenate([cm, sm], axis=1),
        "cosq": cd.T, "sinq": sd.T, "cosm": cm.T, "sinm": sm.T,
        "tk_total": tk_total,
    }


def _swap_perm(rot_dim):
    q = rot_dim // 4
    return jnp.concatenate([jnp.arange(q, 2 * q), jnp.arange(0, q), jnp.arange(3 * q, 4 * q), jnp.arange(2 * q, 3 * q)])


def _map_lane_selector():
    da = jnp.kron(jnp.eye(2 * DA_HEADS, dtype=F32), jnp.ones((1, DA_DK), F32))
    mla = jnp.kron(jnp.eye(MLA_HEADS, dtype=F32), jnp.ones((1, HEAD_PAD), F32))
    top = jnp.concatenate([da, jnp.zeros((2 * DA_HEADS, MLA_HEADS * HEAD_PAD), F32)], axis=1)
    bottom = jnp.concatenate([jnp.zeros((MLA_HEADS, DA_QCOLS), F32), mla], axis=1)
    return jnp.concatenate([top, bottom], axis=0).astype(BF16)


def _odd_weights(w_in, g_q, w_uq, g_kv, w_uk, w_uv):
    w_q = w_in[:, :DA_QCOLS]
    w_cq = w_in[:, DA_QCOLS:Q_COLS]
    w_k = w_in[:, Q_COLS:Q_COLS + DA_QCOLS]
    w_v = w_in[:, Q_COLS + DA_QCOLS:Q_COLS + DA_QCOLS + DA_VCOLS]
    w_ckv = w_in[:, Q_COLS + DA_QCOLS + DA_VCOLS:Q_COLS + DA_QCOLS + DA_VCOLS + MLA_KV_RANK]
    w_kr = w_in[:, Q_COLS + DA_QCOLS + DA_VCOLS + MLA_KV_RANK:]
    wtok = jnp.concatenate([w_k, w_ckv, w_kr, w_kr[:, _swap_perm(MLA_ROPE)]], axis=1).astype(BF16)
    wt = jnp.concatenate([w_v, w_ckv, w_q, w_cq], axis=1).T.astype(BF16)
    pad_k = jnp.zeros((MLA_KV_RANK, MLA_HEADS, HEAD_PAD), F32)
    wukp = pad_k.at[:, :, :MLA_NOPE].set(w_uk.reshape(MLA_KV_RANK, MLA_HEADS, MLA_NOPE))
    wukp = wukp.reshape(MLA_KV_RANK, MLA_HEADS * HEAD_PAD).astype(BF16)
    pad_q = jnp.zeros((MLA_Q_RANK, MLA_HEADS, HEAD_PAD), F32)
    wuqp = pad_q.at[:, :, :MLA_DQK].set(w_uq.reshape(MLA_Q_RANK, MLA_HEADS, MLA_DQK))
    wuqt = wuqp.reshape(MLA_Q_RANK, MLA_HEADS * HEAD_PAD).T.astype(BF16)
    eye = jnp.eye(MLA_ROPE, dtype=F32)
    place = jnp.zeros((2, MLA_ROPE, MLA_HEADS, HEAD_PAD), F32)
    place = place.at[:, :, :, MLA_NOPE:MLA_DQK].set(jnp.broadcast_to(eye[None, :, None, :], (2, MLA_ROPE, MLA_HEADS, MLA_ROPE)))
    place = place.reshape(2 * MLA_ROPE, MLA_HEADS * HEAD_PAD).astype(BF16)
    return {
        "wtok": wtok, "wt": wt, "wukp": wukp, "wuvt": w_uv.T.astype(BF16), "wuqt": wuqt, "place": place,
        "gsel": _map_lane_selector(),
        "gkv_row": g_kv.reshape(1, -1).astype(F32), "gkv_col": g_kv.reshape(-1, 1).astype(F32),
        "gq_col": g_q.reshape(-1, 1).astype(F32),
    }


def kernel(x, c, ctx, c_ctx, w_mod, b_mod, norm_g, w_ffn_in, w_ffn_out, w_in_even, conv_w, w_out_even,
           w_in_odd, g_q_mla, w_uq, g_kv_mla, w_uk, w_uv, lam_q1, lam_k1, lam_q2, lam_k2, g_subln, w_out_odd):
    b, t, d = x.shape
    tc = ctx.shape[1]
    depth = w_mod.shape[0]
    tk_total = t + tc

    rows = -(-(b + 1) // F32_SUBLANES) * F32_SUBLANES
    cond = jnp.zeros((rows, d), F32).at[:b].set(c).at[b].set(c_ctx)
    mod = _adaln(cond, w_mod, b_mod).reshape(depth, rows, N_MOD, d)

    dft_c = _channel_dft_table()
    factored = lambda n: n % (DFT_COLS * BF16_SUBLANES) == 0
    w_pos_x = _factored_dft_tables(t) if factored(t) else _dft_tables(t)
    w_pos_c = _factored_dft_tables(tc) if factored(tc) else _dft_tables(tc)
    tabs_x = _rope_tables(t, False, tk_total)
    tabs_c = _rope_tables(tc, True, tk_total)
    tq = _tile(t, 1024)
    tk = 768 if tk_total % 768 == 0 else _tile(tk_total, 512)
    assert t % tc == 0 and tk_total % tk == 0

    h, hc = x, ctx
    for l in range(depth):
        last = l == depth - 1
        odd = l % 2 == 1
        ctx_live = (not last) or odd
        g = norm_g[l]
        m_x, m_c = mod[l, :b], mod[l, b:b + 1]
        wi0, wo0 = w_ffn_in[l, 0].astype(BF16), w_ffn_out[l, 0].astype(BF16)
        h = _ffn(h, m_x, g, wi0, wo0, 0)
        if ctx_live:
            hc = _ffn(hc, m_c, g, wi0, wo0, 0)

        if not odd:
            e = l // 2
            w_in, w_out = w_in_even[e].astype(BF16), w_out_even[e].astype(BF16)
            h = _even_mixer(h, m_x, g, w_in, conv_w[e], w_out, dft_c, w_pos_x)
            if ctx_live:
                hc = _even_mixer(hc, m_c, g, w_in, conv_w[e], w_out, dft_c, w_pos_c)
        else:
            o = l // 2
            lam_init = 0.8 - 0.6 * math.exp(-0.3 * l)
            wts = _odd_weights(w_in_odd[o], g_q_mla[o], w_uq[o], g_kv_mla[o], w_uk[o], w_uv[o])
            lam_p = jnp.stack([lam_q1[o], lam_k1[o], lam_q2[o], lam_k2[o]]).astype(F32)
            gsub_col = g_subln[o].reshape(-1, 1).astype(F32)
            w_out_t = w_out_odd[o].T.astype(BF16)
            def latent_attention(safe, h=h, hc=hc, m_x=m_x, m_c=m_c, g=g, wts=wts, lam_p=lam_p,
                                 gsub_col=gsub_col, lam_init=lam_init):
                outs_x = _odd_in(h, m_x, g, wts, tabs_x, None, 0, True, not safe)
                kv = _odd_in(hc, m_c, g, wts, tabs_c, outs_x[:N_KV_ARRAYS], t, False, not safe)
                return _attention_call(lam_p, gsub_col, *outs_x[N_KV_ARRAYS:], kv, lam_init, tq, tk, 0,
                                       tk_total // tk, safe)

            otd, otm, lmin = latent_attention(False)
            otd, otm = lax.cond(jnp.min(lmin) >= MIN_DENOMINATOR, lambda: (otd, otm),
                                lambda: tuple(latent_attention(True)[:2]))
            if not last:
                outs_c = _odd_in(hc, m_c, g, wts, dict(tabs_c, tk_total=tc), None, 0, True, False)
                ocd, ocm = _attention_call(lam_p, gsub_col, *outs_c[N_KV_ARRAYS:], outs_c[:N_KV_ARRAYS],
                                           lam_init, tc, tc, 0, 1, True)[:2]
                hc = _odd_out(hc, m_c, g, ocd, ocm, w_out_t)
            h = _odd_out(h, m_x, g, otd, otm, w_out_t)

        wi1, wo1 = w_ffn_in[l, 1].astype(BF16), w_ffn_out[l, 1].astype(BF16)
        h = _ffn(h, m_x, g, wi1, wo1, 2)
        if not last:
            hc = _ffn(hc, m_c, g, wi1, wo1, 2)
    return h
```

```python
import functools
import math

import jax
import jax.numpy as jnp
from jax import lax
from jax.experimental import pallas as pl
from jax.experimental.pallas import tpu as pltpu

F32 = jnp.float32
BF16 = jnp.bfloat16

D_MODEL = 1024
GRID_W = 64
N_MOD = 9
FFN_RES = 0.5
EPS = 1e-6
ROPE_THETA = 10000.0

D_CONV = 512
D_FOURIER = 512
FOURIER_GROUPS = 4
D_FG = D_FOURIER // FOURIER_GROUPS

DA_HEADS = 8
DA_DK = 64
DA_DV = 128
DA_SCALE = DA_DK ** -0.5
MLA_HEADS = 8
MLA_NOPE = 64
MLA_ROPE = 32
MLA_DQK = MLA_NOPE + MLA_ROPE
MLA_DV = 64
MLA_Q_RANK = 384
MLA_KV_RANK = 256
MLA_SCALE = MLA_DQK ** -0.5
DA_QCOLS = DA_HEADS * 2 * DA_DK
DA_VCOLS = DA_HEADS * DA_DV
Q_COLS = DA_QCOLS + MLA_Q_RANK
HEAD_PAD = 128
F32_SUBLANES = 8
BF16_SUBLANES = 16
N_MAPS = 2 * DA_HEADS + MLA_HEADS
LOG2E = 1.4426950408889634
NEG_BIG = -1e30
N_KV_ARRAYS = 5
F8 = jnp.float8_e4m3fn
F8_SAFE_MAX = 400.0
DA_SPLIT_SCALE = math.sqrt(DA_SCALE * LOG2E)
FFN_ROW_SPLITS = 4
FFN_GROUP_ROWS = 256
DA_HEADS_PER_ITER = 4
MLA_HEADS_PER_ITER = 8
ATTN_Q_CHUNK = 256
MLA_Q_CHUNK = 512
BOUND_SLACK = 1.02
MIN_DENOMINATOR = 2.0 ** -40

VMEM_LIMIT_V7X = 56 * 1024 * 1024


def _cparams(sem):
    return pltpu.CompilerParams(dimension_semantics=sem, vmem_limit_bytes=VMEM_LIMIT_V7X)


def _tile(n, pref):
    if n <= pref:
        return n
    t = pref - pref % 128
    while t >= 128:
        if n % t == 0:
            return t
        t -= 128
    return n


def _const_spec(shape):
    nd = len(shape)
    return pl.BlockSpec(shape, lambda *_: (0,) * nd, pipeline_mode=pl.Buffered(1))


def _mod_spec(mod):
    if mod.shape[0] == 1:
        return pl.BlockSpec((1, N_MOD, D_MODEL), lambda b, *_: (0, 0, 0))
    return pl.BlockSpec((1, N_MOD, D_MODEL), lambda b, *_: (b, 0, 0))


def _rms_rows(x, g):
    ms = jnp.mean(x * x, axis=-1, keepdims=True)
    return x * lax.rsqrt(ms + EPS) * g


def _modulated(x, mod, g, slot):
    shift = mod[3 * slot:3 * slot + 1]
    scale = mod[3 * slot + 1:3 * slot + 2]
    return _rms_rows(x, g[2 * slot:2 * slot + 1]) * (1.0 + scale) + shift


def _split8(x):
    hi = x.astype(F8).astype(F32)
    lo = (x - hi).astype(F8).astype(F32)
    return hi, lo


def _dot(a, b):
    return jnp.dot(a, b, preferred_element_type=F32)


def _dot_nt(a, b):
    return lax.dot_general(a, b, (((1,), (1,)), ((), ())), preferred_element_type=F32)


def _adaln_kernel(c_ref, w_ref, b_ref, o_ref):
    c = c_ref[...]
    a = c * jax.nn.sigmoid(c)
    o_ref[0] = jnp.dot(a, w_ref[0], preferred_element_type=F32,
                       precision=lax.Precision.HIGHEST) + b_ref[0]


def _adaln(cond, w_mod, b_mod):
    depth, d, n = w_mod.shape
    rows = cond.shape[0]
    tn = _tile(n, 1152)
    return pl.pallas_call(
        _adaln_kernel,
        grid=(depth, n // tn),
        in_specs=[pl.BlockSpec((rows, d), lambda l, j: (0, 0)),
                  pl.BlockSpec((1, d, tn), lambda l, j: (l, 0, j)),
                  pl.BlockSpec((1, 1, tn), lambda l, j: (l, 0, j))],
        out_specs=pl.BlockSpec((1, rows, tn), lambda l, j: (l, 0, j)),
        out_shape=jax.ShapeDtypeStruct((depth, rows, n), F32),
        compiler_params=_cparams(("parallel", "parallel")),
        name="adaln",
    )(cond, w_mod, b_mod.reshape(depth, 1, n))


def _ffn_kernel(h_ref, mod_ref, g_ref, win_ref, wout_ref, o_ref, *, slot, d_ff):
    mod = mod_ref[0]
    g = g_ref[...]
    res_gate = mod[3 * slot + 2:3 * slot + 3]
    tm = h_ref.shape[1]
    n_groups = max(1, tm // FFN_GROUP_ROWS)
    rows = tm // n_groups
    for part in range(n_groups):
        x = h_ref[0, part * rows:(part + 1) * rows, :]
        xm = _modulated(x, mod, g, slot).astype(BF16)
        gate = _dot(xm, win_ref[:, :d_ff])
        up = _dot(xm, win_ref[:, d_ff:])
        act = (gate * jax.nn.sigmoid(gate) * up).astype(BF16)
        y = _dot(act, wout_ref[...])
        o_ref[0, part * rows:(part + 1) * rows, :] = (
            x + FFN_RES * res_gate * _rms_rows(y, g[2 * slot + 1:2 * slot + 2]))


def _ffn(h, mod, g, w_in, w_out, slot):
    b, t, d = h.shape
    d_ff = w_out.shape[0]
    tm = _tile(t, FFN_ROW_SPLITS * FFN_GROUP_ROWS)
    return pl.pallas_call(
        functools.partial(_ffn_kernel, slot=slot, d_ff=d_ff),
        grid=(b, t // tm),
        in_specs=[pl.BlockSpec((1, tm, d), lambda bb, i: (bb, i, 0)),
                  _mod_spec(mod),
                  _const_spec(g.shape),
                  _const_spec(w_in.shape),
                  _const_spec(w_out.shape)],
        out_specs=pl.BlockSpec((1, tm, d), lambda bb, i: (bb, i, 0)),
        out_shape=jax.ShapeDtypeStruct(h.shape, F32),
        compiler_params=_cparams(("parallel", "parallel")),
        name="ffn",
    )(h, mod, g, w_in, w_out)


def _even_in_kernel(h_ref, mod_ref, g_ref, w_ref, dft_ref, gb_ref, z_ref, a_ref, *, channel_dft):
    xm = _modulated(h_ref[0], mod_ref[0], g_ref[...], 1).astype(BF16)
    u = _dot(xm, w_ref[...])
    gb_ref[0] = u[:, :D_CONV].astype(BF16)
    z_ref[0] = (u[:, D_CONV:2 * D_CONV] * u[:, 2 * D_CONV:3 * D_CONV]).astype(BF16)
    xf = u[:, 3 * D_CONV:].astype(BF16)
    if not channel_dft:
        a_ref[0] = xf
        return
    for gi in range(FOURIER_GROUPS):
        pq = _dot(xf[:, gi * D_FG:(gi + 1) * D_FG], dft_ref[...])
        a_ref[0, :, gi * D_FG:(gi + 1) * D_FG] = pq[:, :D_FG].astype(BF16)
        a_ref[1, :, gi * D_FG:(gi + 1) * D_FG] = pq[:, D_FG:].astype(BF16)


def _even_in(h, mod, g, w_in, dft_c, channel_dft):
    b, t, d = h.shape
    tm = _tile(t, 1024)
    if channel_dft:
        a_spec = pl.BlockSpec((2, tm, D_FOURIER), lambda bb, i: (0, i, bb))
        a_shape = jax.ShapeDtypeStruct((2, t, b * D_FOURIER), BF16)
    else:
        a_spec = pl.BlockSpec((1, tm, D_FOURIER), lambda bb, i: (bb, i, 0))
        a_shape = jax.ShapeDtypeStruct((b, t, D_FOURIER), BF16)
    return pl.pallas_call(
        functools.partial(_even_in_kernel, channel_dft=channel_dft),
        grid=(b, t // tm),
        in_specs=[pl.BlockSpec((1, tm, d), lambda bb, i: (bb, i, 0)),
                  _mod_spec(mod),
                  _const_spec(g.shape),
                  _const_spec(w_in.shape),
                  _const_spec(dft_c.shape)],
        out_specs=[pl.BlockSpec((1, tm, D_CONV), lambda bb, i: (bb, i, 0)),
                   pl.BlockSpec((1, tm, D_CONV), lambda bb, i: (bb, i, 0)),
                   a_spec],
        out_shape=[jax.ShapeDtypeStruct((b, t, D_CONV), BF16),
                   jax.ShapeDtypeStruct((b, t, D_CONV), BF16),
                   a_shape],
        compiler_params=_cparams(("parallel", "parallel")),
        name="even_in",
    )(h, mod, g, w_in, dft_c)


DFT_COLS = 64


def _dft_rows_kernel(x_ref, f_ref, tc_ref, ts_ref, o_ref):
    res = _dot(f_ref[...], x_ref[0])
    r = f_ref.shape[0] // 2
    reps = D_FOURIER // tc_ref.shape[2]
    for c in range(tc_ref.shape[0]):
        a_re = res[:r, c * D_FOURIER:(c + 1) * D_FOURIER]
        a_im = res[r:, c * D_FOURIER:(c + 1) * D_FOURIER]
        tc = jnp.concatenate([tc_ref[c]] * reps, axis=1)
        ts = jnp.concatenate([ts_ref[c]] * reps, axis=1)
        o_ref[0, 0, c] = (a_re * tc + a_im * ts).astype(BF16)
        o_ref[0, 1, c] = (a_im * tc - a_re * ts).astype(BF16)


def _dft_cols_kernel(b_ref, m_ref, cs_ref, o_ref, *, scale):
    z = _dot(m_ref[...], jnp.concatenate([b_ref[0, 0], b_ref[0, 1]], axis=0))
    kb = z.shape[1] // D_FOURIER
    for gi in range(FOURIER_GROUPS):
        lanes = [slice(k * D_FOURIER + gi * D_FG, k * D_FOURIER + (gi + 1) * D_FG) for k in range(kb)]
        z_re = jnp.concatenate([z[:DFT_COLS, ln] for ln in lanes], axis=0)
        z_im = jnp.concatenate([z[DFT_COLS:, ln] for ln in lanes], axis=0)
        y = _dot(jnp.concatenate([z_re, z_im], axis=1).astype(BF16), cs_ref[...]) * scale
        for k in range(kb):
            o_ref[0, :, lanes[k]] = y[k * DFT_COLS:(k + 1) * DFT_COLS].astype(o_ref.dtype)


def _factored_dft(xf, tabs):
    b, t, _ = xf.shape
    r = t // DFT_COLS
    cb = kb = F32_SUBLANES
    rows = pl.pallas_call(
        _dft_rows_kernel,
        grid=(b, DFT_COLS // cb),
        in_specs=[pl.BlockSpec((1, r, cb * D_FOURIER), lambda bb, j: (bb, 0, j)),
                  _const_spec(tabs["f_rows"].shape),
                  pl.BlockSpec((cb, r, HEAD_PAD), lambda bb, j: (j, 0, 0)),
                  pl.BlockSpec((cb, r, HEAD_PAD), lambda bb, j: (j, 0, 0))],
        out_specs=pl.BlockSpec((1, 2, cb, r, D_FOURIER), lambda bb, j: (bb, 0, j, 0, 0)),
        out_shape=jax.ShapeDtypeStruct((b, 2, DFT_COLS, r, D_FOURIER), BF16),
        compiler_params=_cparams(("parallel", "parallel")),
        name="dft_rows",
    )(xf.reshape(b, r, DFT_COLS * D_FOURIER), tabs["f_rows"], tabs["tw_cos"], tabs["tw_sin"])
    out = pl.pallas_call(
        functools.partial(_dft_cols_kernel, scale=1.0 / math.sqrt(t * D_FG)),
        grid=(b, r // kb),
        in_specs=[pl.BlockSpec((1, 2, DFT_COLS, kb * D_FOURIER), lambda bb, j: (bb, 0, 0, j)),
                  _const_spec(tabs["m_cols"].shape),
                  _const_spec(tabs["cs_chan"].shape)],
        out_specs=pl.BlockSpec((1, DFT_COLS, kb * D_FOURIER), lambda bb, j: (bb, 0, j)),
        out_shape=jax.ShapeDtypeStruct((b, DFT_COLS, r * D_FOURIER), BF16),
        compiler_params=_cparams(("parallel", "parallel")),
        name="dft_cols",
    )(rows.reshape(b, 2, DFT_COLS, r * D_FOURIER), tabs["m_cols"], tabs["cs_chan"])
    return out.reshape(b, t, D_FOURIER)


def _cos_sin(num, den):
    ang = (num % den).astype(F32) * (2.0 * math.pi / den)
    return jnp.cos(ang), jnp.sin(ang)


def _factored_dft_tables(t):
    r = t // DFT_COLS
    i_r = jnp.arange(r, dtype=jnp.int32)
    i_c = jnp.arange(DFT_COLS, dtype=jnp.int32)
    i_g = jnp.arange(D_FG, dtype=jnp.int32)
    c_r, s_r = _cos_sin(i_r[:, None] * i_r[None, :], r)
    c_t, s_t = _cos_sin(i_c[:, None] * i_r[None, :], t)
    c_c, s_c = _cos_sin(i_c[:, None] * i_c[None, :], DFT_COLS)
    c_g, s_g = _cos_sin(i_g[:, None] * i_g[None, :], D_FG)
    lanes = lambda a: jnp.broadcast_to(a[:, :, None], (DFT_COLS, r, HEAD_PAD))
    return {
        "f_rows": jnp.concatenate([c_r, -s_r], axis=0).astype(BF16),
        "tw_cos": lanes(c_t), "tw_sin": lanes(s_t),
        "m_cols": jnp.block([[c_c, s_c], [-s_c, c_c]]).astype(BF16),
        "cs_chan": jnp.concatenate([c_g, s_g], axis=0).astype(BF16),
    }


def _matmul_kernel(a_ref, b_ref, o_ref, acc_ref, *, scale):
    k = pl.program_id(2)

    @pl.when(k == 0)
    def _():
        acc_ref[...] = jnp.zeros_like(acc_ref)

    acc_ref[...] += _dot(a_ref[...], b_ref[...])

    @pl.when(k == pl.num_programs(2) - 1)
    def _():
        o_ref[...] = (acc_ref[...] * scale).astype(o_ref.dtype)


def _matmul(a, b, scale, out_dtype):
    m, kk = a.shape
    n = b.shape[1]
    bm, bn, bk = _tile(m, 1024), _tile(n, 1024), _tile(kk, 2048)
    return pl.pallas_call(
        functools.partial(_matmul_kernel, scale=scale),
        grid=(m // bm, n // bn, kk // bk),
        in_specs=[pl.BlockSpec((bm, bk), lambda i, j, k: (i, k)),
                  pl.BlockSpec((bk, bn), lambda i, j, k: (k, j))],
        out_specs=pl.BlockSpec((bm, bn), lambda i, j, k: (i, j)),
        out_shape=jax.ShapeDtypeStruct((m, n), out_dtype),
        scratch_shapes=[pltpu.VMEM((bm, bn), F32)],
        compiler_params=_cparams(("parallel", "parallel", "arbitrary")),
        name="dft_matmul",
    )(a, b)


def _even_out_kernel(h_ref, mod_ref, g_ref, gb_ref, z_ref, zp_ref, zn_ref, yf_ref, cw_ref, w_ref, o_ref):
    i = pl.program_id(1)
    x = h_ref[0]
    mod = mod_ref[0]
    g = g_ref[...]
    z = z_ref[0].astype(F32)
    tm = z.shape[0]
    halo = zp_ref.shape[1]
    prev_row = jnp.where(i > 0, zp_ref[0, halo - 1:halo, :].astype(F32), 0.0)
    next_row = jnp.where(i < pl.num_programs(1) - 1, zn_ref[0, 0:1, :].astype(F32), 0.0)
    row = lax.broadcasted_iota(jnp.int32, z.shape, 0)
    z_before = jnp.where(row == 0, prev_row, pltpu.roll(z, 1, 0))
    z_after = jnp.where(row == tm - 1, next_row, pltpu.roll(z, tm - 1, 0))
    cw = cw_ref[...]
    conv = z_before * cw[0:1] + z * cw[1:2] + z_after * cw[2:3]
    y_conv = (gb_ref[0].astype(F32) * conv).astype(BF16)
    y = _dot(y_conv, w_ref[:D_CONV, :]) + _dot(yf_ref[0], w_ref[D_CONV:, :])
    o_ref[0] = x + mod[5:6] * _rms_rows(y, g[3:4])


def _even_out(h, mod, g, gb, z, yf, conv_w, w_out):
    b, t, d = h.shape
    tm = _tile(t, 1024)
    halo = BF16_SUBLANES
    nh = tm // halo
    last_halo = t // halo - 1
    return pl.pallas_call(
        _even_out_kernel,
        grid=(b, t // tm),
        in_specs=[pl.BlockSpec((1, tm, d), lambda bb, i: (bb, i, 0)),
                  _mod_spec(mod),
                  _const_spec(g.shape),
                  pl.BlockSpec((1, tm, D_CONV), lambda bb, i: (bb, i, 0)),
                  pl.BlockSpec((1, tm, D_CONV), lambda bb, i: (bb, i, 0)),
                  pl.BlockSpec((1, halo, D_CONV), lambda bb, i: (bb, jnp.maximum(i * nh - 1, 0), 0)),
                  pl.BlockSpec((1, halo, D_CONV), lambda bb, i: (bb, jnp.minimum((i + 1) * nh, last_halo), 0)),
                  pl.BlockSpec((1, tm, D_FOURIER), lambda bb, i: (bb, i, 0)),
                  _const_spec(conv_w.shape),
                  _const_spec(w_out.shape)],
        out_specs=pl.BlockSpec((1, tm, d), lambda bb, i: (bb, i, 0)),
        out_shape=jax.ShapeDtypeStruct(h.shape, F32),
        compiler_params=_cparams(("parallel", "parallel")),
        name="even_out",
    )(h, mod, g, gb, z, z, z, yf, conv_w, w_out)


def _dft_tables(t):
    n = jnp.arange(t, dtype=jnp.int32)
    ang = ((n[:, None] * n[None, :]) % t).astype(F32) * (2.0 * math.pi / t)
    w_pos = jnp.concatenate([jnp.cos(ang), -jnp.sin(ang)], axis=1).astype(BF16)
    return w_pos


def _channel_dft_table():
    n = jnp.arange(D_FG, dtype=jnp.int32)
    ang = ((n[:, None] * n[None, :]) % D_FG).astype(F32) * (2.0 * math.pi / D_FG)
    return jnp.concatenate([jnp.cos(ang), jnp.sin(ang)], axis=1).astype(BF16)


def _even_mixer(h, mod, g, w_in, conv_w, w_out, dft_c, dft_pos):
    b, t, _ = h.shape
    if isinstance(dft_pos, dict):
        gb, z, xf = _even_in(h, mod, g, w_in, dft_c, False)
        yf = _factored_dft(xf, dft_pos)
    else:
        gb, z, a = _even_in(h, mod, g, w_in, dft_c, True)
        yf = _matmul(dft_pos, a.reshape(2 * t, b * D_FOURIER), 1.0 / math.sqrt(t * D_FG), BF16)
        yf = yf.reshape(t, b, D_FOURIER).transpose(1, 0, 2)
    return _even_out(h, mod, g, gb, z, yf, conv_w, w_out)


ODD_IN_INPUTS = 20


def _odd_in_kernel(*refs, with_q, split8, tail_step):
    if not tail_step:
        _odd_in_body(*refs, with_q=with_q, split8=split8)
        return
    step, last = pl.program_id(1), pl.num_programs(1) - 1

    @pl.when(step < last)
    def _():
        _odd_in_body(*refs, with_q=with_q, split8=split8)

    @pl.when(step == last)
    def _():
        for ref in refs[ODD_IN_INPUTS:ODD_IN_INPUTS + N_KV_ARRAYS]:
            ref[...] = jnp.zeros(ref.shape, ref.dtype)


def _odd_in_body(h_ref, mod_ref, g_ref, wtok_ref, wt_ref, wukp_ref, wuvt_ref, wuqt_ref, place_ref, gsel_ref,
                 gkv_row_ref, gkv_col_ref, gq_col_ref,
                 cosk_ref, sink_ref, tkr_ref, cosq_ref, sinq_ref, cosm_ref, sinm_ref,
                 khd_ref, khm_ref, vtd_ref, vtm_ref, kn_ref, *q_refs, with_q, split8):
    xm = _modulated(h_ref[0], mod_ref[0], g_ref[...], 1).astype(BF16)
    tm = xm.shape[0]

    ut = _dot(xm, wtok_ref[...])
    k = ut[:, :DA_QCOLS]
    ckv = ut[:, DA_QCOLS:DA_QCOLS + MLA_KV_RANK]
    kr2 = ut[:, DA_QCOLS + MLA_KV_RANK:]
    lane = lax.broadcasted_iota(jnp.int32, k.shape, 1)
    first_half = (lane % (DA_DK // 2)) < (DA_DK // 4)
    k_sw = jnp.where(first_half, pltpu.roll(k, DA_QCOLS - DA_DK // 4, 1), pltpu.roll(k, DA_DK // 4, 1))
    reps = DA_QCOLS // HEAD_PAD
    cosk = jnp.concatenate([cosk_ref[...]] * reps, axis=1)
    sink = jnp.concatenate([sink_ref[...]] * reps, axis=1)
    k_rot = k * cosk + k_sw * sink
    ckvn = _rms_rows(ckv, gkv_row_ref[...]).astype(BF16)
    k_nope = _dot(ckvn, wukp_ref[...])
    pr = kr2 * tkr_ref[...]
    pr_hi = pr.astype(BF16)
    pr_lo = (pr - pr_hi.astype(F32)).astype(BF16)
    k_mla = k_nope + _dot(pr_hi, place_ref[...]) + _dot(pr_lo, place_ref[...])
    lane_h = lax.broadcasted_iota(jnp.int32, (tm, HEAD_PAD), 1)
    low_half = lane_h < DA_DK
    to_low = functools.partial(pltpu.roll, shift=DA_QCOLS - DA_DK, axis=1)
    if split8:
        k_hi, k_lo = _split8(k_rot * DA_SPLIT_SCALE)
        k_eff = k_hi + k_lo
        lo_up, hi_dn = pltpu.roll(k_lo, DA_DK, 1), to_low(k_hi)
        for hh in range(DA_HEADS):
            tl = slice(hh * HEAD_PAD, (hh + 1) * HEAD_PAD)
            khd_ref[0, 2 * hh] = jnp.where(low_half, k_hi[:, tl], lo_up[:, tl]).astype(F8)
            khd_ref[0, 2 * hh + 1] = jnp.where(low_half, hi_dn[:, tl], k_lo[:, tl]).astype(F8)
    else:
        k_eff = k_rot.astype(BF16).astype(F32)
        k_dn = to_low(k_rot)
        for hh in range(DA_HEADS):
            tl = slice(hh * HEAD_PAD, (hh + 1) * HEAD_PAD)
            for m, src in enumerate((k_rot, k_dn)):
                khd_ref[0, 2 * hh + m] = jnp.where(lane_h == DA_DK, 1.0,
                                                   jnp.where(low_half, src[:, tl], 0.0)).astype(BF16)
    sq = [(k_eff * k_eff).astype(BF16)]
    for hh in range(MLA_HEADS):
        kd = k_mla[:, hh * HEAD_PAD:(hh + 1) * HEAD_PAD]
        khm_ref[0, hh] = jnp.where(lane_h == MLA_DQK, 1.0, kd).astype(BF16)
        kf = kd.astype(BF16).astype(F32)
        sq.append((kf * kf).astype(BF16))
    kn_ref[0] = _dot_nt(gsel_ref[...], jnp.concatenate(sq, axis=1))

    r0 = DA_VCOLS
    vckv = _dot_nt(wt_ref[:r0 + MLA_KV_RANK, :], xm)
    for hh in range(DA_HEADS):
        vtd_ref[0, hh] = vckv[hh * DA_DV:(hh + 1) * DA_DV].astype(BF16)
    ckv_t = vckv[r0:]
    ms = jnp.mean(ckv_t * ckv_t, axis=0, keepdims=True)
    ckvn_t = (ckv_t * lax.rsqrt(ms + EPS) * gkv_col_ref[...]).astype(BF16)
    vm_t = _dot(wuvt_ref[...], ckvn_t)
    for hh in range(MLA_HEADS):
        vtm_ref[0, hh] = vm_t[hh * MLA_DV:(hh + 1) * MLA_DV].astype(BF16)

    if with_q:
        qtd_ref, qtm_ref = q_refs
        r1 = r0 + MLA_KV_RANK
        q_t = _dot_nt(wt_ref[r1:, :], xm)
        cosq, sinq = cosq_ref[...], sinq_ref[...]
        qd = DA_DK // 4
        for mp in range(2 * DA_HEADS):
            q = q_t[mp * DA_DK:(mp + 1) * DA_DK]
            q_sw = jnp.concatenate([q[qd:2 * qd], q[:qd], q[3 * qd:], q[2 * qd:3 * qd]], axis=0)
            q_rot = q * cosq + q_sw * sinq
            if split8:
                q_hi, q_lo = _split8(q_rot * DA_SPLIT_SCALE)
                qtd_ref[0, mp, :DA_DK, :] = q_hi.astype(F8)
                qtd_ref[0, mp, DA_DK:, :] = q_lo.astype(F8)
            else:
                qtd_ref[0, mp // 2, (mp % 2) * DA_DK:(mp % 2 + 1) * DA_DK, :] = (
                    q_rot * (DA_SCALE * LOG2E)).astype(BF16)
        cq_t = q_t[DA_QCOLS:]
        ms = jnp.mean(cq_t * cq_t, axis=0, keepdims=True)
        cqn_t = (cq_t * lax.rsqrt(ms + EPS) * gq_col_ref[...]).astype(BF16)
        qm_t = _dot(wuqt_ref[...], cqn_t) * (MLA_SCALE * LOG2E)
        cosm, sinm = cosm_ref[...], sinm_ref[...]
        rd = MLA_ROPE // 4
        for hh in range(MLA_HEADS):
            base = hh * HEAD_PAD
            qtm_ref[0, hh, :MLA_NOPE, :] = qm_t[base:base + MLA_NOPE].astype(BF16)
            r = qm_t[base + MLA_NOPE:base + MLA_DQK]
            r_sw = jnp.concatenate([r[rd:2 * rd], r[:rd], r[3 * rd:], r[2 * rd:3 * rd]], axis=0)
            qtm_ref[0, hh, MLA_NOPE:MLA_DQK, :] = (r * cosm + r_sw * sinm).astype(BF16)
            qtm_ref[0, hh, MLA_DQK:, :] = jnp.zeros((HEAD_PAD - MLA_DQK, tm), BF16)


def _odd_in(h, mod, g, wts, tabs, kv_prev, tok_off, with_q, split8):
    b, t, d = h.shape
    da_dtype = F8 if split8 else BF16
    tk_total = tabs["tk_total"]
    tm = _tile(t, 512)
    assert tok_off % tm == 0
    off = tok_off // tm
    n_tok = t // tm
    tail_step = kv_prev is None and tk_total > tok_off + t
    assert not tail_step or tk_total - (tok_off + t) <= tm
    tok = (lambda i: jnp.minimum(i, n_tok - 1)) if tail_step else (lambda i: i)
    row_tab = lambda w: pl.BlockSpec((tm, w), lambda bb, i: (tok(i), 0))
    col_tab = lambda r: pl.BlockSpec((r, tm), lambda bb, i: (0, tok(i)))
    consts = [wts["wtok"], wts["wt"], wts["wukp"], wts["wuvt"], wts["wuqt"], wts["place"], wts["gsel"],
              wts["gkv_row"], wts["gkv_col"], wts["gq_col"]]
    in_specs = ([pl.BlockSpec((1, tm, d), lambda bb, i: (bb, tok(i), 0)), _mod_spec(mod), _const_spec(g.shape)]
                + [_const_spec(c.shape) for c in consts]
                + [row_tab(HEAD_PAD), row_tab(HEAD_PAD), row_tab(2 * MLA_ROPE),
                   col_tab(DA_DK), col_tab(DA_DK), col_tab(MLA_ROPE), col_tab(MLA_ROPE)])
    args = [h, mod, g] + consts + [tabs["cosk"], tabs["sink"], tabs["tkr"],
                                   tabs["cosq"], tabs["sinq"], tabs["cosm"], tabs["sinm"]]
    out_specs = [pl.BlockSpec((1, 2 * DA_HEADS, tm, HEAD_PAD), lambda bb, i: (bb, 0, i + off, 0)),
                 pl.BlockSpec((1, MLA_HEADS, tm, HEAD_PAD), lambda bb, i: (bb, 0, i + off, 0)),
                 pl.BlockSpec((1, DA_HEADS, DA_DV, tm), lambda bb, i: (bb, 0, 0, i + off)),
                 pl.BlockSpec((1, MLA_HEADS, MLA_DV, tm), lambda bb, i: (bb, 0, 0, i + off)),
                 pl.BlockSpec((1, N_MAPS, tm), lambda bb, i: (bb, 0, i + off))]
    out_shape = [jax.ShapeDtypeStruct((b, 2 * DA_HEADS, tk_total, HEAD_PAD), da_dtype),
                 jax.ShapeDtypeStruct((b, MLA_HEADS, tk_total, HEAD_PAD), BF16),
                 jax.ShapeDtypeStruct((b, DA_HEADS, DA_DV, tk_total), BF16),
                 jax.ShapeDtypeStruct((b, MLA_HEADS, MLA_DV, tk_total), BF16),
                 jax.ShapeDtypeStruct((b, N_MAPS, tk_total), F32)]
    if with_q:
        n_qd = 2 * DA_HEADS if split8 else DA_HEADS
        out_specs += [pl.BlockSpec((1, n_qd, HEAD_PAD, tm), lambda bb, i: (bb, 0, 0, tok(i))),
                      pl.BlockSpec((1, MLA_HEADS, HEAD_PAD, tm), lambda bb, i: (bb, 0, 0, tok(i)))]
        out_shape += [jax.ShapeDtypeStruct((b, n_qd, HEAD_PAD, t), da_dtype),
                      jax.ShapeDtypeStruct((b, MLA_HEADS, HEAD_PAD, t), BF16)]
    aliases = {}
    assert len(args) == ODD_IN_INPUTS
    kernel_fn = functools.partial(_odd_in_kernel, with_q=with_q, split8=split8, tail_step=tail_step)
    if kv_prev is not None:
        n_in = len(args)
        in_specs += [pl.BlockSpec(memory_space=pl.ANY)] * N_KV_ARRAYS
        args += list(kv_prev)
        aliases = {n_in + a: a for a in range(N_KV_ARRAYS)}
        kernel_fn = functools.partial(_odd_in_alias_kernel, n_in=n_in, with_q=with_q, split8=split8)
    return pl.pallas_call(
        kernel_fn,
        grid=(b, n_tok + tail_step),
        in_specs=in_specs,
        out_specs=out_specs,
        out_shape=out_shape,
        input_output_aliases=aliases,
        compiler_params=_cparams(("parallel", "arbitrary" if tail_step else "parallel")),
        name="odd_in",
    )(*args)


def _odd_in_alias_kernel(*refs, n_in, with_q, split8):
    _odd_in_body(*refs[:n_in], *refs[n_in + N_KV_ARRAYS:], with_q=with_q, split8=split8)


def _attn_kernel(lam_ref, gsub_ref, kn_ref, qtd_ref, qtm_ref, khd_ref, khm_ref, vtd_ref, vtm_ref,
                 otd_ref, otm_ref, lmin_ref, qad_ref, qam_ref, shift_ref, m_ref, l_ref, accd_ref, accm_ref,
                 *, lam_init, safe):
    j = pl.program_id(2)
    tq = qtd_ref.shape[-1]

    @pl.when(j == 0)
    def _():
        m_ref[...] = jnp.full_like(m_ref, NEG_BIG)
        l_ref[...] = jnp.zeros_like(l_ref)
        accd_ref[...] = jnp.zeros_like(accd_ref)
        accm_ref[...] = jnp.zeros_like(accm_ref)
        kmax = jnp.sqrt(jnp.max(kn_ref[0], axis=1, keepdims=True))
        shift_row = lax.broadcasted_iota(jnp.int32, (BF16_SUBLANES, tq), 0) == 0
        for idx in range(2 * DA_HEADS):
            if safe:
                q = qtd_ref[0, idx // 2, (idx % 2) * DA_DK:(idx % 2 + 1) * DA_DK, :]
                qad_ref[idx] = jnp.concatenate([q, jnp.zeros((HEAD_PAD - DA_DK, tq), BF16)], axis=0)
            else:
                q_hi, q_lo = qtd_ref[0, idx, :DA_DK, :], qtd_ref[0, idx, DA_DK:, :]
                qad_ref[idx] = jnp.concatenate([q_hi, q_hi, q_lo, q_lo], axis=0)
                qf = q_hi.astype(F32) + q_lo.astype(F32)
                qn = jnp.sqrt(jnp.sum(qf * qf, axis=0, keepdims=True))
                ok = (qn < F8_SAFE_MAX) & (kmax[idx:idx + 1] < F8_SAFE_MAX)
                shift_ref[idx] = jnp.where(ok, (BOUND_SLACK * kmax[idx:idx + 1]) * qn, jnp.inf)
        for hh in range(MLA_HEADS):
            idx = 2 * DA_HEADS + hh
            q = qtm_ref[0, hh, :MLA_DQK, :]
            if safe:
                shift_blk = jnp.zeros((BF16_SUBLANES, tq), BF16)
            else:
                qf = q.astype(F32)
                qn = jnp.sqrt(jnp.sum(qf * qf, axis=0, keepdims=True))
                shift = -(BOUND_SLACK * kmax[idx:idx + 1]) * qn
                shift_blk = jnp.where(shift_row, shift, 0.0).astype(BF16)
            pad = jnp.zeros((HEAD_PAD - MLA_DQK - BF16_SUBLANES, tq), BF16)
            qam_ref[hh] = jnp.concatenate([q, shift_blk, pad], axis=0)

    def softmax_pv(s, ln, idx, v_t, acc_ref, acc_idx):
        if safe:
            m_old = m_ref[idx, :, ln]
            m_new = jnp.maximum(m_old, jnp.max(s, axis=0, keepdims=True))
            alpha = jnp.exp2(m_old - m_new)
            m_ref[idx, :, ln] = m_new
            p = jnp.exp2(s - m_new)
            l_ref[idx, :, ln] = alpha * l_ref[idx, :, ln] + jnp.sum(p.reshape(-1, F32_SUBLANES, s.shape[1]), axis=0)
            acc_ref[acc_idx, :, ln] = acc_ref[acc_idx, :, ln] * alpha + _dot(v_t, p.astype(BF16))
        else:
            p = jnp.exp2(s)
            l_ref[idx, :, ln] += jnp.sum(p.reshape(-1, F32_SUBLANES, s.shape[1]), axis=0)
            acc_ref[acc_idx, :, ln] += _dot(v_t, p.astype(BF16))

    chunk = ATTN_Q_CHUNK if (not safe and tq % ATTN_Q_CHUNK == 0) else tq
    q_chunks = [slice(c, c + chunk) for c in range(0, tq, chunk)]
    da_per_iter = 1 if safe else DA_HEADS_PER_ITER
    mla_per_iter = 1 if safe else MLA_HEADS_PER_ITER

    def da_score(hh, m, ln):
        keys = khd_ref[0, 2 * hh + m]
        if safe:
            return _dot(keys, qad_ref[2 * hh + m, :, ln])
        return _dot(jnp.concatenate([keys, keys], axis=1), qad_ref[2 * hh + m, :, ln])

    def mla_score(hh, ln):
        return _dot(khm_ref[0, hh], qam_ref[hh, :, ln])

    def run_maps(scores, consume):
        s_next = scores[0]()
        for n in range(len(scores)):
            s_cur = s_next
            if n + 1 < len(scores):
                s_next = scores[n + 1]()
            consume[n](s_cur)

    def da_units(hh, scores, consume):
        for m in range(2):
            for ln in q_chunks:
                scores.append(functools.partial(da_score, hh, m, ln))
                consume.append(lambda s, m=m, ln=ln: softmax_pv(
                    s if safe else s - shift_ref[2 * hh + m, :, ln], ln, 2 * hh + m, vtd_ref[0, hh],
                    accd_ref, 2 * hh + m))

    def mla_units(hh, scores, consume):
        mchunk = MLA_Q_CHUNK if (not safe and tq % MLA_Q_CHUNK == 0) else tq
        for c in range(0, tq, mchunk):
            ln = slice(c, c + mchunk)
            scores.append(functools.partial(mla_score, hh, ln))
            consume.append(lambda s, ln=ln: softmax_pv(s, ln, 2 * DA_HEADS + hh, vtm_ref[0, hh], accm_ref, hh))

    def da_body(it, carry):
        scores, consume = [], []
        for u in range(da_per_iter):
            da_units(it * da_per_iter + u, scores, consume)
        run_maps(scores, consume)
        return carry

    def mla_body(it, carry):
        scores, consume = [], []
        for u in range(mla_per_iter):
            mla_units(it * mla_per_iter + u, scores, consume)
        run_maps(scores, consume)
        return carry

    lax.fori_loop(0, DA_HEADS // da_per_iter, da_body, 0)
    lax.fori_loop(0, MLA_HEADS // mla_per_iter, mla_body, 0)

    @pl.when(j == pl.num_programs(2) - 1)
    def _():
        lp = lam_ref[...]
        lam = (jnp.exp(jnp.sum(lp[0:1] * lp[1:2], axis=1, keepdims=True))
               - jnp.exp(jnp.sum(lp[2:3] * lp[3:4], axis=1, keepdims=True)) + lam_init)
        gsub = gsub_ref[...]
        lsum = [jnp.sum(l_ref[idx], axis=0, keepdims=True) for idx in range(N_MAPS)]
        linv = [1.0 / ls for ls in lsum]
        for hh in range(DA_HEADS):
            o = accd_ref[2 * hh] * linv[2 * hh] - accd_ref[2 * hh + 1] * (lam * linv[2 * hh + 1])
            ms = jnp.mean(o * o, axis=0, keepdims=True)
            otd_ref[0, hh] = (o * (lax.rsqrt(ms + EPS) * (1.0 - lam_init)) * gsub).astype(otd_ref.dtype)
        for hh in range(MLA_HEADS):
            otm_ref[0, hh] = (accm_ref[hh] * linv[2 * DA_HEADS + hh]).astype(otm_ref.dtype)
        lmin = lsum[0]
        for idx in range(1, N_MAPS):
            lmin = jnp.minimum(lmin, lsum[idx])
        lmin_ref[0, 0] = jnp.broadcast_to(jnp.min(lmin, axis=1, keepdims=True), lmin_ref.shape[2:])


def _attention_call(lam_p, gsub_col, qtd, qtm, kv, lam_init, tq, tk, kv_off, n_kv, safe):
    khd, khm, vtd, vtm, kn = kv
    b, n_qd, _, t = qtd.shape
    nq = t // tq
    assert kv_off % n_kv == 0
    qad_scratch = (pltpu.VMEM((2 * DA_HEADS, HEAD_PAD, tq), BF16) if safe
                   else pltpu.VMEM((2 * DA_HEADS, 4 * DA_DK, tq), F8))
    return pl.pallas_call(
        functools.partial(_attn_kernel, lam_init=lam_init, safe=safe),
        grid=(b, nq, n_kv),
        in_specs=[_const_spec(lam_p.shape),
                  _const_spec(gsub_col.shape),
                  pl.BlockSpec((1, N_MAPS, n_kv * tk), lambda bb, i, j: (bb, 0, kv_off // n_kv)),
                  pl.BlockSpec((1, n_qd, HEAD_PAD, tq), lambda bb, i, j: (bb, 0, 0, i)),
                  pl.BlockSpec((1, MLA_HEADS, HEAD_PAD, tq), lambda bb, i, j: (bb, 0, 0, i)),
                  pl.BlockSpec((1, 2 * DA_HEADS, tk, HEAD_PAD), lambda bb, i, j: (bb, 0, kv_off + j, 0)),
                  pl.BlockSpec((1, MLA_HEADS, tk, HEAD_PAD), lambda bb, i, j: (bb, 0, kv_off + j, 0)),
                  pl.BlockSpec((1, DA_HEADS, DA_DV, tk), lambda bb, i, j: (bb, 0, 0, kv_off + j)),
                  pl.BlockSpec((1, MLA_HEADS, MLA_DV, tk), lambda bb, i, j: (bb, 0, 0, kv_off + j))],
        out_specs=[pl.BlockSpec((1, DA_HEADS, DA_DV, tq), lambda bb, i, j: (bb, 0, 0, i)),
                   pl.BlockSpec((1, MLA_HEADS, MLA_DV, tq), lambda bb, i, j: (bb, 0, 0, i)),
                   pl.BlockSpec((1, 1, F32_SUBLANES, HEAD_PAD), lambda bb, i, j: (bb, i, 0, 0))],
        out_shape=[jax.ShapeDtypeStruct((b, DA_HEADS, DA_DV, t), BF16),
                   jax.ShapeDtypeStruct((b, MLA_HEADS, MLA_DV, t), BF16),
                   jax.ShapeDtypeStruct((b, nq, F32_SUBLANES, HEAD_PAD), F32)],
        scratch_shapes=[qad_scratch,
                        pltpu.VMEM((MLA_HEADS, HEAD_PAD, tq), BF16),
                        pltpu.VMEM((2 * DA_HEADS, 1, tq), F32),
                        pltpu.VMEM((N_MAPS, 1, tq), F32),
                        pltpu.VMEM((N_MAPS, F32_SUBLANES, tq), F32),
                        pltpu.VMEM((2 * DA_HEADS, DA_DV, tq), F32),
                        pltpu.VMEM((MLA_HEADS, MLA_DV, tq), F32)],
        compiler_params=_cparams(("parallel", "parallel", "arbitrary")),
        name="attention_safe" if safe else "attention",
    )(lam_p, gsub_col, kn, qtd, qtm, khd, khm, vtd, vtm)


def _odd_out_kernel(h_ref, mod_ref, g_ref, otd_ref, otm_ref, wt_ref, o_ref):
    y_t = _dot(wt_ref[:, :DA_VCOLS], otd_ref[0]) + _dot(wt_ref[:, DA_VCOLS:], otm_ref[0])
    y = y_t.T
    o_ref[0] = h_ref[0] + mod_ref[0][5:6] * _rms_rows(y, g_ref[3:4])


def _odd_out(h, mod, g, otd, otm, w_out_t):
    b, t, d = h.shape
    tm = _tile(t, 1024)
    otd = otd.reshape(b, DA_VCOLS, t)
    otm = otm.reshape(b, MLA_HEADS * MLA_DV, t)
    return pl.pallas_call(
        _odd_out_kernel,
        grid=(b, t // tm),
        in_specs=[pl.BlockSpec((1, tm, d), lambda bb, i: (bb, i, 0)),
                  _mod_spec(mod),
                  _const_spec(g.shape),
                  pl.BlockSpec((1, DA_VCOLS, tm), lambda bb, i: (bb, 0, i)),
                  pl.BlockSpec((1, MLA_HEADS * MLA_DV, tm), lambda bb, i: (bb, 0, i)),
                  _const_spec(w_out_t.shape)],
        out_specs=pl.BlockSpec((1, tm, d), lambda bb, i: (bb, i, 0)),
        out_shape=jax.ShapeDtypeStruct(h.shape, F32),
        compiler_params=_cparams(("parallel", "parallel")),
        name="odd_out",
    )(h, mod, g, otd, otm, w_out_t)


def _rope_angles(n_tokens, rot_dim):
    rows = n_tokens // GRID_W
    row = jnp.broadcast_to(jnp.arange(rows)[:, None], (rows, GRID_W)).reshape(-1).astype(F32)
    col = jnp.broadcast_to(jnp.arange(GRID_W)[None, :], (rows, GRID_W)).reshape(-1).astype(F32)
    n_freq = rot_dim // 4
    freqs = ROPE_THETA ** (-jnp.arange(n_freq, dtype=F32) / n_freq)
    return row[:, None] * freqs, col[:, None] * freqs


def _rope_cos_sin(n_tokens, rot_dim, identity):
    if identity:
        return jnp.ones((n_tokens, rot_dim), F32), jnp.zeros((n_tokens, rot_dim), F32)
    ar, ac = _rope_angles(n_tokens, rot_dim)
    cos = jnp.concatenate([jnp.cos(ar), jnp.cos(ar), jnp.cos(ac), jnp.cos(ac)], axis=1)
    sin = jnp.concatenate([-jnp.sin(ar), jnp.sin(ar), -jnp.sin(ac), jnp.sin(ac)], axis=1)
    return cos, sin


def _rope_tables(n_tokens, identity, tk_total):
    cd, sd = _rope_cos_sin(n_tokens, DA_DK, identity)
    cm, sm = _rope_cos_sin(n_tokens, MLA_ROPE, identity)
    return {
        "cosk": jnp.concatenate([cd, cd], axis=1), "sink": jnp.concatenate([sd, sd], axis=1),
        "tkr": jnp.concatenate([cm, sm], axis=1),
        "cosq": cd.T, "sinq": sd.T, "cosm": cm.T, "sinm": sm.T,
        "tk_total": tk_total,
    }


def _swap_perm(rot_dim):
    q = rot_dim // 4
    return jnp.concatenate([jnp.arange(q, 2 * q), jnp.arange(0, q), jnp.arange(3 * q, 4 * q), jnp.arange(2 * q, 3 * q)])


def _map_lane_selector():
    da = jnp.kron(jnp.eye(2 * DA_HEADS, dtype=F32), jnp.ones((1, DA_DK), F32))
    mla = jnp.kron(jnp.eye(MLA_HEADS, dtype=F32), jnp.ones((1, HEAD_PAD), F32))
    top = jnp.concatenate([da, jnp.zeros((2 * DA_HEADS, MLA_HEADS * HEAD_PAD), F32)], axis=1)
    bottom = jnp.concatenate([jnp.zeros((MLA_HEADS, DA_QCOLS), F32), mla], axis=1)
    return jnp.concatenate([top, bottom], axis=0).astype(BF16)


def _odd_weights(w_in, g_q, w_uq, g_kv, w_uk, w_uv):
    w_q = w_in[:, :DA_QCOLS]
    w_cq = w_in[:, DA_QCOLS:Q_COLS]
    w_k = w_in[:, Q_COLS:Q_COLS + DA_QCOLS]
    w_v = w_in[:, Q_COLS + DA_QCOLS:Q_COLS + DA_QCOLS + DA_VCOLS]
    w_ckv = w_in[:, Q_COLS + DA_QCOLS + DA_VCOLS:Q_COLS + DA_QCOLS + DA_VCOLS + MLA_KV_RANK]
    w_kr = w_in[:, Q_COLS + DA_QCOLS + DA_VCOLS + MLA_KV_RANK:]
    wtok = jnp.concatenate([w_k, w_ckv, w_kr, w_kr[:, _swap_perm(MLA_ROPE)]], axis=1).astype(BF16)
    wt = jnp.concatenate([w_v, w_ckv, w_q, w_cq], axis=1).T.astype(BF16)
    pad_k = jnp.zeros((MLA_KV_RANK, MLA_HEADS, HEAD_PAD), F32)
    wukp = pad_k.at[:, :, :MLA_NOPE].set(w_uk.reshape(MLA_KV_RANK, MLA_HEADS, MLA_NOPE))
    wukp = wukp.reshape(MLA_KV_RANK, MLA_HEADS * HEAD_PAD).astype(BF16)
    pad_q = jnp.zeros((MLA_Q_RANK, MLA_HEADS, HEAD_PAD), F32)
    wuqp = pad_q.at[:, :, :MLA_DQK].set(w_uq.reshape(MLA_Q_RANK, MLA_HEADS, MLA_DQK))
    wuqt = wuqp.reshape(MLA_Q_RANK, MLA_HEADS * HEAD_PAD).T.astype(BF16)
    eye = jnp.eye(MLA_ROPE, dtype=F32)
    place = jnp.zeros((2, MLA_ROPE, MLA_HEADS, HEAD_PAD), F32)
    place = place.at[:, :, :, MLA_NOPE:MLA_DQK].set(jnp.broadcast_to(eye[None, :, None, :], (2, MLA_ROPE, MLA_HEADS, MLA_ROPE)))
    place = place.reshape(2 * MLA_ROPE, MLA_HEADS * HEAD_PAD).astype(BF16)
    return {
        "wtok": wtok, "wt": wt, "wukp": wukp, "wuvt": w_uv.T.astype(BF16), "wuqt": wuqt, "place": place,
        "gsel": _map_lane_selector(),
        "gkv_row": g_kv.reshape(1, -1).astype(F32), "gkv_col": g_kv.reshape(-1, 1).astype(F32),
        "gq_col": g_q.reshape(-1, 1).astype(F32),
    }


def kernel(x, c, ctx, c_ctx, w_mod, b_mod, norm_g, w_ffn_in, w_ffn_out, w_in_even, conv_w, w_out_even,
           w_in_odd, g_q_mla, w_uq, g_kv_mla, w_uk, w_uv, lam_q1, lam_k1, lam_q2, lam_k2, g_subln, w_out_odd):
    b, t, d = x.shape
    tc = ctx.shape[1]
    depth = w_mod.shape[0]
    tk_total = t + tc

    rows = -(-(b + 1) // F32_SUBLANES) * F32_SUBLANES
    cond = jnp.zeros((rows, d), F32).at[:b].set(c).at[b].set(c_ctx)
    mod = _adaln(cond, w_mod, b_mod).reshape(depth, rows, N_MOD, d)

    dft_c = _channel_dft_table()
    factored = lambda n: n % (DFT_COLS * BF16_SUBLANES) == 0
    w_pos_x = _factored_dft_tables(t) if factored(t) else _dft_tables(t)
    w_pos_c = _factored_dft_tables(tc) if factored(tc) else _dft_tables(tc)
    tabs_x = _rope_tables(t, False, tk_total)
    tabs_c = _rope_tables(tc, True, tk_total)
    tq = _tile(t, 1024)
    tk = 768 if tk_total % 768 == 0 else _tile(tk_total, 512)
    assert t % tc == 0 and tk_total % tk == 0

    h, hc = x, ctx
    for l in range(depth):
        last = l == depth - 1
        odd = l % 2 == 1
        ctx_live = (not last) or odd
        g = norm_g[l]
        m_x, m_c = mod[l, :b], mod[l, b:b + 1]
        wi0, wo0 = w_ffn_in[l, 0].astype(BF16), w_ffn_out[l, 0].astype(BF16)
        h = _ffn(h, m_x, g, wi0, wo0, 0)
        if ctx_live:
            hc = _ffn(hc, m_c, g, wi0, wo0, 0)

        if not odd:
            e = l // 2
            w_in, w_out = w_in_even[e].astype(BF16), w_out_even[e].astype(BF16)
            h = _even_mixer(h, m_x, g, w_in, conv_w[e], w_out, dft_c, w_pos_x)
            if ctx_live:
                hc = _even_mixer(hc, m_c, g, w_in, conv_w[e], w_out, dft_c, w_pos_c)
        else:
            o = l // 2
            lam_init = 0.8 - 0.6 * math.exp(-0.3 * l)
            wts = _odd_weights(w_in_odd[o], g_q_mla[o], w_uq[o], g_kv_mla[o], w_uk[o], w_uv[o])
            lam_p = jnp.stack([lam_q1[o], lam_k1[o], lam_q2[o], lam_k2[o]]).astype(F32)
            gsub_col = g_subln[o].reshape(-1, 1).astype(F32)
            w_out_t = w_out_odd[o].T.astype(BF16)
            def latent_attention(safe, h=h, hc=hc, m_x=m_x, m_c=m_c, g=g, wts=wts, lam_p=lam_p,
                                 gsub_col=gsub_col, lam_init=lam_init):
                outs_x = _odd_in(h, m_x, g, wts, tabs_x, None, 0, True, not safe)
                kv = _odd_in(hc, m_c, g, wts, tabs_c, outs_x[:N_KV_ARRAYS], t, False, not safe)
                return _attention_call(lam_p, gsub_col, *outs_x[N_KV_ARRAYS:], kv, lam_init, tq, tk, 0,
                                       tk_total // tk, safe)

            otd, otm, lmin = latent_attention(False)
            otd, otm = lax.cond(jnp.min(lmin) >= MIN_DENOMINATOR, lambda: (otd, otm),
                                lambda: tuple(latent_attention(True)[:2]))
            if not last:
                outs_c = _odd_in(hc, m_c, g, wts, dict(tabs_c, tk_total=tc), None, 0, True, False)
                ocd, ocm = _attention_call(lam_p, gsub_col, *outs_c[N_KV_ARRAYS:], outs_c[:N_KV_ARRAYS],
                                           lam_init, tc, tc, 0, 1, True)[:2]
                hc = _odd_out(hc, m_c, g, ocd, ocm, w_out_t)
            h = _odd_out(h, m_x, g, otd, otm, w_out_t)

        wi1, wo1 = w_ffn_in[l, 1].astype(BF16), w_ffn_out[l, 1].astype(BF16)
        h = _ffn(h, m_x, g, wi1, wo1, 2)
        if not last:
            hc = _ffn(hc, m_c, g, wi1, wo1, 2)
    return h
```

```python
import functools
import math

import jax
import jax.numpy as jnp
from jax import lax
from jax.experimental import pallas as pl
from jax.experimental.pallas import tpu as pltpu

F32 = jnp.float32
BF16 = jnp.bfloat16

D_MODEL = 1024
GRID_W = 64
N_MOD = 9
FFN_RES = 0.5
EPS = 1e-6
ROPE_THETA = 10000.0

D_CONV = 512
D_FOURIER = 512
FOURIER_GROUPS = 4
D_FG = D_FOURIER // FOURIER_GROUPS

DA_HEADS = 8
DA_DK = 64
DA_DV = 128
DA_SCALE = DA_DK ** -0.5
MLA_HEADS = 8
MLA_NOPE = 64
MLA_ROPE = 32
MLA_DQK = MLA_NOPE + MLA_ROPE
MLA_DV = 64
MLA_Q_RANK = 384
MLA_KV_RANK = 256
MLA_SCALE = MLA_DQK ** -0.5
DA_QCOLS = DA_HEADS * 2 * DA_DK
DA_VCOLS = DA_HEADS * DA_DV
Q_COLS = DA_QCOLS + MLA_Q_RANK
HEAD_PAD = 128
F32_SUBLANES = 8
BF16_SUBLANES = 16
N_MAPS = 2 * DA_HEADS + MLA_HEADS
LOG2E = 1.4426950408889634
NEG_BIG = -1e30
N_KV_ARRAYS = 5
F8 = jnp.float8_e4m3fn
F8_SAFE_MAX = 400.0
DA_SPLIT_SCALE = math.sqrt(DA_SCALE * LOG2E)
FFN_ROW_SPLITS = 4
FFN_GROUP_ROWS = 256
DA_HEADS_PER_ITER = 4
MLA_HEADS_PER_ITER = 8
ATTN_Q_CHUNK = 256
DA_DV_AUG = DA_DV + BF16_SUBLANES
MLA_Q_CHUNK = 512
BOUND_SLACK = 1.02
MIN_DENOMINATOR = 2.0 ** -40

VMEM_LIMIT_V7X = 56 * 1024 * 1024


def _cparams(sem):
    return pltpu.CompilerParams(dimension_semantics=sem, vmem_limit_bytes=VMEM_LIMIT_V7X)


def _tile(n, pref):
    if n <= pref:
        return n
    t = pref - pref % 128
    while t >= 128:
        if n % t == 0:
            return t
        t -= 128
    return n


def _const_spec(shape):
    nd = len(shape)
    return pl.BlockSpec(shape, lambda *_: (0,) * nd, pipeline_mode=pl.Buffered(1))


def _mod_spec(mod):
    if mod.shape[0] == 1:
        return pl.BlockSpec((1, N_MOD, D_MODEL), lambda b, *_: (0, 0, 0))
    return pl.BlockSpec((1, N_MOD, D_MODEL), lambda b, *_: (b, 0, 0))


def _rms_rows(x, g):
    ms = jnp.mean(x * x, axis=-1, keepdims=True)
    return x * lax.rsqrt(ms + EPS) * g


def _modulated(x, mod, g, slot):
    shift = mod[3 * slot:3 * slot + 1]
    scale = mod[3 * slot + 1:3 * slot + 2]
    return _rms_rows(x, g[2 * slot:2 * slot + 1]) * (1.0 + scale) + shift


def _split8(x):
    hi = x.astype(F8).astype(F32)
    lo = (x - hi).astype(F8).astype(F32)
    return hi, lo


def _dot(a, b):
    return jnp.dot(a, b, preferred_element_type=F32)


def _dot_nt(a, b):
    return lax.dot_general(a, b, (((1,), (1,)), ((), ())), preferred_element_type=F32)


def _adaln_kernel(c_ref, w_ref, b_ref, o_ref):
    c = c_ref[...]
    a = c * jax.nn.sigmoid(c)
    o_ref[0] = jnp.dot(a, w_ref[0], preferred_element_type=F32,
                       precision=lax.Precision.HIGHEST) + b_ref[0]


def _adaln(cond, w_mod, b_mod):
    depth, d, n = w_mod.shape
    rows = cond.shape[0]
    tn = _tile(n, 1152)
    return pl.pallas_call(
        _adaln_kernel,
        grid=(depth, n // tn),
        in_specs=[pl.BlockSpec((rows, d), lambda l, j: (0, 0)),
                  pl.BlockSpec((1, d, tn), lambda l, j: (l, 0, j)),
                  pl.BlockSpec((1, 1, tn), lambda l, j: (l, 0, j))],
        out_specs=pl.BlockSpec((1, rows, tn), lambda l, j: (l, 0, j)),
        out_shape=jax.ShapeDtypeStruct((depth, rows, n), F32),
        compiler_params=_cparams(("parallel", "parallel")),
        name="adaln",
    )(cond, w_mod, b_mod.reshape(depth, 1, n))


def _ffn_kernel(h_ref, mod_ref, g_ref, win_ref, wout_ref, o_ref, *, slot, d_ff):
    mod = mod_ref[0]
    g = g_ref[...]
    res_gate = mod[3 * slot + 2:3 * slot + 3]
    tm = h_ref.shape[1]
    n_groups = max(1, tm // FFN_GROUP_ROWS)
    rows = tm // n_groups
    for part in range(n_groups):
        x = h_ref[0, part * rows:(part + 1) * rows, :]
        xm = _modulated(x, mod, g, slot).astype(BF16)
        gate = _dot(xm, win_ref[:, :d_ff])
        up = _dot(xm, win_ref[:, d_ff:])
        act = (gate * jax.nn.sigmoid(gate) * up).astype(BF16)
        y = _dot(act, wout_ref[...])
        o_ref[0, part * rows:(part + 1) * rows, :] = (
            x + FFN_RES * res_gate * _rms_rows(y, g[2 * slot + 1:2 * slot + 2]))


def _ffn(h, mod, g, w_in, w_out, slot):
    b, t, d = h.shape
    d_ff = w_out.shape[0]
    tm = _tile(t, FFN_ROW_SPLITS * FFN_GROUP_ROWS)
    return pl.pallas_call(
        functools.partial(_ffn_kernel, slot=slot, d_ff=d_ff),
        grid=(b, t // tm),
        in_specs=[pl.BlockSpec((1, tm, d), lambda bb, i: (bb, i, 0)),
                  _mod_spec(mod),
                  _const_spec(g.shape),
                  _const_spec(w_in.shape),
                  _const_spec(w_out.shape)],
        out_specs=pl.BlockSpec((1, tm, d), lambda bb, i: (bb, i, 0)),
        out_shape=jax.ShapeDtypeStruct(h.shape, F32),
        compiler_params=_cparams(("parallel", "parallel")),
        name="ffn",
    )(h, mod, g, w_in, w_out)


def _even_in_kernel(h_ref, mod_ref, g_ref, w_ref, dft_ref, gb_ref, z_ref, a_ref, *, channel_dft):
    xm = _modulated(h_ref[0], mod_ref[0], g_ref[...], 1).astype(BF16)
    u = _dot(xm, w_ref[...])
    gb_ref[0] = u[:, :D_CONV].astype(BF16)
    z_ref[0] = (u[:, D_CONV:2 * D_CONV] * u[:, 2 * D_CONV:3 * D_CONV]).astype(BF16)
    xf = u[:, 3 * D_CONV:].astype(BF16)
    if not channel_dft:
        a_ref[0] = xf
        return
    for gi in range(FOURIER_GROUPS):
        pq = _dot(xf[:, gi * D_FG:(gi + 1) * D_FG], dft_ref[...])
        a_ref[0, :, gi * D_FG:(gi + 1) * D_FG] = pq[:, :D_FG].astype(BF16)
        a_ref[1, :, gi * D_FG:(gi + 1) * D_FG] = pq[:, D_FG:].astype(BF16)


def _even_in(h, mod, g, w_in, dft_c, channel_dft):
    b, t, d = h.shape
    tm = _tile(t, 1024)
    if channel_dft:
        a_spec = pl.BlockSpec((2, tm, D_FOURIER), lambda bb, i: (0, i, bb))
        a_shape = jax.ShapeDtypeStruct((2, t, b * D_FOURIER), BF16)
    else:
        a_spec = pl.BlockSpec((1, tm, D_FOURIER), lambda bb, i: (bb, i, 0))
        a_shape = jax.ShapeDtypeStruct((b, t, D_FOURIER), BF16)
    return pl.pallas_call(
        functools.partial(_even_in_kernel, channel_dft=channel_dft),
        grid=(b, t // tm),
        in_specs=[pl.BlockSpec((1, tm, d), lambda bb, i: (bb, i, 0)),
                  _mod_spec(mod),
                  _const_spec(g.shape),
                  _const_spec(w_in.shape),
                  _const_spec(dft_c.shape)],
        out_specs=[pl.BlockSpec((1, tm, D_CONV), lambda bb, i: (bb, i, 0)),
                   pl.BlockSpec((1, tm, D_CONV), lambda bb, i: (bb, i, 0)),
                   a_spec],
        out_shape=[jax.ShapeDtypeStruct((b, t, D_CONV), BF16),
                   jax.ShapeDtypeStruct((b, t, D_CONV), BF16),
                   a_shape],
        compiler_params=_cparams(("parallel", "parallel")),
        name="even_in",
    )(h, mod, g, w_in, dft_c)


DFT_COLS = 64


def _dft_rows_kernel(x_ref, f_ref, tc_ref, ts_ref, o_ref):
    res = _dot(f_ref[...], x_ref[0])
    r = f_ref.shape[0] // 2
    reps = D_FOURIER // tc_ref.shape[2]
    for c in range(tc_ref.shape[0]):
        a_re = res[:r, c * D_FOURIER:(c + 1) * D_FOURIER]
        a_im = res[r:, c * D_FOURIER:(c + 1) * D_FOURIER]
        tc = jnp.concatenate([tc_ref[c]] * reps, axis=1)
        ts = jnp.concatenate([ts_ref[c]] * reps, axis=1)
        o_ref[0, 0, c] = (a_re * tc + a_im * ts).astype(BF16)
        o_ref[0, 1, c] = (a_im * tc - a_re * ts).astype(BF16)


def _dft_cols_kernel(b_ref, m_ref, cs_ref, o_ref, *, scale):
    z = _dot(m_ref[...], jnp.concatenate([b_ref[0, 0], b_ref[0, 1]], axis=0))
    kb = z.shape[1] // D_FOURIER
    for gi in range(FOURIER_GROUPS):
        lanes = [slice(k * D_FOURIER + gi * D_FG, k * D_FOURIER + (gi + 1) * D_FG) for k in range(kb)]
        z_re = jnp.concatenate([z[:DFT_COLS, ln] for ln in lanes], axis=0)
        z_im = jnp.concatenate([z[DFT_COLS:, ln] for ln in lanes], axis=0)
        y = _dot(jnp.concatenate([z_re, z_im], axis=1).astype(BF16), cs_ref[...]) * scale
        for k in range(kb):
            o_ref[0, :, lanes[k]] = y[k * DFT_COLS:(k + 1) * DFT_COLS].astype(o_ref.dtype)


def _factored_dft(xf, tabs):
    b, t, _ = xf.shape
    r = t // DFT_COLS
    cb = kb = F32_SUBLANES
    rows = pl.pallas_call(
        _dft_rows_kernel,
        grid=(b, DFT_COLS // cb),
        in_specs=[pl.BlockSpec((1, r, cb * D_FOURIER), lambda bb, j: (bb, 0, j)),
                  _const_spec(tabs["f_rows"].shape),
                  pl.BlockSpec((cb, r, HEAD_PAD), lambda bb, j: (j, 0, 0)),
                  pl.BlockSpec((cb, r, HEAD_PAD), lambda bb, j: (j, 0, 0))],
        out_specs=pl.BlockSpec((1, 2, cb, r, D_FOURIER), lambda bb, j: (bb, 0, j, 0, 0)),
        out_shape=jax.ShapeDtypeStruct((b, 2, DFT_COLS, r, D_FOURIER), BF16),
        compiler_params=_cparams(("parallel", "parallel")),
        name="dft_rows",
    )(xf.reshape(b, r, DFT_COLS * D_FOURIER), tabs["f_rows"], tabs["tw_cos"], tabs["tw_sin"])
    out = pl.pallas_call(
        functools.partial(_dft_cols_kernel, scale=1.0 / math.sqrt(t * D_FG)),
        grid=(b, r // kb),
        in_specs=[pl.BlockSpec((1, 2, DFT_COLS, kb * D_FOURIER), lambda bb, j: (bb, 0, 0, j)),
                  _const_spec(tabs["m_cols"].shape),
                  _const_spec(tabs["cs_chan"].shape)],
        out_specs=pl.BlockSpec((1, DFT_COLS, kb * D_FOURIER), lambda bb, j: (bb, 0, j)),
        out_shape=jax.ShapeDtypeStruct((b, DFT_COLS, r * D_FOURIER), BF16),
        compiler_params=_cparams(("parallel", "parallel")),
        name="dft_cols",
    )(rows.reshape(b, 2, DFT_COLS, r * D_FOURIER), tabs["m_cols"], tabs["cs_chan"])
    return out.reshape(b, t, D_FOURIER)


def _cos_sin(num, den):
    ang = (num % den).astype(F32) * (2.0 * math.pi / den)
    return jnp.cos(ang), jnp.sin(ang)


def _factored_dft_tables(t):
    r = t // DFT_COLS
    i_r = jnp.arange(r, dtype=jnp.int32)
    i_c = jnp.arange(DFT_COLS, dtype=jnp.int32)
    i_g = jnp.arange(D_FG, dtype=jnp.int32)
    c_r, s_r = _cos_sin(i_r[:, None] * i_r[None, :], r)
    c_t, s_t = _cos_sin(i_c[:, None] * i_r[None, :], t)
    c_c, s_c = _cos_sin(i_c[:, None] * i_c[None, :], DFT_COLS)
    c_g, s_g = _cos_sin(i_g[:, None] * i_g[None, :], D_FG)
    lanes = lambda a: jnp.broadcast_to(a[:, :, None], (DFT_COLS, r, HEAD_PAD))
    return {
        "f_rows": jnp.concatenate([c_r, -s_r], axis=0).astype(BF16),
        "tw_cos": lanes(c_t), "tw_sin": lanes(s_t),
        "m_cols": jnp.block([[c_c, s_c], [-s_c, c_c]]).astype(BF16),
        "cs_chan": jnp.concatenate([c_g, s_g], axis=0).astype(BF16),
    }


def _matmul_kernel(a_ref, b_ref, o_ref, acc_ref, *, scale):
    k = pl.program_id(2)

    @pl.when(k == 0)
    def _():
        acc_ref[...] = jnp.zeros_like(acc_ref)

    acc_ref[...] += _dot(a_ref[...], b_ref[...])

    @pl.when(k == pl.num_programs(2) - 1)
    def _():
        o_ref[...] = (acc_ref[...] * scale).astype(o_ref.dtype)


def _matmul(a, b, scale, out_dtype):
    m, kk = a.shape
    n = b.shape[1]
    bm, bn, bk = _tile(m, 1024), _tile(n, 1024), _tile(kk, 2048)
    return pl.pallas_call(
        functools.partial(_matmul_kernel, scale=scale),
        grid=(m // bm, n // bn, kk // bk),
        in_specs=[pl.BlockSpec((bm, bk), lambda i, j, k: (i, k)),
                  pl.BlockSpec((bk, bn), lambda i, j, k: (k, j))],
        out_specs=pl.BlockSpec((bm, bn), lambda i, j, k: (i, j)),
        out_shape=jax.ShapeDtypeStruct((m, n), out_dtype),
        scratch_shapes=[pltpu.VMEM((bm, bn), F32)],
        compiler_params=_cparams(("parallel", "parallel", "arbitrary")),
        name="dft_matmul",
    )(a, b)


def _even_out_kernel(h_ref, mod_ref, g_ref, gb_ref, z_ref, zp_ref, zn_ref, yf_ref, cw_ref, w_ref, o_ref):
    i = pl.program_id(1)
    x = h_ref[0]
    mod = mod_ref[0]
    g = g_ref[...]
    z = z_ref[0].astype(F32)
    tm = z.shape[0]
    halo = zp_ref.shape[1]
    prev_row = jnp.where(i > 0, zp_ref[0, halo - 1:halo, :].astype(F32), 0.0)
    next_row = jnp.where(i < pl.num_programs(1) - 1, zn_ref[0, 0:1, :].astype(F32), 0.0)
    row = lax.broadcasted_iota(jnp.int32, z.shape, 0)
    z_before = jnp.where(row == 0, prev_row, pltpu.roll(z, 1, 0))
    z_after = jnp.where(row == tm - 1, next_row, pltpu.roll(z, tm - 1, 0))
    cw = cw_ref[...]
    conv = z_before * cw[0:1] + z * cw[1:2] + z_after * cw[2:3]
    y_conv = (gb_ref[0].astype(F32) * conv).astype(BF16)
    y = _dot(y_conv, w_ref[:D_CONV, :]) + _dot(yf_ref[0], w_ref[D_CONV:, :])
    o_ref[0] = x + mod[5:6] * _rms_rows(y, g[3:4])


def _even_out(h, mod, g, gb, z, yf, conv_w, w_out):
    b, t, d = h.shape
    tm = _tile(t, 1024)
    halo = BF16_SUBLANES
    nh = tm // halo
    last_halo = t // halo - 1
    return pl.pallas_call(
        _even_out_kernel,
        grid=(b, t // tm),
        in_specs=[pl.BlockSpec((1, tm, d), lambda bb, i: (bb, i, 0)),
                  _mod_spec(mod),
                  _const_spec(g.shape),
                  pl.BlockSpec((1, tm, D_CONV), lambda bb, i: (bb, i, 0)),
                  pl.BlockSpec((1, tm, D_CONV), lambda bb, i: (bb, i, 0)),
                  pl.BlockSpec((1, halo, D_CONV), lambda bb, i: (bb, jnp.maximum(i * nh - 1, 0), 0)),
                  pl.BlockSpec((1, halo, D_CONV), lambda bb, i: (bb, jnp.minimum((i + 1) * nh, last_halo), 0)),
                  pl.BlockSpec((1, tm, D_FOURIER), lambda bb, i: (bb, i, 0)),
                  _const_spec(conv_w.shape),
                  _const_spec(w_out.shape)],
        out_specs=pl.BlockSpec((1, tm, d), lambda bb, i: (bb, i, 0)),
        out_shape=jax.ShapeDtypeStruct(h.shape, F32),
        compiler_params=_cparams(("parallel", "parallel")),
        name="even_out",
    )(h, mod, g, gb, z, z, z, yf, conv_w, w_out)


def _dft_tables(t):
    n = jnp.arange(t, dtype=jnp.int32)
    ang = ((n[:, None] * n[None, :]) % t).astype(F32) * (2.0 * math.pi / t)
    w_pos = jnp.concatenate([jnp.cos(ang), -jnp.sin(ang)], axis=1).astype(BF16)
    return w_pos


def _channel_dft_table():
    n = jnp.arange(D_FG, dtype=jnp.int32)
    ang = ((n[:, None] * n[None, :]) % D_FG).astype(F32) * (2.0 * math.pi / D_FG)
    return jnp.concatenate([jnp.cos(ang), jnp.sin(ang)], axis=1).astype(BF16)


def _even_mixer(h, mod, g, w_in, conv_w, w_out, dft_c, dft_pos):
    b, t, _ = h.shape
    if isinstance(dft_pos, dict):
        gb, z, xf = _even_in(h, mod, g, w_in, dft_c, False)
        yf = _factored_dft(xf, dft_pos)
    else:
        gb, z, a = _even_in(h, mod, g, w_in, dft_c, True)
        yf = _matmul(dft_pos, a.reshape(2 * t, b * D_FOURIER), 1.0 / math.sqrt(t * D_FG), BF16)
        yf = yf.reshape(t, b, D_FOURIER).transpose(1, 0, 2)
    return _even_out(h, mod, g, gb, z, yf, conv_w, w_out)


ODD_IN_INPUTS = 20


def _odd_in_kernel(*refs, with_q, split8, tail_step):
    if not tail_step:
        _odd_in_body(*refs, with_q=with_q, split8=split8)
        return
    step, last = pl.program_id(1), pl.num_programs(1) - 1

    @pl.when(step < last)
    def _():
        _odd_in_body(*refs, with_q=with_q, split8=split8)

    @pl.when(step == last)
    def _():
        for ref in refs[ODD_IN_INPUTS:ODD_IN_INPUTS + N_KV_ARRAYS]:
            ref[...] = jnp.zeros(ref.shape, ref.dtype)


def _odd_in_body(h_ref, mod_ref, g_ref, wtok_ref, wt_ref, wukp_ref, wuvt_ref, wuqt_ref, place_ref, gsel_ref,
                 gkv_row_ref, gkv_col_ref, gq_col_ref,
                 cosk_ref, sink_ref, tkr_ref, cosq_ref, sinq_ref, cosm_ref, sinm_ref,
                 khd_ref, khm_ref, vtd_ref, vtm_ref, kn_ref, *q_refs, with_q, split8):
    xm = _modulated(h_ref[0], mod_ref[0], g_ref[...], 1).astype(BF16)
    tm = xm.shape[0]

    ut = _dot(xm, wtok_ref[...])
    k = ut[:, :DA_QCOLS]
    ckv = ut[:, DA_QCOLS:DA_QCOLS + MLA_KV_RANK]
    kr2 = ut[:, DA_QCOLS + MLA_KV_RANK:]
    lane = lax.broadcasted_iota(jnp.int32, k.shape, 1)
    first_half = (lane % (DA_DK // 2)) < (DA_DK // 4)
    k_sw = jnp.where(first_half, pltpu.roll(k, DA_QCOLS - DA_DK // 4, 1), pltpu.roll(k, DA_DK // 4, 1))
    reps = DA_QCOLS // HEAD_PAD
    cosk = jnp.concatenate([cosk_ref[...]] * reps, axis=1)
    sink = jnp.concatenate([sink_ref[...]] * reps, axis=1)
    k_rot = k * cosk + k_sw * sink
    ckvn = _rms_rows(ckv, gkv_row_ref[...]).astype(BF16)
    k_nope = _dot(ckvn, wukp_ref[...])
    pr = kr2 * tkr_ref[...]
    pr_hi = pr.astype(BF16)
    pr_lo = (pr - pr_hi.astype(F32)).astype(BF16)
    k_mla = k_nope + _dot(pr_hi, place_ref[...]) + _dot(pr_lo, place_ref[...])
    lane_h = lax.broadcasted_iota(jnp.int32, (tm, HEAD_PAD), 1)
    low_half = lane_h < DA_DK
    to_low = functools.partial(pltpu.roll, shift=DA_QCOLS - DA_DK, axis=1)
    if split8:
        k_hi, k_lo = _split8(k_rot * DA_SPLIT_SCALE)
        k_eff = k_hi + k_lo
        lo_up, hi_dn = pltpu.roll(k_lo, DA_DK, 1), to_low(k_hi)
        for hh in range(DA_HEADS):
            tl = slice(hh * HEAD_PAD, (hh + 1) * HEAD_PAD)
            khd_ref[0, 2 * hh] = jnp.where(low_half, k_hi[:, tl], lo_up[:, tl]).astype(F8)
            khd_ref[0, 2 * hh + 1] = jnp.where(low_half, hi_dn[:, tl], k_lo[:, tl]).astype(F8)
    else:
        k_eff = k_rot.astype(BF16).astype(F32)
        k_dn = to_low(k_rot)
        for hh in range(DA_HEADS):
            tl = slice(hh * HEAD_PAD, (hh + 1) * HEAD_PAD)
            for m, src in enumerate((k_rot, k_dn)):
                khd_ref[0, 2 * hh + m] = jnp.where(lane_h == DA_DK, 1.0,
                                                   jnp.where(low_half, src[:, tl], 0.0)).astype(BF16)
    sq = [(k_eff * k_eff).astype(BF16)]
    for hh in range(MLA_HEADS):
        kd = k_mla[:, hh * HEAD_PAD:(hh + 1) * HEAD_PAD]
        khm_ref[0, hh] = jnp.where(lane_h == MLA_DQK, 1.0, kd).astype(BF16)
        kf = kd.astype(BF16).astype(F32)
        sq.append((kf * kf).astype(BF16))
    kn_ref[0] = _dot_nt(gsel_ref[...], jnp.concatenate(sq, axis=1))

    r0 = DA_VCOLS
    vckv = _dot_nt(wt_ref[:r0 + MLA_KV_RANK, :], xm)
    for hh in range(DA_HEADS):
        vtd_ref[0, hh, :DA_DV, :] = vckv[hh * DA_DV:(hh + 1) * DA_DV].astype(BF16)
        ones_row = lax.broadcasted_iota(jnp.int32, (BF16_SUBLANES, tm), 0) == 0
        vtd_ref[0, hh, DA_DV:, :] = jnp.where(ones_row, 1.0, 0.0).astype(BF16)
    ckv_t = vckv[r0:]
    ms = jnp.mean(ckv_t * ckv_t, axis=0, keepdims=True)
    ckvn_t = (ckv_t * lax.rsqrt(ms + EPS) * gkv_col_ref[...]).astype(BF16)
    vm_t = _dot(wuvt_ref[...], ckvn_t)
    for hh in range(MLA_HEADS):
        vtm_ref[0, hh] = vm_t[hh * MLA_DV:(hh + 1) * MLA_DV].astype(BF16)

    if with_q:
        qtd_ref, qtm_ref = q_refs
        r1 = r0 + MLA_KV_RANK
        q_t = _dot_nt(wt_ref[r1:, :], xm)
        cosq, sinq = cosq_ref[...], sinq_ref[...]
        qd = DA_DK // 4
        for mp in range(2 * DA_HEADS):
            q = q_t[mp * DA_DK:(mp + 1) * DA_DK]
            q_sw = jnp.concatenate([q[qd:2 * qd], q[:qd], q[3 * qd:], q[2 * qd:3 * qd]], axis=0)
            q_rot = q * cosq + q_sw * sinq
            if split8:
                q_hi, q_lo = _split8(q_rot * DA_SPLIT_SCALE)
                qtd_ref[0, mp, :DA_DK, :] = q_hi.astype(F8)
                qtd_ref[0, mp, DA_DK:, :] = q_lo.astype(F8)
            else:
                qtd_ref[0, mp // 2, (mp % 2) * DA_DK:(mp % 2 + 1) * DA_DK, :] = (
                    q_rot * (DA_SCALE * LOG2E)).astype(BF16)
        cq_t = q_t[DA_QCOLS:]
        ms = jnp.mean(cq_t * cq_t, axis=0, keepdims=True)
        cqn_t = (cq_t * lax.rsqrt(ms + EPS) * gq_col_ref[...]).astype(BF16)
        qm_t = _dot(wuqt_ref[...], cqn_t) * (MLA_SCALE * LOG2E)
        cosm, sinm = cosm_ref[...], sinm_ref[...]
        rd = MLA_ROPE // 4
        for hh in range(MLA_HEADS):
            base = hh * HEAD_PAD
            qtm_ref[0, hh, :MLA_NOPE, :] = qm_t[base:base + MLA_NOPE].astype(BF16)
            r = qm_t[base + MLA_NOPE:base + MLA_DQK]
            r_sw = jnp.concatenate([r[rd:2 * rd], r[:rd], r[3 * rd:], r[2 * rd:3 * rd]], axis=0)
            qtm_ref[0, hh, MLA_NOPE:MLA_DQK, :] = (r * cosm + r_sw * sinm).astype(BF16)
            qtm_ref[0, hh, MLA_DQK:, :] = jnp.zeros((HEAD_PAD - MLA_DQK, tm), BF16)


def _odd_in(h, mod, g, wts, tabs, kv_prev, tok_off, with_q, split8):
    b, t, d = h.shape
    da_dtype = F8 if split8 else BF16
    tk_total = tabs["tk_total"]
    tm = _tile(t, 512)
    assert tok_off % tm == 0
    off = tok_off // tm
    n_tok = t // tm
    tail_step = kv_prev is None and tk_total > tok_off + t
    assert not tail_step or tk_total - (tok_off + t) <= tm
    tok = (lambda i: jnp.minimum(i, n_tok - 1)) if tail_step else (lambda i: i)
    row_tab = lambda w: pl.BlockSpec((tm, w), lambda bb, i: (tok(i), 0))
    col_tab = lambda r: pl.BlockSpec((r, tm), lambda bb, i: (0, tok(i)))
    consts = [wts["wtok"], wts["wt"], wts["wukp"], wts["wuvt"], wts["wuqt"], wts["place"], wts["gsel"],
              wts["gkv_row"], wts["gkv_col"], wts["gq_col"]]
    in_specs = ([pl.BlockSpec((1, tm, d), lambda bb, i: (bb, tok(i), 0)), _mod_spec(mod), _const_spec(g.shape)]
                + [_const_spec(c.shape) for c in consts]
                + [row_tab(HEAD_PAD), row_tab(HEAD_PAD), row_tab(2 * MLA_ROPE),
                   col_tab(DA_DK), col_tab(DA_DK), col_tab(MLA_ROPE), col_tab(MLA_ROPE)])
    args = [h, mod, g] + consts + [tabs["cosk"], tabs["sink"], tabs["tkr"],
                                   tabs["cosq"], tabs["sinq"], tabs["cosm"], tabs["sinm"]]
    out_specs = [pl.BlockSpec((1, 2 * DA_HEADS, tm, HEAD_PAD), lambda bb, i: (bb, 0, i + off, 0)),
                 pl.BlockSpec((1, MLA_HEADS, tm, HEAD_PAD), lambda bb, i: (bb, 0, i + off, 0)),
                 pl.BlockSpec((1, DA_HEADS, DA_DV_AUG, tm), lambda bb, i: (bb, 0, 0, i + off)),
                 pl.BlockSpec((1, MLA_HEADS, MLA_DV, tm), lambda bb, i: (bb, 0, 0, i + off)),
                 pl.BlockSpec((1, N_MAPS, tm), lambda bb, i: (bb, 0, i + off))]
    out_shape = [jax.ShapeDtypeStruct((b, 2 * DA_HEADS, tk_total, HEAD_PAD), da_dtype),
                 jax.ShapeDtypeStruct((b, MLA_HEADS, tk_total, HEAD_PAD), BF16),
                 jax.ShapeDtypeStruct((b, DA_HEADS, DA_DV_AUG, tk_total), BF16),
                 jax.ShapeDtypeStruct((b, MLA_HEADS, MLA_DV, tk_total), BF16),
                 jax.ShapeDtypeStruct((b, N_MAPS, tk_total), F32)]
    if with_q:
        n_qd = 2 * DA_HEADS if split8 else DA_HEADS
        out_specs += [pl.BlockSpec((1, n_qd, HEAD_PAD, tm), lambda bb, i: (bb, 0, 0, tok(i))),
                      pl.BlockSpec((1, MLA_HEADS, HEAD_PAD, tm), lambda bb, i: (bb, 0, 0, tok(i)))]
        out_shape += [jax.ShapeDtypeStruct((b, n_qd, HEAD_PAD, t), da_dtype),
                      jax.ShapeDtypeStruct((b, MLA_HEADS, HEAD_PAD, t), BF16)]
    aliases = {}
    assert len(args) == ODD_IN_INPUTS
    kernel_fn = functools.partial(_odd_in_kernel, with_q=with_q, split8=split8, tail_step=tail_step)
    if kv_prev is not None:
        n_in = len(args)
        in_specs += [pl.BlockSpec(memory_space=pl.ANY)] * N_KV_ARRAYS
        args += list(kv_prev)
        aliases = {n_in + a: a for a in range(N_KV_ARRAYS)}
        kernel_fn = functools.partial(_odd_in_alias_kernel, n_in=n_in, with_q=with_q, split8=split8)
    return pl.pallas_call(
        kernel_fn,
        grid=(b, n_tok + tail_step),
        in_specs=in_specs,
        out_specs=out_specs,
        out_shape=out_shape,
        input_output_aliases=aliases,
        compiler_params=_cparams(("parallel", "arbitrary" if tail_step else "parallel")),
        name="odd_in",
    )(*args)


def _odd_in_alias_kernel(*refs, n_in, with_q, split8):
    _odd_in_body(*refs[:n_in], *refs[n_in + N_KV_ARRAYS:], with_q=with_q, split8=split8)


def _attn_kernel(lam_ref, gsub_ref, kn_ref, qtd_ref, qtm_ref, khd_ref, khm_ref, vtd_ref, vtm_ref,
                 otd_ref, otm_ref, lmin_ref, qad_ref, qam_ref, shift_ref, m_ref, l_ref, accd_ref, accm_ref,
                 *, lam_init, safe):
    j = pl.program_id(2)
    tq = qtd_ref.shape[-1]

    @pl.when(j == 0)
    def _():
        m_ref[...] = jnp.full_like(m_ref, NEG_BIG)
        l_ref[...] = jnp.zeros_like(l_ref)
        accd_ref[...] = jnp.zeros_like(accd_ref)
        accm_ref[...] = jnp.zeros_like(accm_ref)
        kmax = jnp.sqrt(jnp.max(kn_ref[0], axis=1, keepdims=True))
        shift_row = lax.broadcasted_iota(jnp.int32, (BF16_SUBLANES, tq), 0) == 0
        for idx in range(2 * DA_HEADS):
            if safe:
                q = qtd_ref[0, idx // 2, (idx % 2) * DA_DK:(idx % 2 + 1) * DA_DK, :]
                qad_ref[idx] = jnp.concatenate([q, jnp.zeros((HEAD_PAD - DA_DK, tq), BF16)], axis=0)
            else:
                q_hi, q_lo = qtd_ref[0, idx, :DA_DK, :], qtd_ref[0, idx, DA_DK:, :]
                qad_ref[idx] = jnp.concatenate([q_hi, q_hi, q_lo, q_lo], axis=0)
                qf = q_hi.astype(F32) + q_lo.astype(F32)
                qn = jnp.sqrt(jnp.sum(qf * qf, axis=0, keepdims=True))
                ok = (qn < F8_SAFE_MAX) & (kmax[idx:idx + 1] < F8_SAFE_MAX)
                shift_ref[idx] = jnp.where(ok, (BOUND_SLACK * kmax[idx:idx + 1]) * qn, jnp.inf)
        for hh in range(MLA_HEADS):
            idx = 2 * DA_HEADS + hh
            q = qtm_ref[0, hh, :MLA_DQK, :]
            if safe:
                shift_blk = jnp.zeros((BF16_SUBLANES, tq), BF16)
            else:
                qf = q.astype(F32)
                qn = jnp.sqrt(jnp.sum(qf * qf, axis=0, keepdims=True))
                shift = -(BOUND_SLACK * kmax[idx:idx + 1]) * qn
                shift_blk = jnp.where(shift_row, shift, 0.0).astype(BF16)
            pad = jnp.zeros((HEAD_PAD - MLA_DQK - BF16_SUBLANES, tq), BF16)
            qam_ref[hh] = jnp.concatenate([q, shift_blk, pad], axis=0)

    def softmax_pv(s, ln, idx, v_t, acc_ref, acc_idx, mxu_sum=False):
        if safe:
            m_old = m_ref[idx, :, ln]
            m_new = jnp.maximum(m_old, jnp.max(s, axis=0, keepdims=True))
            alpha = jnp.exp2(m_old - m_new)
            m_ref[idx, :, ln] = m_new
            p = jnp.exp2(s - m_new)
            l_ref[idx, :, ln] = alpha * l_ref[idx, :, ln] + jnp.sum(p.reshape(-1, F32_SUBLANES, s.shape[1]), axis=0)
            acc_ref[acc_idx, :, ln] = acc_ref[acc_idx, :, ln] * alpha + _dot(v_t, p.astype(BF16))
        else:
            p = jnp.exp2(s)
            if not mxu_sum:
                l_ref[idx, :, ln] += jnp.sum(p.reshape(-1, F32_SUBLANES, s.shape[1]), axis=0)
            acc_ref[acc_idx, :, ln] += _dot(v_t, p.astype(BF16))

    chunk = ATTN_Q_CHUNK if (not safe and tq % ATTN_Q_CHUNK == 0) else tq
    q_chunks = [slice(c, c + chunk) for c in range(0, tq, chunk)]
    da_per_iter = 1 if safe else DA_HEADS_PER_ITER
    mla_per_iter = 1 if safe else MLA_HEADS_PER_ITER

    def da_score(hh, m, ln):
        keys = khd_ref[0, 2 * hh + m]
        if safe:
            return _dot(keys, qad_ref[2 * hh + m, :, ln])
        return _dot(jnp.concatenate([keys, keys], axis=1), qad_ref[2 * hh + m, :, ln])

    def mla_score(hh, ln):
        return _dot(khm_ref[0, hh], qam_ref[hh, :, ln])

    def run_maps(scores, consume):
        s_next = scores[0]()
        for n in range(len(scores)):
            s_cur = s_next
            if n + 1 < len(scores):
                s_next = scores[n + 1]()
            consume[n](s_cur)

    def da_units(hh, scores, consume):
        for m in range(2):
            for ln in q_chunks:
                scores.append(functools.partial(da_score, hh, m, ln))
                consume.append(lambda s, m=m, ln=ln: softmax_pv(
                    s if safe else s - shift_ref[2 * hh + m, :, ln], ln, 2 * hh + m, vtd_ref[0, hh],
                    accd_ref, 2 * hh + m, mxu_sum=not safe))

    def mla_units(hh, scores, consume):
        mchunk = MLA_Q_CHUNK if (not safe and tq % MLA_Q_CHUNK == 0) else tq
        for c in range(0, tq, mchunk):
            ln = slice(c, c + mchunk)
            scores.append(functools.partial(mla_score, hh, ln))
            consume.append(lambda s, ln=ln: softmax_pv(s, ln, 2 * DA_HEADS + hh, vtm_ref[0, hh], accm_ref, hh))

    def da_body(it, carry):
        scores, consume = [], []
        for u in range(da_per_iter):
            da_units(it * da_per_iter + u, scores, consume)
        run_maps(scores, consume)
        return carry

    def mla_body(it, carry):
        scores, consume = [], []
        for u in range(mla_per_iter):
            mla_units(it * mla_per_iter + u, scores, consume)
        run_maps(scores, consume)
        return carry

    lax.fori_loop(0, DA_HEADS // da_per_iter, da_body, 0)
    lax.fori_loop(0, MLA_HEADS // mla_per_iter, mla_body, 0)

    @pl.when(j == pl.num_programs(2) - 1)
    def _():
        lp = lam_ref[...]
        lam = (jnp.exp(jnp.sum(lp[0:1] * lp[1:2], axis=1, keepdims=True))
               - jnp.exp(jnp.sum(lp[2:3] * lp[3:4], axis=1, keepdims=True)) + lam_init)
        gsub = gsub_ref[...]
        lsum = [jnp.sum(l_ref[idx], axis=0, keepdims=True) for idx in range(N_MAPS)]
        if not safe:
            for idx in range(2 * DA_HEADS):
                lsum[idx] = accd_ref[idx, DA_DV:DA_DV + 1, :]
        linv = [1.0 / ls for ls in lsum]
        for hh in range(DA_HEADS):
            o = (accd_ref[2 * hh, :DA_DV, :] * linv[2 * hh]
                 - accd_ref[2 * hh + 1, :DA_DV, :] * (lam * linv[2 * hh + 1]))
            ms = jnp.mean(o * o, axis=0, keepdims=True)
            otd_ref[0, hh] = (o * (lax.rsqrt(ms + EPS) * (1.0 - lam_init)) * gsub).astype(otd_ref.dtype)
        for hh in range(MLA_HEADS):
            otm_ref[0, hh] = (accm_ref[hh] * linv[2 * DA_HEADS + hh]).astype(otm_ref.dtype)
        lmin = lsum[0]
        for idx in range(1, N_MAPS):
            lmin = jnp.minimum(lmin, lsum[idx])
        lmin_ref[0, 0] = jnp.broadcast_to(jnp.min(lmin, axis=1, keepdims=True), lmin_ref.shape[2:])


def _attention_call(lam_p, gsub_col, qtd, qtm, kv, lam_init, tq, tk, kv_off, n_kv, safe):
    khd, khm, vtd, vtm, kn = kv
    b, n_qd, _, t = qtd.shape
    nq = t // tq
    assert kv_off % n_kv == 0
    kv_mode = {}
    qad_scratch = (pltpu.VMEM((2 * DA_HEADS, HEAD_PAD, tq), BF16) if safe
                   else pltpu.VMEM((2 * DA_HEADS, 4 * DA_DK, tq), F8))
    return pl.pallas_call(
        functools.partial(_attn_kernel, lam_init=lam_init, safe=safe),
        grid=(b, nq, n_kv),
        in_specs=[_const_spec(lam_p.shape),
                  _const_spec(gsub_col.shape),
                  pl.BlockSpec((1, N_MAPS, n_kv * tk), lambda bb, i, j: (bb, 0, kv_off // n_kv)),
                  pl.BlockSpec((1, n_qd, HEAD_PAD, tq), lambda bb, i, j: (bb, 0, 0, i)),
                  pl.BlockSpec((1, MLA_HEADS, HEAD_PAD, tq), lambda bb, i, j: (bb, 0, 0, i)),
                  pl.BlockSpec((1, 2 * DA_HEADS, tk, HEAD_PAD), lambda bb, i, j: (bb, 0, kv_off + j, 0), **kv_mode),
                  pl.BlockSpec((1, MLA_HEADS, tk, HEAD_PAD), lambda bb, i, j: (bb, 0, kv_off + j, 0), **kv_mode),
                  pl.BlockSpec((1, DA_HEADS, DA_DV_AUG, tk), lambda bb, i, j: (bb, 0, 0, kv_off + j), **kv_mode),
                  pl.BlockSpec((1, MLA_HEADS, MLA_DV, tk), lambda bb, i, j: (bb, 0, 0, kv_off + j), **kv_mode)],
        out_specs=[pl.BlockSpec((1, DA_HEADS, DA_DV, tq), lambda bb, i, j: (bb, 0, 0, i)),
                   pl.BlockSpec((1, MLA_HEADS, MLA_DV, tq), lambda bb, i, j: (bb, 0, 0, i)),
                   pl.BlockSpec((1, 1, F32_SUBLANES, HEAD_PAD), lambda bb, i, j: (bb, i, 0, 0))],
        out_shape=[jax.ShapeDtypeStruct((b, DA_HEADS, DA_DV, t), BF16),
                   jax.ShapeDtypeStruct((b, MLA_HEADS, MLA_DV, t), BF16),
                   jax.ShapeDtypeStruct((b, nq, F32_SUBLANES, HEAD_PAD), F32)],
        scratch_shapes=[qad_scratch,
                        pltpu.VMEM((MLA_HEADS, HEAD_PAD, tq), BF16),
                        pltpu.VMEM((2 * DA_HEADS, 1, tq), F32),
                        pltpu.VMEM((N_MAPS, 1, tq), F32),
                        pltpu.VMEM((N_MAPS, F32_SUBLANES, tq), F32),
                        pltpu.VMEM((2 * DA_HEADS, DA_DV_AUG, tq), F32),
                        pltpu.VMEM((MLA_HEADS, MLA_DV, tq), F32)],
        compiler_params=_cparams(("parallel", "parallel", "arbitrary")),
        name="attention_safe" if safe else "attention",
    )(lam_p, gsub_col, kn, qtd, qtm, khd, khm, vtd, vtm)


def _odd_out_kernel(h_ref, mod_ref, g_ref, otd_ref, otm_ref, wt_ref, o_ref):
    y_t = _dot(wt_ref[:, :DA_VCOLS], otd_ref[0]) + _dot(wt_ref[:, DA_VCOLS:], otm_ref[0])
    y = y_t.T
    o_ref[0] = h_ref[0] + mod_ref[0][5:6] * _rms_rows(y, g_ref[3:4])


def _odd_out(h, mod, g, otd, otm, w_out_t):
    b, t, d = h.shape
    tm = _tile(t, 1024)
    otd = otd.reshape(b, DA_VCOLS, t)
    otm = otm.reshape(b, MLA_HEADS * MLA_DV, t)
    return pl.pallas_call(
        _odd_out_kernel,
        grid=(b, t // tm),
        in_specs=[pl.BlockSpec((1, tm, d), lambda bb, i: (bb, i, 0)),
                  _mod_spec(mod),
                  _const_spec(g.shape),
                  pl.BlockSpec((1, DA_VCOLS, tm), lambda bb, i: (bb, 0, i)),
                  pl.BlockSpec((1, MLA_HEADS * MLA_DV, tm), lambda bb, i: (bb, 0, i)),
                  _const_spec(w_out_t.shape)],
        out_specs=pl.BlockSpec((1, tm, d), lambda bb, i: (bb, i, 0)),
        out_shape=jax.ShapeDtypeStruct(h.shape, F32),
        compiler_params=_cparams(("parallel", "parallel")),
        name="odd_out",
    )(h, mod, g, otd, otm, w_out_t)


def _rope_angles(n_tokens, rot_dim):
    rows = n_tokens // GRID_W
    row = jnp.broadcast_to(jnp.arange(rows)[:, None], (rows, GRID_W)).reshape(-1).astype(F32)
    col = jnp.broadcast_to(jnp.arange(GRID_W)[None, :], (rows, GRID_W)).reshape(-1).astype(F32)
    n_freq = rot_dim // 4
    freqs = ROPE_THETA ** (-jnp.arange(n_freq, dtype=F32) / n_freq)
    return row[:, None] * freqs, col[:, None] * freqs


def _rope_cos_sin(n_tokens, rot_dim, identity):
    if identity:
        return jnp.ones((n_tokens, rot_dim), F32), jnp.zeros((n_tokens, rot_dim), F32)
    ar, ac = _rope_angles(n_tokens, rot_dim)
    cos = jnp.concatenate([jnp.cos(ar), jnp.cos(ar), jnp.cos(ac), jnp.cos(ac)], axis=1)
    sin = jnp.concatenate([-jnp.sin(ar), jnp.sin(ar), -jnp.sin(ac), jnp.sin(ac)], axis=1)
    return cos, sin


def _rope_tables(n_tokens, identity, tk_total):
    cd, sd = _rope_cos_sin(n_tokens, DA_DK, identity)
    cm, sm = _rope_cos_sin(n_tokens, MLA_ROPE, identity)
    return {
        "cosk": jnp.concatenate([cd, cd], axis=1), "sink": jnp.concatenate([sd, sd], axis=1),
        "tkr": jnp.concatenate([cm, sm], axis=1),
        "cosq": cd.T, "sinq": sd.T, "cosm": cm.T, "sinm": sm.T,
        "tk_total": tk_total,
    }


def _swap_perm(rot_dim):
    q = rot_dim // 4
    return jnp.concatenate([jnp.arange(q, 2 * q), jnp.arange(0, q), jnp.arange(3 * q, 4 * q), jnp.arange(2 * q, 3 * q)])


def _map_lane_selector():
    da = jnp.kron(jnp.eye(2 * DA_HEADS, dtype=F32), jnp.ones((1, DA_DK), F32))
    mla = jnp.kron(jnp.eye(MLA_HEADS, dtype=F32), jnp.ones((1, HEAD_PAD), F32))
    top = jnp.concatenate([da, jnp.zeros((2 * DA_HEADS, MLA_HEADS * HEAD_PAD), F32)], axis=1)
    bottom = jnp.concatenate([jnp.zeros((MLA_HEADS, DA_QCOLS), F32), mla], axis=1)
    return jnp.concatenate([top, bottom], axis=0).astype(BF16)


def _odd_weights(w_in, g_q, w_uq, g_kv, w_uk, w_uv):
    w_q = w_in[:, :DA_QCOLS]
    w_cq = w_in[:, DA_QCOLS:Q_COLS]
    w_k = w_in[:, Q_COLS:Q_COLS + DA_QCOLS]
    w_v = w_in[:, Q_COLS + DA_QCOLS:Q_COLS + DA_QCOLS + DA_VCOLS]
    w_ckv = w_in[:, Q_COLS + DA_QCOLS + DA_VCOLS:Q_COLS + DA_QCOLS + DA_VCOLS + MLA_KV_RANK]
    w_kr = w_in[:, Q_COLS + DA_QCOLS + DA_VCOLS + MLA_KV_RANK:]
    wtok = jnp.concatenate([w_k, w_ckv, w_kr, w_kr[:, _swap_perm(MLA_ROPE)]], axis=1).astype(BF16)
    wt = jnp.concatenate([w_v, w_ckv, w_q, w_cq], axis=1).T.astype(BF16)
    pad_k = jnp.zeros((MLA_KV_RANK, MLA_HEADS, HEAD_PAD), F32)
    wukp = pad_k.at[:, :, :MLA_NOPE].set(w_uk.reshape(MLA_KV_RANK, MLA_HEADS, MLA_NOPE))
    wukp = wukp.reshape(MLA_KV_RANK, MLA_HEADS * HEAD_PAD).astype(BF16)
    pad_q = jnp.zeros((MLA_Q_RANK, MLA_HEADS, HEAD_PAD), F32)
    wuqp = pad_q.at[:, :, :MLA_DQK].set(w_uq.reshape(MLA_Q_RANK, MLA_HEADS, MLA_DQK))
    wuqt = wuqp.reshape(MLA_Q_RANK, MLA_HEADS * HEAD_PAD).T.astype(BF16)
    eye = jnp.eye(MLA_ROPE, dtype=F32)
    place = jnp.zeros((2, MLA_ROPE, MLA_HEADS, HEAD_PAD), F32)
    place = place.at[:, :, :, MLA_NOPE:MLA_DQK].set(jnp.broadcast_to(eye[None, :, None, :], (2, MLA_ROPE, MLA_HEADS, MLA_ROPE)))
    place = place.reshape(2 * MLA_ROPE, MLA_HEADS * HEAD_PAD).astype(BF16)
    return {
        "wtok": wtok, "wt": wt, "wukp": wukp, "wuvt": w_uv.T.astype(BF16), "wuqt": wuqt, "place": place,
        "gsel": _map_lane_selector(),
        "gkv_row": g_kv.reshape(1, -1).astype(F32), "gkv_col": g_kv.reshape(-1, 1).astype(F32),
        "gq_col": g_q.reshape(-1, 1).astype(F32),
    }


def kernel(x, c, ctx, c_ctx, w_mod, b_mod, norm_g, w_ffn_in, w_ffn_out, w_in_even, conv_w, w_out_even,
           w_in_odd, g_q_mla, w_uq, g_kv_mla, w_uk, w_uv, lam_q1, lam_k1, lam_q2, lam_k2, g_subln, w_out_odd):
    b, t, d = x.shape
    tc = ctx.shape[1]
    depth = w_mod.shape[0]
    tk_total = t + tc

    rows = -(-(b + 1) // F32_SUBLANES) * F32_SUBLANES
    cond = jnp.zeros((rows, d), F32).at[:b].set(c).at[b].set(c_ctx)
    mod = _adaln(cond, w_mod, b_mod).reshape(depth, rows, N_MOD, d)

    dft_c = _channel_dft_table()
    factored = lambda n: n % (DFT_COLS * BF16_SUBLANES) == 0
    w_pos_x = _factored_dft_tables(t) if factored(t) else _dft_tables(t)
    w_pos_c = _factored_dft_tables(tc) if factored(tc) else _dft_tables(tc)
    tabs_x = _rope_tables(t, False, tk_total)
    tabs_c = _rope_tables(tc, True, tk_total)
    tq = _tile(t, 1024)
    tk = 768 if tk_total % 768 == 0 else _tile(tk_total, 512)
    assert t % tc == 0 and tk_total % tk == 0

    h, hc = x, ctx
    for l in range(depth):
        last = l == depth - 1
        odd = l % 2 == 1
        ctx_live = (not last) or odd
        g = norm_g[l]
        m_x, m_c = mod[l, :b], mod[l, b:b + 1]
        wi0, wo0 = w_ffn_in[l, 0].astype(BF16), w_ffn_out[l, 0].astype(BF16)
        h = _ffn(h, m_x, g, wi0, wo0, 0)
        if ctx_live:
            hc = _ffn(hc, m_c, g, wi0, wo0, 0)

        if not odd:
            e = l // 2
            w_in, w_out = w_in_even[e].astype(BF16), w_out_even[e].astype(BF16)
            h = _even_mixer(h, m_x, g, w_in, conv_w[e], w_out, dft_c, w_pos_x)
            if ctx_live:
                hc = _even_mixer(hc, m_c, g, w_in, conv_w[e], w_out, dft_c, w_pos_c)
        else:
            o = l // 2
            lam_init = 0.8 - 0.6 * math.exp(-0.3 * l)
            wts = _odd_weights(w_in_odd[o], g_q_mla[o], w_uq[o], g_kv_mla[o], w_uk[o], w_uv[o])
            lam_p = jnp.stack([lam_q1[o], lam_k1[o], lam_q2[o], lam_k2[o]]).astype(F32)
            gsub_col = g_subln[o].reshape(-1, 1).astype(F32)
            w_out_t = w_out_odd[o].T.astype(BF16)
            def latent_attention(safe, h=h, hc=hc, m_x=m_x, m_c=m_c, g=g, wts=wts, lam_p=lam_p,
                                 gsub_col=gsub_col, lam_init=lam_init):
                outs_x = _odd_in(h, m_x, g, wts, tabs_x, None, 0, True, not safe)
                kv = _odd_in(hc, m_c, g, wts, tabs_c, outs_x[:N_KV_ARRAYS], t, False, not safe)
                return _attention_call(lam_p, gsub_col, *outs_x[N_KV_ARRAYS:], kv, lam_init, tq, tk, 0,
                                       tk_total // tk, safe)

            otd, otm, lmin = latent_attention(False)
            otd, otm = lax.cond(jnp.min(lmin) >= MIN_DENOMINATOR, lambda: (otd, otm),
                                lambda: tuple(latent_attention(True)[:2]))
            if not last:
                outs_c = _odd_in(hc, m_c, g, wts, dict(tabs_c, tk_total=tc), None, 0, True, False)
                ocd, ocm = _attention_call(lam_p, gsub_col, *outs_c[N_KV_ARRAYS:], outs_c[:N_KV_ARRAYS],
                                           lam_init, tc, tc, 0, 1, True)[:2]
                hc = _odd_out(hc, m_c, g, ocd, ocm, w_out_t)
            h = _odd_out(h, m_x, g, otd, otm, w_out_t)

        wi1, wo1 = w_ffn_in[l, 1].astype(BF16), w_ffn_out[l, 1].astype(BF16)
        h = _ffn(h, m_x, g, wi1, wo1, 2)
        if not last:
            hc = _ffn(hc, m_c, g, wi1, wo1, 2)
    return h
```
